```python
import math
import jax, jax.numpy as jnp
from jax import lax
import numpy as np

D_MODEL = 1024
BATCH = 2
SEQ = 8192
DEPTH = 2
DEC_BATCH = 128
DEC_SEQ = 8
PAST_LEN = 16384
PAGE_SIZE = 128

N_EVEN = (DEPTH + 1) // 2
N_ODD = DEPTH // 2
HALF_W = D_MODEL // 2
H_A = 4
DV_A = HALF_W // H_A
DK_A = DV_A // 2
RET_CHUNK = 128
RET_THETA = 10000.0
G_B = 4
DG_B = HALF_W // G_B
MLP_CHUNK = 128
H_C = 4
DK_C = 128
DV_C = HALF_W // H_C
HGRN_CHUNK = 64
HD_D = 64
HQ_D = HALF_W // HD_D
HKV_D = 2
WINDOW = 128
ROPE_THETA = 500000.0
ROT_DIM_D = HD_D // 4
N_MEM = 256
H_X = 4
HD_X = 128
D_FF = ((8 * D_MODEL // 3 + 255) // 256) * 256
CONV_W = 3
EPS = 1e-6

EVEN_SIZES = (H_A * DK_A, H_A * DK_A, H_A * DV_A, H_A * DV_A, G_B * DG_B, G_B * DG_B)
ODD_SIZES = (H_C * DK_C, H_C * DK_C, H_C * DV_C, H_C * DV_C, HQ_D * HD_D, HKV_D * HD_D, HKV_D * HD_D)
EVEN_OUT = H_A * DV_A + G_B * DG_B
ODD_OUT = H_C * DV_C + HQ_D * HD_D

kernel_name = 'hybrid_retention_gmlp_hgrn2_swa_step'

F32 = jnp.float32


def _split(t, sizes):
    idx, acc = [], 0
    for s in sizes[:-1]:
        acc += s
        idx.append(acc)
    return jnp.split(t, idx, axis=-1)


def _rms_norm(x, g):
    xf = x.astype(F32)
    y = xf * lax.rsqrt(jnp.mean(xf * xf, axis=-1, keepdims=True) + EPS)
    return (y * g).astype(x.dtype)


def _layer_norm(x, g):
    xf = x.astype(F32)
    mu = jnp.mean(xf, axis=-1, keepdims=True)
    xc = xf - mu
    y = xc * lax.rsqrt(jnp.mean(xc * xc, axis=-1, keepdims=True) + EPS)
    return (y * g).astype(x.dtype)


def _rope(x, pos, rot_dim, theta):
    half = rot_dim // 2
    inv = theta ** (-jnp.arange(half, dtype=F32) / half)
    ang = pos.astype(F32)[:, None] * inv[None, :]
    cos = jnp.cos(ang)[:, None, :]
    sin = jnp.sin(ang)[:, None, :]
    xf = x.astype(F32)
    x1 = xf[..., :half]
    x2 = xf[..., half:rot_dim]
    out = jnp.concatenate([x1 * cos - x2 * sin, x2 * cos + x1 * sin, xf[..., rot_dim:]], axis=-1)
    return out.astype(x.dtype)


def _retention(q, k, v, s0):
    B, L, H, _ = q.shape
    DV = v.shape[-1]
    c = math.gcd(L, RET_CHUNK)
    n = L // c
    lg = jnp.log1p(-jnp.exp2(-5.0 - jnp.arange(H, dtype=F32)))
    idx = jnp.arange(c, dtype=F32)
    rel = idx[:, None] - idx[None, :]
    dmask = jnp.where(rel >= 0, jnp.exp(rel[None] * lg[:, None, None]), 0.0)
    q_dec = jnp.exp((idx + 1.0)[None, :] * lg[:, None])[..., None]
    k_dec = jnp.exp((c - 1.0 - idx)[None, :] * lg[:, None])[..., None]
    c_dec = jnp.exp(c * lg)[:, None, None]

    def chunks(t):
        return t.astype(F32).reshape(B, n, c, H, t.shape[-1]).transpose(1, 0, 3, 2, 4)

    def step(S, inp):
        qc, kc, vc = inp
        a = jnp.einsum('bhnd,bhmd->bhnm', qc, kc) * dmask
        o = jnp.einsum('bhnm,bhmv->bhnv', a, vc) + jnp.einsum('bhnd,bhdv->bhnv', qc * q_dec, S)
        S = c_dec * S + jnp.einsum('bhmd,bhmv->bhdv', kc * k_dec, vc)
        return S, o

    S, o = lax.scan(step, s0.astype(F32), (chunks(q), chunks(k), chunks(v)))
    return o.transpose(1, 0, 3, 2, 4).reshape(B, L, H, DV), S


def _hgrn2(q, k, log_f, v, s0):
    B, L, H, _ = q.shape
    DV = v.shape[-1]
    c = math.gcd(L, HGRN_CHUNK)
    n = L // c
    tri = jnp.tril(jnp.ones((c, c), dtype=bool))[:, :, None]

    def chunks(t):
        return t.astype(F32).reshape(B, n, c, H, t.shape[-1]).transpose(1, 0, 3, 2, 4)

    def step(S, inp):
        qc, kc, gc, vc = inp
        b = jnp.cumsum(gc, axis=2)
        diff = jnp.where(tri, b[:, :, :, None, :] - b[:, :, None, :, :], -jnp.inf)
        a = jnp.einsum('bhnmd,bhnd->bhnm', jnp.exp(diff) * kc[:, :, None, :, :], qc)
        o = jnp.einsum('bhnm,bhmv->bhnv', a, vc) + jnp.einsum('bhnd,bhdv->bhnv', qc * jnp.exp(b), S)
        bl = b[:, :, -1:, :]
        S = jnp.exp(bl[:, :, 0, :, None]) * S + jnp.einsum('bhmd,bhmv->bhdv', kc * jnp.exp(bl - b), vc)
        return S, o

    S, o = lax.scan(step, s0.astype(F32), (chunks(q), chunks(k), chunks(log_f), chunks(v)))
    return o.transpose(1, 0, 3, 2, 4).reshape(B, L, H, DV), S


def _chunk_gate(u, v, w_s, b_s):
    B, L, G, DG = v.shape
    c = min(MLP_CHUNK, L)
    n = L // c
    w = jnp.tril(w_s[:, :c, :c])
    mixed = jnp.einsum('gts,bnsgd->bntgd', w, v.reshape(B, n, c, G, DG)) + b_s[:, :c].T[None, None, :, :, None]
    return u * mixed.reshape(B, L, G, DG)


def _sink_probs(s, mask, sink):
    s = jnp.where(mask, s, -jnp.inf)
    m = jnp.maximum(jnp.max(s, axis=-1, keepdims=True), sink)
    p = jnp.exp(s - m)
    return p / (jnp.sum(p, axis=-1, keepdims=True) + jnp.exp(sink - m))


def _swa_prompt(q, k, v, sinks):
    B, L = q.shape[:2]
    nb = L // WINDOW
    rep = HQ_D // HKV_D
    qb = q.astype(F32).reshape(B, nb, WINDOW, HKV_D, rep, HD_D)
    kb = k.astype(F32).reshape(B, nb, WINDOW, HKV_D, HD_D)
    vb = v.astype(F32).reshape(B, nb, WINDOW, HKV_D, HD_D)
    pad = jnp.zeros_like(kb[:, :1])
    kk = jnp.concatenate([jnp.concatenate([pad, kb[:, :-1]], axis=1), kb], axis=2)
    vv = jnp.concatenate([jnp.concatenate([pad, vb[:, :-1]], axis=1), vb], axis=2)
    s = jnp.einsum('bnqgrd,bnkgd->bngrqk', qb, kk) * (HD_D ** -0.5)
    qi = jnp.arange(WINDOW)[:, None]
    ki = jnp.arange(2 * WINDOW)[None, :] - WINDOW
    band = (ki <= qi) & (qi - ki < WINDOW)
    mask = band[None] & ((jnp.arange(nb)[:, None, None] > 0) | (ki >= 0)[None])
    p = _sink_probs(s, mask[None, :, None, None], sinks.astype(F32).reshape(HKV_D, rep)[None, None, :, :, None, None])
    o = jnp.einsum('bngrqk,bnkgd->bnqgrd', p, vv)
    return o.reshape(B, L, HQ_D * HD_D), k[:, L - WINDOW:], v[:, L - WINDOW:]


def _swa_sample(q, k, v, kbuf, vbuf, sinks):
    B, L = q.shape[:2]
    rep = HQ_D // HKV_D
    kk = jnp.concatenate([kbuf.astype(k.dtype), k], axis=1)
    vv = jnp.concatenate([vbuf.astype(v.dtype), v], axis=1)
    s = jnp.einsum('bqgrd,bkgd->bgrqk', q.astype(F32).reshape(B, L, HKV_D, rep, HD_D), kk.astype(F32)) * (HD_D ** -0.5)
    qi = jnp.arange(L)[:, None]
    ki = jnp.arange(WINDOW + L)[None, :] - WINDOW
    mask = (ki <= qi) & (qi - ki < WINDOW)
    p = _sink_probs(s, mask, sinks.astype(F32).reshape(HKV_D, rep)[None, :, :, None, None])
    o = jnp.einsum('bgrqk,bkgd->bqgrd', p, vv.astype(F32))
    return o.reshape(B, L, HQ_D * HD_D), kk[:, L:], vv[:, L:]


def _even_mixer(h, pos, s0, w_in, w_out, ret_gn_g, mlp_norm_g, w_s, b_s):
    B, L, _ = h.shape
    qa, ka, va, ga, ub, vb = _split(h @ w_in, EVEN_SIZES)
    qa = _rope(qa.reshape(B, L, H_A, DK_A), pos, DK_A, RET_THETA)
    ka = _rope(ka.reshape(B, L, H_A, DK_A), pos, DK_A, RET_THETA) * (DK_A ** -0.5)
    o_a, s_new = _retention(qa, ka, va.reshape(B, L, H_A, DV_A), s0)
    o_a = (jax.nn.silu(ga) * _layer_norm(o_a, ret_gn_g).reshape(B, L, H_A * DV_A).astype(h.dtype))
    u = jax.nn.gelu(ub, approximate=False).reshape(B, L, G_B, DG_B)
    v_rows = _layer_norm(jax.nn.gelu(vb, approximate=False).reshape(B, L, G_B, DG_B), mlp_norm_g)
    o_b = _chunk_gate(u, v_rows, w_s, b_s).reshape(B, L, G_B * DG_B).astype(h.dtype)
    y = jnp.concatenate([o_a, o_b], axis=-1) @ w_out
    return y, s_new, v_rows


def _odd_mixer(h, pos, s0, kbuf, vbuf, lb, w_in, w_out, onorm_g, qn_g, kn_g, sinks):
    B, L, _ = h.shape
    qc, fc, ic, gc, qd, kd, vd = _split(h @ w_in, ODD_SIZES)
    f = lb + (1.0 - lb) * jax.nn.sigmoid(fc.astype(F32))
    f = f.reshape(B, L, H_C, DK_C)
    q_c = jax.nn.silu(qc).reshape(B, L, H_C, DK_C)
    o_c, s_new = _hgrn2(q_c, 1.0 - f, jnp.log(f), ic.reshape(B, L, H_C, DV_C), s0)
    o_c = (_rms_norm(o_c, onorm_g).reshape(B, L, H_C * DV_C).astype(h.dtype) * jax.nn.silu(gc))
    q_d = _rope(_rms_norm(qd.reshape(B, L, HQ_D, HD_D), qn_g), pos, ROT_DIM_D, ROPE_THETA)
    k_d = _rope(_rms_norm(kd.reshape(B, L, HKV_D, HD_D), kn_g), pos, ROT_DIM_D, ROPE_THETA)
    v_d = vd.reshape(B, L, HKV_D, HD_D)
    if kbuf is None:
        o_d, k_new, v_new = _swa_prompt(q_d, k_d, v_d, sinks)
    else:
        o_d, k_new, v_new = _swa_sample(q_d, k_d, v_d, kbuf, vbuf, sinks)
    y = jnp.concatenate([o_c, o_d.astype(h.dtype)], axis=-1) @ w_out
    return y, s_new, k_new, v_new


def _mem_kv(mem, w_kv, kn_g):
    B = mem.shape[0]
    k, v = jnp.split(mem @ w_kv, 2, axis=-1)
    return _rms_norm(k.reshape(B, N_MEM, H_X, HD_X), kn_g), v.reshape(B, N_MEM, H_X, HD_X)


def _mem_attend(h, mk, mv, w_q, qn_g, w_o):
    B, L, _ = h.shape
    q = _rms_norm((h @ w_q).reshape(B, L, H_X, HD_X), qn_g)
    s = jnp.einsum('blhd,bmhd->bhlm', q.astype(F32), mk.astype(F32)) * (HD_X ** -0.5)
    p = jax.nn.softmax(s, axis=-1)
    o = jnp.einsum('bhlm,bmhd->blhd', p, mv.astype(F32)).astype(h.dtype)
    return o.reshape(B, L, H_X * HD_X) @ w_o


def _conv_ffn(h, c0, w_gate, w_up, conv_w, conv_b, w_down):
    L = h.shape[1]
    g = h @ w_gate
    u = h @ w_up
    gg = jnp.concatenate([c0.astype(g.dtype), g], axis=1)
    c = conv_b + sum(gg[:, j:j + L] * conv_w[j] for j in range(CONV_W))
    y = (jax.nn.gelu(c, approximate=False) * u).astype(h.dtype) @ w_down
    return y, gg[:, L:]


def _run_group(x, pos, mem, st, W):
    prompt = st is None
    B = x.shape[0]
    lbs = jnp.cumsum(jax.nn.softmax(W['hgrn_lb_logits'].astype(F32), axis=0), axis=0)
    out = {'ret': [], 'chunk_v': [], 'hgrn': [], 'swa_k': [], 'swa_v': [], 'mem_k': [], 'mem_v': [], 'conv': []}
    for l in range(DEPTH):
        j = l // 2
        h = _rms_norm(x, W['norm_mix_g'][l])
        if l % 2 == 0:
            s0 = jnp.zeros((B, H_A, DK_A, DV_A), F32) if prompt else st['ret'][j]
            y, s_new, v_rows = _even_mixer(h, pos, s0, W['ev_w_in'][j], W['ev_w_out'][j], W['ret_gn_g'][j],
                                           W['mlp_norm_g'][j], W['mlp_w_s'][j], W['mlp_b_s'][j])
            out['ret'].append(s_new)
            out['chunk_v'].append(v_rows)
        else:
            s0 = jnp.zeros((B, H_C, DK_C, DV_C), F32) if prompt else st['hgrn'][j]
            kbuf = None if prompt else st['swa_k'][j]
            vbuf = None if prompt else st['swa_v'][j]
            y, s_new, k_new, v_new = _odd_mixer(h, pos, s0, kbuf, vbuf, lbs[l] - lbs[0], W['od_w_in'][j],
                                                W['od_w_out'][j], W['hgrn_onorm_g'][j], W['swa_qnorm_g'][j],
                                                W['swa_knorm_g'][j], W['swa_sinks'][j])
            out['hgrn'].append(s_new)
            out['swa_k'].append(k_new)
            out['swa_v'].append(v_new)
        x = x + y
        h = _rms_norm(x, W['norm_mem_g'][l])
        if prompt:
            mk, mv = _mem_kv(mem, W['mem_w_kv'][l], W['mem_knorm_g'][l])
            out['mem_k'].append(mk)
            out['mem_v'].append(mv)
        else:
            mk, mv = st['mem_k'][l], st['mem_v'][l]
        x = x + _mem_attend(h, mk, mv, W['mem_w_q'][l], W['mem_qnorm_g'][l], W['mem_w_o'][l])
        h = _rms_norm(x, W['norm_ffn_g'][l])
        c0 = jnp.zeros((B, CONV_W - 1, D_FF), x.dtype) if prompt else st['conv'][l]
        y, c_new = _conv_ffn(h, c0, W['ffn_w_gate'][l], W['ffn_w_up'][l], W['ffn_conv_w'][l],
                             W['ffn_conv_b'][l], W['ffn_w_down'][l])
        out['conv'].append(c_new)
        x = x + y
    return x, {name: jnp.stack(rows) for name, rows in out.items() if rows}


def setup_inputs(seed: int = 0) -> dict:
    key = jax.random.key(seed)
    keys = iter(jax.random.split(key, 64))

    def nrm(shape, scale):
        return scale * jax.random.normal(next(keys), shape, F32)

    def gain(shape):
        return 1.0 + 0.05 * jax.random.normal(next(keys), shape, F32)

    ev_in = sum(EVEN_SIZES)
    od_in = sum(ODD_SIZES)
    return {
        'x_prompt': nrm((BATCH, SEQ, D_MODEL), 1.0),
        'x_sample': nrm((DEC_BATCH, DEC_SEQ, D_MODEL), 1.0),
        'state_ret': nrm((N_EVEN, DEC_BATCH, H_A, DK_A, DV_A), 1.0),
        'state_hgrn': nrm((N_ODD, DEC_BATCH, H_C, DK_C, DV_C), 1.0),
        'cache_swa_k': nrm((N_ODD, DEC_BATCH, WINDOW, HKV_D, HD_D), 1.0),
        'cache_swa_v': nrm((N_ODD, DEC_BATCH, WINDOW, HKV_D, HD_D), 1.0),
        'cache_mem_k': nrm((DEPTH, DEC_BATCH, N_MEM, H_X, HD_X), 1.0),
        'cache_mem_v': nrm((DEPTH, DEC_BATCH, N_MEM, H_X, HD_X), 1.0),
        'state_ffn_conv': nrm((DEPTH, DEC_BATCH, CONV_W - 1, D_FF), 1.0),
        'mem_prompt': nrm((BATCH, N_MEM, D_MODEL), 1.0),
        'norm_mix_g': gain((DEPTH, D_MODEL)),
        'norm_mem_g': gain((DEPTH, D_MODEL)),
        'norm_ffn_g': gain((DEPTH, D_MODEL)),
        'ev_w_in': nrm((N_EVEN, D_MODEL, ev_in), D_MODEL ** -0.5),
        'ev_w_out': nrm((N_EVEN, EVEN_OUT, D_MODEL), 0.5 * EVEN_OUT ** -0.5),
        'ret_gn_g': gain((N_EVEN, H_A, DV_A)),
        'mlp_norm_g': gain((N_EVEN, G_B, DG_B)),
        'mlp_w_s': nrm((N_EVEN, G_B, MLP_CHUNK, MLP_CHUNK), 0.5 * MLP_CHUNK ** -0.5),
        'mlp_b_s': gain((N_EVEN, G_B, MLP_CHUNK)),
        'od_w_in': nrm((N_ODD, D_MODEL, od_in), D_MODEL ** -0.5),
        'od_w_out': nrm((N_ODD, ODD_OUT, D_MODEL), 0.5 * ODD_OUT ** -0.5),
        'hgrn_lb_logits': nrm((DEPTH, H_C * DK_C), 0.5),
        'hgrn_onorm_g': gain((N_ODD, H_C, DV_C)),
        'swa_qnorm_g': gain((N_ODD, HD_D)),
        'swa_knorm_g': gain((N_ODD, HD_D)),
        'swa_sinks': nrm((N_ODD, HQ_D), 0.5),
        'mem_w_q': nrm((DEPTH, D_MODEL, H_X * HD_X), D_MODEL ** -0.5),
        'mem_w_kv': nrm((DEPTH, D_MODEL, 2 * H_X * HD_X), D_MODEL ** -0.5),
        'mem_qnorm_g': gain((DEPTH, HD_X)),
        'mem_knorm_g': gain((DEPTH, HD_X)),
        'mem_w_o': nrm((DEPTH, H_X * HD_X, D_MODEL), 0.5 * (H_X * HD_X) ** -0.5),
        'ffn_w_gate': nrm((DEPTH, D_MODEL, D_FF), D_MODEL ** -0.5),
        'ffn_w_up': nrm((DEPTH, D_MODEL, D_FF), D_MODEL ** -0.5),
        'ffn_conv_w': nrm((DEPTH, CONV_W, D_FF), CONV_W ** -0.5),
        'ffn_conv_b': nrm((DEPTH, D_FF), 0.02),
        'ffn_w_down': nrm((DEPTH, D_FF, D_MODEL), 0.5 * D_FF ** -0.5),
    }


def reference(x_prompt, x_sample, state_ret, state_hgrn, cache_swa_k, cache_swa_v, cache_mem_k, cache_mem_v,
              state_ffn_conv, mem_prompt, norm_mix_g, norm_mem_g, norm_ffn_g, ev_w_in, ev_w_out, ret_gn_g,
              mlp_norm_g, mlp_w_s, mlp_b_s, od_w_in, od_w_out, hgrn_lb_logits, hgrn_onorm_g, swa_qnorm_g,
              swa_knorm_g, swa_sinks, mem_w_q, mem_w_kv, mem_qnorm_g, mem_knorm_g, mem_w_o, ffn_w_gate,
              ffn_w_up, ffn_conv_w, ffn_conv_b, ffn_w_down):
    W = dict(norm_mix_g=norm_mix_g, norm_mem_g=norm_mem_g, norm_ffn_g=norm_ffn_g, ev_w_in=ev_w_in,
             ev_w_out=ev_w_out, ret_gn_g=ret_gn_g, mlp_norm_g=mlp_norm_g, mlp_w_s=mlp_w_s, mlp_b_s=mlp_b_s,
             od_w_in=od_w_in, od_w_out=od_w_out, hgrn_lb_logits=hgrn_lb_logits, hgrn_onorm_g=hgrn_onorm_g,
             swa_qnorm_g=swa_qnorm_g, swa_knorm_g=swa_knorm_g, swa_sinks=swa_sinks, mem_w_q=mem_w_q,
             mem_w_kv=mem_w_kv, mem_qnorm_g=mem_qnorm_g, mem_knorm_g=mem_knorm_g, mem_w_o=mem_w_o,
             ffn_w_gate=ffn_w_gate, ffn_w_up=ffn_w_up, ffn_conv_w=ffn_conv_w, ffn_conv_b=ffn_conv_b,
             ffn_w_down=ffn_w_down)
    pos_prompt = jnp.arange(SEQ, dtype=jnp.int32)
    pos_sample = PAST_LEN + jnp.arange(DEC_SEQ, dtype=jnp.int32)
    y_prompt, ns_p = _run_group(x_prompt, pos_prompt, mem_prompt, None, W)
    st = dict(ret=state_ret, hgrn=state_hgrn, swa_k=cache_swa_k, swa_v=cache_swa_v,
              mem_k=cache_mem_k, mem_v=cache_mem_v, conv=state_ffn_conv)
    y_sample, ns_s = _run_group(x_sample, pos_sample, None, st, W)
    ret_state_prompt = ns_p['ret']
    hgrn_state_prompt = ns_p['hgrn']
    swa_k_prompt = ns_p['swa_k']
    swa_v_prompt = ns_p['swa_v']
    mem_k_prompt = ns_p['mem_k']
    mem_v_prompt = ns_p['mem_v']
    ffn_conv_prompt = ns_p['conv']
    ret_state_sample = ns_s['ret']
    chunk_v_sample = ns_s['chunk_v']
    hgrn_state_sample = ns_s['hgrn']
    swa_k_sample = ns_s['swa_k']
    swa_v_sample = ns_s['swa_v']
    ffn_conv_sample = ns_s['conv']
    return (y_prompt, y_sample, ret_state_prompt, hgrn_state_prompt, swa_k_prompt, swa_v_prompt,
            mem_k_prompt, mem_v_prompt, ffn_conv_prompt, ret_state_sample, chunk_v_sample,
            hgrn_state_sample, swa_k_sample, swa_v_sample, ffn_conv_sample)
```

```python
import functools
import math

import jax
import jax.numpy as jnp
import numpy as np
from jax import lax
from jax.experimental import pallas as pl
from jax.experimental.pallas import tpu as pltpu

F32 = jnp.float32
BF16 = jnp.bfloat16

D_MODEL = 1024
DEPTH = 2
PAST_LEN = 16384
HALF_W = D_MODEL // 2
H_A, DV_A, DK_A = 4, 128, 64
RET_CHUNK = 128
RET_THETA = 10000.0
G_B, DG_B = 4, 128
MLP_CHUNK = 128
H_C, DK_C, DV_C = 4, 128, 128
HGRN_CHUNK = 64
HGRN_SUB = 8
HD_D, HQ_D, HKV_D = 64, 8, 2
WINDOW = 128
ROPE_THETA = 500000.0
ROT_DIM_D = HD_D // 4
N_MEM, H_X, HD_X = 256, 4, 128
D_FF = 2816
CONV_W = 3
EPS = 1e-6
EVEN_IN = 2 * H_A * DK_A + 2 * H_A * DV_A + 2 * G_B * DG_B
ODD_IN = 4 * H_C * DK_C + HQ_D * HD_D + 2 * HKV_D * HD_D
SQRT_HALF = float(np.sqrt(0.5))

VMEM_LIMIT_BYTES = 56 * 1024 * 1024


def _cparams(sem):
    return pltpu.CompilerParams(dimension_semantics=sem, vmem_limit_bytes=VMEM_LIMIT_BYTES)


def _rms(x, g):
    return x * lax.rsqrt(jnp.mean(x * x, axis=-1, keepdims=True) + EPS) * g


def _ln(x, g):
    mu = jnp.mean(x, axis=-1, keepdims=True)
    xc = x - mu
    return xc * lax.rsqrt(jnp.mean(xc * xc, axis=-1, keepdims=True) + EPS) * g


def _sigmoid(x):
    return 1.0 / (1.0 + jnp.exp(-x))


def _silu(x):
    return x * _sigmoid(x)


def _gelu(x):
    return 0.5 * x * (1.0 + lax.erf(x * SQRT_HALF))


def _dot(a, b):
    return jnp.dot(a.astype(BF16), b.astype(BF16), preferred_element_type=F32)


def _dot_nt(a, b):
    return lax.dot_general(a.astype(BF16), b.astype(BF16), (((1,), (1,)), ((), ())), preferred_element_type=F32)


def _dot_tn(a, b):
    return lax.dot_general(a.astype(BF16), b.astype(BF16), (((0,), (0,)), ((), ())), preferred_element_type=F32)


def _split3(x):
    hi = x.astype(BF16)
    r1 = x - hi.astype(F32)
    mid = r1.astype(BF16)
    lo = (r1 - mid.astype(F32)).astype(BF16)
    return hi, mid, lo


def _norm_proj_body(x_ref, g_ref, w_ref, o_ref):
    h = _rms(x_ref[...], g_ref[...]).astype(BF16)
    n = o_ref.shape[1]
    for n0 in range(0, n, 512):
        n1 = min(n0 + 512, n)
        o_ref[:, n0:n1] = jnp.dot(h, w_ref[:, n0:n1], preferred_element_type=F32)


def _norm_proj(x, g, w, tm):
    t, d = x.shape
    n = w.shape[1]
    return pl.pallas_call(
        _norm_proj_body,
        grid=(t // tm,),
        in_specs=[pl.BlockSpec((tm, d), lambda i: (i, 0)),
                  pl.BlockSpec((1, d), lambda i: (0, 0)),
                  pl.BlockSpec((d, n), lambda i: (0, 0))],
        out_specs=pl.BlockSpec((tm, n), lambda i: (i, 0)),
        out_shape=jax.ShapeDtypeStruct((t, n), F32),
        compiler_params=_cparams(("arbitrary",)),
        name="norm_proj",
    )(x, g.reshape(1, d), w)


def _out_proj_body(o1_ref, o2_ref, w_ref, x_ref, y_ref):
    k1 = o1_ref.shape[1]
    acc = jnp.dot(o1_ref[...].astype(BF16), w_ref[0:k1, :], preferred_element_type=F32)
    acc = acc + jnp.dot(o2_ref[...].astype(BF16), w_ref[k1:, :], preferred_element_type=F32)
    y_ref[...] = x_ref[...] + acc


def _out_proj(o1, o2, w, x, tm):
    t, d = x.shape
    k1, k2 = o1.shape[1], o2.shape[1]
    return pl.pallas_call(
        _out_proj_body,
        grid=(t // tm,),
        in_specs=[pl.BlockSpec((tm, k1), lambda i: (i, 0)),
                  pl.BlockSpec((tm, k2), lambda i: (i, 0)),
                  pl.BlockSpec((k1 + k2, d), lambda i: (0, 0)),
                  pl.BlockSpec((tm, d), lambda i: (i, 0))],
        out_specs=pl.BlockSpec((tm, d), lambda i: (i, 0)),
        out_shape=jax.ShapeDtypeStruct((t, d), F32),
        compiler_params=_cparams(("arbitrary",)),
        name="out_proj",
    )(o1, o2, w, x)


def _even_body(q_ref, k_ref, v_ref, ga_ref, ub_ref, vb_ref, cos_ref, sin_ref, dmask_ref, qdec_ref, kdec_ref,
               cdec_ref, gng_ref, mng_ref, ws_ref, bs_ref, s0_ref, oa_ref, ob_ref, sout_ref, *rest,
               c, nseg, carry, emit_v):
    if emit_v:
        vrows_ref, st_ref = rest
    else:
        (st_ref,) = rest
    step = pl.program_id(1)
    if carry:
        @pl.when(step == 0)
        def _():
            st_ref[...] = s0_ref[0]

    lane = lax.broadcasted_iota(jnp.int32, (c, H_A * DK_A), 1)
    first_half = (lane % DK_A) < (DK_A // 2)
    tril = lax.broadcasted_iota(jnp.int32, (c, c), 1) <= lax.broadcasted_iota(jnp.int32, (c, c), 0)
    width = H_A * DK_A

    def seg(s, carry_val):
        rows = pl.ds(pl.multiple_of(s * c, c), c)
        cos = cos_ref[rows, :]
        sin = sin_ref[rows, :]

        def rope(x):
            rot = jnp.where(first_half, pltpu.roll(x, width - DK_A // 2, 1), pltpu.roll(x, DK_A // 2, 1))
            return x * cos + rot * sin

        q = rope(q_ref[rows, :])
        k = rope(k_ref[rows, :]) * (DK_A ** -0.5)
        v = v_ref[rows, :]
        qd = q * qdec_ref[...]
        kd = k * kdec_ref[...]
        ga = ga_ref[rows, :]
        for h in range(H_A):
            ks = slice(h * DK_A, (h + 1) * DK_A)
            vs = slice(h * DV_A, (h + 1) * DV_A)
            s_old = st_ref[h] if carry else s0_ref[s, h]
            vh = v[:, vs]
            a = _dot_nt(q[:, ks], k[:, ks]) * dmask_ref[h]
            o = _dot(a, vh) + _dot(qd[:, ks], s_old)
            s_new = cdec_ref[h, 0:1, :] * s_old + _dot_tn(kd[:, ks], vh)
            if carry:
                st_ref[h] = s_new
            else:
                sout_ref[s, h] = s_new
            gh = ga[:, vs]
            oa_ref[rows, vs] = _silu(gh) * _ln(o, gng_ref[:, vs])
        u = _gelu(ub_ref[rows, :])
        vg = _gelu(vb_ref[rows, :])
        for g in range(G_B):
            gs = slice(g * DG_B, (g + 1) * DG_B)
            vr = _ln(vg[:, gs], mng_ref[:, gs])
            if emit_v:
                vrows_ref[rows, gs] = vr
            w = jnp.where(tril, ws_ref[g], 0.0)
            ob_ref[rows, gs] = u[:, gs] * (_dot(w, vr) + bs_ref[:, gs])
        return carry_val

    lax.fori_loop(0, nseg, seg, 0)
    if carry:
        @pl.when(step == pl.num_programs(1) - 1)
        def _():
            sout_ref[0] = st_ref[...]


def _retention_consts(c):
    lg = jnp.log1p(-jnp.exp2(-5.0 - jnp.arange(H_A, dtype=F32)))
    idx = jnp.arange(c, dtype=F32)
    rel = idx[:, None] - idx[None, :]
    dmask = jnp.where(rel >= 0, jnp.exp(rel[None] * lg[:, None, None]), 0.0)
    q_dec = jnp.exp((idx + 1.0)[None, :] * lg[:, None])
    k_dec = jnp.exp((c - 1.0 - idx)[None, :] * lg[:, None])
    c_dec = jnp.exp(c * lg)
    qdec = jnp.repeat(q_dec.T, DK_A, axis=1)
    kdec = jnp.repeat(k_dec.T, DK_A, axis=1)
    cdec = jnp.broadcast_to(c_dec[:, None, None], (H_A, 8, DV_A))
    return dmask, qdec, kdec, cdec


def _rope_tables(pos, rot_dim, theta, head_dim, reps):
    half = rot_dim // 2
    inv = theta ** (-jnp.arange(half, dtype=F32) / half)
    ang = pos.astype(F32)[:, None] * inv[None, :]
    cos = jnp.cos(ang)
    sin = jnp.sin(ang)
    l = pos.shape[0]
    pad = head_dim - rot_dim
    cos_h = jnp.concatenate([cos, cos, jnp.ones((l, pad), F32)], axis=1)
    sin_h = jnp.concatenate([-sin, sin, jnp.zeros((l, pad), F32)], axis=1)
    return jnp.tile(cos_h, (1, reps)), jnp.tile(sin_h, (1, reps))


def _even_mixer(proj, pos, s0, gn_g, mn_g, w_s, b_s, *, n_seq, seq_len, c, nseg, carry, emit_v):
    t = proj.shape[0]
    r = nseg * c
    dmask, qdec, kdec, cdec = _retention_consts(c)
    cos, sin = _rope_tables(pos, DK_A, RET_THETA, DK_A, H_A)
    if carry:
        steps = seq_len // r
        grid = (n_seq, steps)
        row_map = lambda b, i: b * steps + i
        tab_map = lambda b, i: (i, 0)
        s_blk = 1
    else:
        reps = r // seq_len
        cos, sin = jnp.tile(cos, (reps, 1)), jnp.tile(sin, (reps, 1))
        grid = (t // r, 1)
        row_map = lambda b, i: b
        tab_map = lambda b, i: (0, 0)
        s_blk = nseg
    bs_full = jnp.repeat(b_s[:, :c].T, DG_B, axis=1)
    col = lambda w, j: pl.BlockSpec((r, w), lambda b, i, j=j: (row_map(b, i), j))
    full = lambda shape: pl.BlockSpec(shape, lambda b, i: (0,) * len(shape))
    st_spec = pl.BlockSpec((s_blk, H_A, DK_A, DV_A), lambda b, i: (b, 0, 0, 0))
    out_specs = [col(512, 0), col(512, 0), st_spec]
    out_shape = [jax.ShapeDtypeStruct((t, 512), F32), jax.ShapeDtypeStruct((t, 512), F32),
                 jax.ShapeDtypeStruct(s0.shape, F32)]
    if emit_v:
        out_specs.append(col(512, 0))
        out_shape.append(jax.ShapeDtypeStruct((t, 512), F32))
    return pl.pallas_call(
        functools.partial(_even_body, c=c, nseg=nseg, carry=carry, emit_v=emit_v),
        grid=grid,
        in_specs=[col(256, 0), col(256, 1), col(512, 1), col(512, 2), col(512, 3), col(512, 4),
                  pl.BlockSpec((r, 256), tab_map), pl.BlockSpec((r, 256), tab_map),
                  full((H_A, c, c)), full((c, 256)), full((c, 256)), full((H_A, 8, DV_A)),
                  full((1, 512)), full((1, 512)), full((G_B, c, c)), full((c, 512)), st_spec],
        out_specs=out_specs,
        out_shape=out_shape,
        scratch_shapes=[pltpu.VMEM((H_A, DK_A, DV_A), F32)],
        compiler_params=_cparams(("arbitrary", "arbitrary")),
        name="even_mixer",
    )(proj, proj, proj, proj, proj, proj, cos, sin, dmask, qdec, kdec, cdec,
      gn_g.reshape(1, 512), mn_g.reshape(1, 512), w_s[:, :c, :c], bs_full, s0)


def _hgrn_body(qc_ref, fc_ref, ic_ref, gc_ref, lbl_ref, ong_ref, s0_ref, oc_ref, sout_ref, st_ref,
               *, c, nseg, carry, layer):
    step = pl.program_id(1)
    if carry:
        @pl.when(step == 0)
        def _():
            st_ref[...] = s0_ref[0]

    lbl = lbl_ref[...]
    e = jnp.exp(lbl - jnp.max(lbl, axis=0, keepdims=True))
    sm = e / jnp.sum(e, axis=0, keepdims=True)
    acc = sm[0:1, :]
    first = acc
    for i in range(1, layer + 1):
        acc = acc + sm[i:i + 1, :]
    lb = acc - first

    sub = min(HGRN_SUB, c)
    nsub = c // sub
    row = lax.broadcasted_iota(jnp.int32, (c, c), 0)
    colm = lax.broadcasted_iota(jnp.int32, (c, c), 1)
    tri = (colm <= row).astype(BF16)
    rowc = lax.broadcasted_iota(jnp.int32, (c, 1), 0)
    band_masks = [(colm == row - d) & ((row % sub) >= d) for d in range(sub)]

    def seg(s, carry_val):
        rows = pl.ds(pl.multiple_of(s * c, c), c)
        f = lb + (1.0 - lb) * _sigmoid(fc_ref[rows, :])
        g = jnp.log(f)
        k = 1.0 - f
        q = _silu(qc_ref[rows, :])
        v = ic_ref[rows, :]
        gc = gc_ref[rows, :]
        ghi, gmid, glo = _split3(g)
        b = (jnp.dot(tri, ghi, preferred_element_type=F32) + jnp.dot(tri, gmid, preferred_element_type=F32)
             + jnp.dot(tri, glo, preferred_element_type=F32))
        for h in range(H_C):
            hs = slice(h * DK_C, (h + 1) * DK_C)
            qh, kh, fh, bh, vh = q[:, hs], k[:, hs], f[:, hs], b[:, hs], v[:, hs]
            s_old = st_ref[h] if carry else s0_ref[s, h]
            a = jnp.zeros((c, c), F32)
            p = None
            for d in range(sub):
                if d == 0:
                    term = qh * kh
                else:
                    rf = fh if d == 1 else pltpu.roll(fh, d - 1, 0)
                    p = rf if d == 1 else p * rf
                    term = qh * p * pltpu.roll(kh, d, 0)
                band = jnp.sum(term, axis=-1, keepdims=True)
                a = a + jnp.where(band_masks[d], band, 0.0)
            if nsub > 1:
                bend = jnp.concatenate(
                    [jnp.broadcast_to(bh[(j + 1) * sub - 1:(j + 1) * sub, :], (sub, DK_C)) for j in range(nsub)], axis=0)
                kt = kh * jnp.exp(bend - bh)
                qs, ks = [], []
                for j in range(nsub - 1):
                    be = bh[(j + 1) * sub - 1:(j + 1) * sub, :]
                    later = rowc >= (j + 1) * sub
                    own = (rowc >= j * sub) & (rowc < (j + 1) * sub)
                    qs.append(jnp.where(later, qh * jnp.exp(jnp.minimum(bh - be, 0.0)), 0.0).astype(BF16))
                    ks.append(jnp.where(own, kt, 0.0).astype(BF16))
                a = a + lax.dot_general(jnp.concatenate(qs, axis=1), jnp.concatenate(ks, axis=1),
                                        (((1,), (1,)), ((), ())), preferred_element_type=F32)
            o = _dot(a, vh) + _dot(qh * jnp.exp(bh), s_old)
            bl = bh[c - 1:c, :]
            decay = jnp.broadcast_to(jnp.exp(bl), (DK_C, DK_C)).T
            s_new = decay * s_old + _dot_tn(kh * jnp.exp(bl - bh), vh)
            if carry:
                st_ref[h] = s_new
            else:
                sout_ref[s, h] = s_new
            oc_ref[rows, hs] = _rms(o, ong_ref[:, hs]) * _silu(gc[:, hs])
        return carry_val

    lax.fori_loop(0, nseg, seg, 0)
    if carry:
        @pl.when(step == pl.num_programs(1) - 1)
        def _():
            sout_ref[0] = st_ref[...]


def _hgrn_mixer(proj, lb_logits, on_g, s0, *, n_seq, seq_len, c, nseg, carry, layer):
    t = proj.shape[0]
    r = nseg * c
    if carry:
        steps = seq_len // r
        grid = (n_seq, steps)
        row_map = lambda b, i: b * steps + i
        s_blk = 1
    else:
        grid = (t // r, 1)
        row_map = lambda b, i: b
        s_blk = nseg
    col = lambda j: pl.BlockSpec((r, 512), lambda b, i, j=j: (row_map(b, i), j))
    full = lambda shape: pl.BlockSpec(shape, lambda b, i: (0,) * len(shape))
    st_spec = pl.BlockSpec((s_blk, H_C, DK_C, DV_C), lambda b, i: (b, 0, 0, 0))
    return pl.pallas_call(
        functools.partial(_hgrn_body, c=c, nseg=nseg, carry=carry, layer=layer),
        grid=grid,
        in_specs=[col(0), col(1), col(2), col(3), full((DEPTH, 512)), full((1, 512)), st_spec],
        out_specs=[col(0), st_spec],
        out_shape=[jax.ShapeDtypeStruct((t, 512), F32), jax.ShapeDtypeStruct(s0.shape, F32)],
        scratch_shapes=[pltpu.VMEM((H_C, DK_C, DV_C), F32)],
        compiler_params=_cparams(("arbitrary", "arbitrary")),
        name="hgrn_mixer",
    )(proj, proj, proj, proj, lb_logits, on_g.reshape(1, 512), s0)


def _head_norm(x, g):
    rows, width = x.shape
    lo = lax.broadcasted_iota(jnp.int32, (rows, 128), 1) < HD_D
    outs = []
    for t in range(width // 128):
        xt = x[:, t * 128:(t + 1) * 128]
        sq = xt * xt
        ss_lo = jnp.sum(jnp.where(lo, sq, 0.0), axis=-1, keepdims=True)
        ss_hi = jnp.sum(jnp.where(lo, 0.0, sq), axis=-1, keepdims=True)
        scale = jnp.where(lo, lax.rsqrt(ss_lo * (1.0 / HD_D) + EPS), lax.rsqrt(ss_hi * (1.0 / HD_D) + EPS))
        outs.append(xt * scale)
    return jnp.concatenate(outs, axis=1) * g


def _rope_partial(x, cos, sin):
    width = x.shape[1]
    half = ROT_DIM_D // 2
    lane = lax.broadcasted_iota(jnp.int32, x.shape, 1)
    rot = jnp.where((lane % HD_D) < half, pltpu.roll(x, width - half, 1), pltpu.roll(x, half, 1))
    return x * cos + rot * sin


def _swa_prompt_body(q_ref, k_ref, v_ref, cos_ref, sin_ref, qng_ref, kng_ref, sink_ref, o_ref, kout_ref, vout_ref,
                     kprev_ref, vprev_ref):
    nb = pl.program_id(1)

    @pl.when(nb == 0)
    def _():
        kprev_ref[...] = jnp.zeros_like(kprev_ref)
        vprev_ref[...] = jnp.zeros_like(vprev_ref)

    cos = cos_ref[...]
    sin = sin_ref[...]
    rep = HQ_D // HKV_D
    q = _rope_partial(_head_norm(q_ref[...], qng_ref[...]), jnp.concatenate([cos] * (HQ_D // 2), axis=1),
                      jnp.concatenate([sin] * (HQ_D // 2), axis=1))
    k = _rope_partial(_head_norm(k_ref[...], kng_ref[...]), cos, sin)
    v = v_ref[...]
    kk = jnp.concatenate([kprev_ref[...], k], axis=0)
    vv = jnp.concatenate([vprev_ref[...], v], axis=0)
    qi = lax.broadcasted_iota(jnp.int32, (WINDOW, 2 * WINDOW), 0)
    ci = lax.broadcasted_iota(jnp.int32, (WINDOW, 2 * WINDOW), 1)
    mask = ((ci < WINDOW) & (ci > qi) & (nb > 0)) | ((ci >= WINDOW) & ((ci - WINDOW) <= qi))
    outs = []
    for h in range(HQ_D):
        g = h // rep
        kg = kk[:, g * HD_D:(g + 1) * HD_D]
        vg = vv[:, g * HD_D:(g + 1) * HD_D]
        s = _dot_nt(q[:, h * HD_D:(h + 1) * HD_D], kg) * (HD_D ** -0.5)
        s = jnp.where(mask, s, -jnp.inf)
        sink = sink_ref[h:h + 1, 0:1]
        m = jnp.maximum(jnp.max(s, axis=-1, keepdims=True), sink)
        p = jnp.exp(s - m)
        den = jnp.sum(p, axis=-1, keepdims=True) + jnp.exp(sink - m)
        outs.append(_dot(p, vg) / den)
    o_ref[...] = jnp.concatenate(outs, axis=1)
    kprev_ref[...] = k
    vprev_ref[...] = v
    kout_ref[0] = k
    vout_ref[0] = v


def _swa_prompt(proj, pos, qn_g, kn_g, sinks, *, n_seq, seq_len):
    t = proj.shape[0]
    nb = seq_len // WINDOW
    cos, sin = _rope_tables(pos, ROT_DIM_D, ROPE_THETA, HD_D, 2)
    rowb = lambda w, j: pl.BlockSpec((WINDOW, w), lambda b, i, j=j: (b * nb + i, j))
    full = lambda shape: pl.BlockSpec(shape, lambda b, i: (0,) * len(shape))
    kv_out = pl.BlockSpec((1, WINDOW, 128), lambda b, i: (b, 0, 0))
    return pl.pallas_call(
        _swa_prompt_body,
        grid=(n_seq, nb),
        in_specs=[rowb(512, 4), rowb(128, 20), rowb(128, 21),
                  pl.BlockSpec((WINDOW, 128), lambda b, i: (i, 0)), pl.BlockSpec((WINDOW, 128), lambda b, i: (i, 0)),
                  full((1, 512)), full((1, 128)), full((HQ_D, 128))],
        out_specs=[rowb(512, 0), kv_out, kv_out],
        out_shape=[jax.ShapeDtypeStruct((t, 512), F32), jax.ShapeDtypeStruct((n_seq, WINDOW, 128), F32),
                   jax.ShapeDtypeStruct((n_seq, WINDOW, 128), F32)],
        scratch_shapes=[pltpu.VMEM((WINDOW, 128), F32), pltpu.VMEM((WINDOW, 128), F32)],
        compiler_params=_cparams(("arbitrary", "arbitrary")),
        name="swa_prompt",
    )(proj, proj, proj, cos, sin, jnp.tile(qn_g, HQ_D).reshape(1, 512), jnp.tile(kn_g, HKV_D).reshape(1, 128),
      jnp.broadcast_to(sinks[:, None], (HQ_D, 128)))


def _swa_sample_body(q_ref, k_ref, v_ref, kc_ref, vc_ref, cos_ref, sin_ref, qng_ref, kng_ref, sink_ref,
                     o_ref, kout_ref, vout_ref, qs_ref, ks_ref, *, nseq, l):
    rep = HQ_D // HKV_D
    cos = cos_ref[...]
    sin = sin_ref[...]
    qs_ref[...] = _rope_partial(_head_norm(q_ref[...], qng_ref[...]), jnp.concatenate([cos] * (HQ_D // 2), axis=1),
                                jnp.concatenate([sin] * (HQ_D // 2), axis=1))
    ks_ref[...] = _rope_partial(_head_norm(k_ref[...], kng_ref[...]), cos, sin)
    qi = lax.broadcasted_iota(jnp.int32, (rep * l, WINDOW), 0) % l
    c1 = lax.broadcasted_iota(jnp.int32, (rep * l, WINDOW), 1)
    mask1 = c1 > qi
    qi2 = lax.broadcasted_iota(jnp.int32, (rep * l, l), 0) % l
    c2 = lax.broadcasted_iota(jnp.int32, (rep * l, l), 1)
    mask2 = c2 <= qi2

    def seq(s, carry_val):
        rows = pl.ds(pl.multiple_of(s * l, l), l)
        q = qs_ref[rows, :]
        k = ks_ref[rows, :]
        v = v_ref[rows, :]
        kb = kc_ref[s]
        vb = vc_ref[s]
        outs = []
        for g in range(HKV_D):
            gs = slice(g * HD_D, (g + 1) * HD_D)
            qg = jnp.concatenate([q[:, (g * rep + r) * HD_D:(g * rep + r + 1) * HD_D] for r in range(rep)], axis=0)
            s1 = jnp.where(mask1, _dot_nt(qg, kb[:, gs]) * (HD_D ** -0.5), -jnp.inf)
            s2 = jnp.where(mask2, _dot_nt(qg, k[:, gs]) * (HD_D ** -0.5), -jnp.inf)
            sink = sink_ref[g]
            m = jnp.maximum(jnp.maximum(jnp.max(s1, axis=-1, keepdims=True), jnp.max(s2, axis=-1, keepdims=True)),
                            sink[:, 0:1])
            p1 = jnp.exp(s1 - m)
            p2 = jnp.exp(s2 - m)
            den = (jnp.sum(p1, axis=-1, keepdims=True) + jnp.sum(p2, axis=-1, keepdims=True)
                   + jnp.exp(sink[:, 0:1] - m))
            og = (_dot(p1, vb[:, gs]) + _dot(p2, v[:, gs])) / den
            outs.extend([og[r * l:(r + 1) * l, :] for r in range(rep)])
        o_ref[rows, :] = jnp.concatenate(outs, axis=1)
        kout_ref[s, 0:WINDOW - l, :] = kb[l:, :]
        kout_ref[s, WINDOW - l:, :] = k
        vout_ref[s, 0:WINDOW - l, :] = vb[l:, :]
        vout_ref[s, WINDOW - l:, :] = v
        return carry_val

    lax.fori_loop(0, nseq, seq, 0)


def _swa_sample(proj, pos, kcache, vcache, qn_g, kn_g, sinks, *, nseq):
    t = proj.shape[0]
    l = pos.shape[0]
    r = nseq * l
    rep = HQ_D // HKV_D
    cos, sin = _rope_tables(pos, ROT_DIM_D, ROPE_THETA, HD_D, 2)
    cos, sin = jnp.tile(cos, (nseq, 1)), jnp.tile(sin, (nseq, 1))
    sink_rows = jnp.broadcast_to(jnp.repeat(sinks.reshape(HKV_D, rep), l, axis=1)[:, :, None], (HKV_D, rep * l, 128))
    rowb = lambda w, j: pl.BlockSpec((r, w), lambda i, j=j: (i, j))
    full = lambda shape: pl.BlockSpec(shape, lambda i: (0,) * len(shape))
    cache = pl.BlockSpec((nseq, WINDOW, 128), lambda i: (i, 0, 0))
    return pl.pallas_call(
        functools.partial(_swa_sample_body, nseq=nseq, l=l),
        grid=(t // r,),
        in_specs=[rowb(512, 4), rowb(128, 20), rowb(128, 21), cache, cache,
                  full((r, 128)), full((r, 128)), full((1, 512)), full((1, 128)), full((HKV_D, rep * l, 128))],
        out_specs=[rowb(512, 0), cache, cache],
        out_shape=[jax.ShapeDtypeStruct((t, 512), F32), jax.ShapeDtypeStruct(kcache.shape, F32),
                   jax.ShapeDtypeStruct(vcache.shape, F32)],
        scratch_shapes=[pltpu.VMEM((r, 512), F32), pltpu.VMEM((r, 128), F32)],
        compiler_params=_cparams(("arbitrary",)),
        name="swa_sample",
    )(proj, proj, proj, kcache, vcache, cos, sin, jnp.tile(qn_g, HQ_D).reshape(1, 512),
      jnp.tile(kn_g, HKV_D).reshape(1, 128), sink_rows)


def _mem_kv_body(m_ref, w_ref, g_ref, k_ref, v_ref):
    kv = jnp.dot(m_ref[...].astype(BF16), w_ref[...], preferred_element_type=F32)
    hw = H_X * HD_X
    for h in range(H_X):
        hs = slice(h * HD_X, (h + 1) * HD_X)
        k_ref[:, hs] = _rms(kv[:, hs], g_ref[...])
    v_ref[...] = kv[:, hw:]


def _mem_kv(mem, w, g):
    t = mem.shape[0]
    hw = H_X * HD_X
    return pl.pallas_call(
        _mem_kv_body,
        out_shape=[jax.ShapeDtypeStruct((t, hw), F32), jax.ShapeDtypeStruct((t, hw), F32)],
        compiler_params=pltpu.CompilerParams(vmem_limit_bytes=VMEM_LIMIT_BYTES),
        name="mem_kv",
    )(mem, w, g.reshape(1, HD_X))


def _mem_attend_body(x_ref, g_ref, wq_ref, qng_ref, mk_ref, mv_ref, wo_ref, y_ref, att_ref, *, nseq):
    x = x_ref[...]
    q = jnp.dot(_rms(x, g_ref[...]).astype(BF16), wq_ref[...], preferred_element_type=F32)
    rows_per = x.shape[0] // nseq
    for s in range(nseq):
        rs = slice(s * rows_per, (s + 1) * rows_per)
        for h in range(H_X):
            hs = slice(h * HD_X, (h + 1) * HD_X)
            qh = _rms(q[rs, hs], qng_ref[...])
            sc = _dot_nt(qh, mk_ref[s, :, hs]) * (HD_X ** -0.5)
            m = jnp.max(sc, axis=-1, keepdims=True)
            p = jnp.exp(sc - m)
            att_ref[rs, hs] = _dot(p, mv_ref[s, :, hs]) / jnp.sum(p, axis=-1, keepdims=True)
    y_ref[...] = x + jnp.dot(att_ref[...].astype(BF16), wo_ref[...], preferred_element_type=F32)


def _mem_attend(x, g, wq, qn_g, mk, mv, wo, *, tm, nseq, tiles_per_mem):
    t, d = x.shape
    hw = H_X * HD_X
    full = lambda shape: pl.BlockSpec(shape, lambda i: (0,) * len(shape))
    mem_spec = pl.BlockSpec((nseq, N_MEM, hw), lambda i: (i // tiles_per_mem, 0, 0))
    return pl.pallas_call(
        functools.partial(_mem_attend_body, nseq=nseq),
        grid=(t // tm,),
        in_specs=[pl.BlockSpec((tm, d), lambda i: (i, 0)), full((1, d)), full((d, hw)), full((1, HD_X)),
                  mem_spec, mem_spec, full((hw, d))],
        out_specs=pl.BlockSpec((tm, d), lambda i: (i, 0)),
        out_shape=jax.ShapeDtypeStruct((t, d), F32),
        scratch_shapes=[pltpu.VMEM((tm, hw), F32)],
        compiler_params=_cparams(("arbitrary",)),
        name="mem_attend",
    )(x, g.reshape(1, d), wq, qn_g.reshape(1, HD_X), mk, mv, wo)


def _ffn_body(*refs, tm, nff, tiles_per_seq, per_row_state):
    if per_row_state:
        (x_ref, g_ref, wg_ref, wu_ref, cw_ref, cb_ref, wd_ref, c1_ref, c2_ref,
         y_ref, gt_ref, h_ref, acc_ref, gg_ref, tail_ref) = refs
    else:
        (x_ref, g_ref, wg_ref, wu_ref, cw_ref, cb_ref, wd_ref,
         y_ref, gt_ref, h_ref, acc_ref, gg_ref, tail_ref) = refs
    i = pl.program_id(0)
    j = pl.program_id(1)

    @pl.when(j == 0)
    def _():
        h_ref[...] = _rms(x_ref[...], g_ref[...]).astype(BF16)
        acc_ref[...] = jnp.zeros_like(acc_ref)

    h = h_ref[...]
    gate = jnp.dot(h, wg_ref[...], preferred_element_type=F32)
    up = jnp.dot(h, wu_ref[...], preferred_element_type=F32)
    gg_ref[8:, :] = gate
    if per_row_state:
        gg_ref[0:8, :] = jnp.zeros((8, gate.shape[1]), F32)
        pos = lax.broadcasted_iota(jnp.int32, gate.shape, 0) % 8
        g1 = jnp.where(pos == 0, c1_ref[...], gg_ref[7:7 + tm, :])
        g2 = jnp.where(pos < 2, c2_ref[...], gg_ref[6:6 + tm, :])
        gt_ref[...] = gate
    else:
        first = (i % tiles_per_seq) == 0

        @pl.when(first)
        def _():
            gg_ref[0:8, :] = jnp.zeros((8, gate.shape[1]), F32)

        @pl.when(jnp.logical_not(first))
        def _():
            gg_ref[0:8, :] = tail_ref[j]

        g1 = gg_ref[7:7 + tm, :]
        g2 = gg_ref[6:6 + tm, :]
        tail_ref[j] = gate[tm - 8:, :]
        gt_ref[...] = gate[tm - 8:, :]
    cw = cw_ref[...]
    conv = cb_ref[...] + cw[0:1, :] * g2 + cw[1:2, :] * g1 + cw[2:3, :] * gate
    act = (_gelu(conv) * up).astype(BF16)
    acc_ref[...] += jnp.dot(act, wd_ref[...], preferred_element_type=F32)

    @pl.when(j == nff - 1)
    def _():
        y_ref[...] = x_ref[...] + acc_ref[...]


def _ffn(x, g, wg, wu, cw, cb, wd, *, tm, ffc, tiles_per_seq, c1=None, c2=None):
    t, d = x.shape
    ff = wg.shape[1]
    nff = ff // ffc
    per_row_state = c1 is not None
    tail_n = tm if per_row_state else 8
    in_specs = [pl.BlockSpec((tm, d), lambda i, j: (i, 0)), pl.BlockSpec((1, d), lambda i, j: (0, 0)),
                pl.BlockSpec((d, ffc), lambda i, j: (0, j)), pl.BlockSpec((d, ffc), lambda i, j: (0, j)),
                pl.BlockSpec((8, ffc), lambda i, j: (0, j)), pl.BlockSpec((1, ffc), lambda i, j: (0, j)),
                pl.BlockSpec((ffc, d), lambda i, j: (j, 0))]
    args = [x, g.reshape(1, d), wg, wu, jnp.pad(cw, ((0, 8 - CONV_W), (0, 0))), cb.reshape(1, ff), wd]
    if per_row_state:
        in_specs += [pl.BlockSpec((tm, ffc), lambda i, j: (i, j)), pl.BlockSpec((tm, ffc), lambda i, j: (i, j))]
        args += [c1, c2]
    return pl.pallas_call(
        functools.partial(_ffn_body, tm=tm, nff=nff, tiles_per_seq=tiles_per_seq, per_row_state=per_row_state),
        grid=(t // tm, nff),
        in_specs=in_specs,
        out_specs=[pl.BlockSpec((tm, d), lambda i, j: (i, 0)), pl.BlockSpec((tail_n, ffc), lambda i, j: (i, j))],
        out_shape=[jax.ShapeDtypeStruct((t, d), F32), jax.ShapeDtypeStruct((t // tm * tail_n, ff), F32)],
        scratch_shapes=[pltpu.VMEM((tm, d), BF16), pltpu.VMEM((tm, d), F32), pltpu.VMEM((tm + 8, ffc), F32),
                        pltpu.VMEM((nff, 8, ffc), F32)],
        compiler_params=_cparams(("arbitrary", "arbitrary")),
        name="ffn",
    )(*args)


def _run_group(x, pos, n_seq, seq_len, mem, st, w):
    prompt = st is None
    t = x.shape[0]
    tm = 512 if prompt else t
    out = {k: [] for k in ("ret", "chunk_v", "hgrn", "swa_k", "swa_v", "mem_k", "mem_v", "conv")}
    for l in range(DEPTH):
        j = l // 2
        if l % 2 == 0:
            proj = _norm_proj(x, w["norm_mix_g"][l], w["ev_w_in"][j], tm)
            if prompt:
                c = math.gcd(seq_len, RET_CHUNK)
                s0 = jnp.zeros((n_seq, H_A, DK_A, DV_A), F32)
                o_a, o_b, s_new = _even_mixer(proj, pos, s0, w["ret_gn_g"][j], w["mlp_norm_g"][j], w["mlp_w_s"][j],
                                              w["mlp_b_s"][j], n_seq=n_seq, seq_len=seq_len, c=c, nseg=4, carry=True,
                                              emit_v=False)
            else:
                o_a, o_b, s_new, v_rows = _even_mixer(proj, pos, st["ret"][j], w["ret_gn_g"][j], w["mlp_norm_g"][j],
                                                      w["mlp_w_s"][j], w["mlp_b_s"][j], n_seq=n_seq, seq_len=seq_len,
                                                      c=seq_len, nseg=16, carry=False, emit_v=True)
                out["chunk_v"].append(v_rows.reshape(n_seq, seq_len, G_B, DG_B))
            out["ret"].append(s_new)
            x = _out_proj(o_a, o_b, w["ev_w_out"][j], x, tm)
        else:
            proj = _norm_proj(x, w["norm_mix_g"][l], w["od_w_in"][j], tm)
            if prompt:
                c = math.gcd(seq_len, HGRN_CHUNK)
                s0 = jnp.zeros((n_seq, H_C, DK_C, DV_C), F32)
                o_c, s_new = _hgrn_mixer(proj, w["hgrn_lb_logits"], w["hgrn_onorm_g"][j], s0, n_seq=n_seq,
                                         seq_len=seq_len, c=c, nseg=8, carry=True, layer=l)
                o_d, k_new, v_new = _swa_prompt(proj, pos, w["swa_qnorm_g"][j], w["swa_knorm_g"][j], w["swa_sinks"][j],
                                                n_seq=n_seq, seq_len=seq_len)
            else:
                o_c, s_new = _hgrn_mixer(proj, w["hgrn_lb_logits"], w["hgrn_onorm_g"][j], st["hgrn"][j], n_seq=n_seq,
                                         seq_len=seq_len, c=seq_len, nseg=16, carry=False, layer=l)
                o_d, k_new, v_new = _swa_sample(proj, pos, st["swa_k"][j].reshape(n_seq, WINDOW, HKV_D * HD_D),
                                                st["swa_v"][j].reshape(n_seq, WINDOW, HKV_D * HD_D),
                                                w["swa_qnorm_g"][j], w["swa_knorm_g"][j], w["swa_sinks"][j], nseq=16)
            out["hgrn"].append(s_new)
            out["swa_k"].append(k_new.reshape(n_seq, WINDOW, HKV_D, HD_D))
            out["swa_v"].append(v_new.reshape(n_seq, WINDOW, HKV_D, HD_D))
            x = _out_proj(o_c, o_d, w["od_w_out"][j], x, tm)
        hw = H_X * HD_X
        if prompt:
            mk, mv = _mem_kv(mem, w["mem_w_kv"][l], w["mem_knorm_g"][l])
            out["mem_k"].append(mk.reshape(n_seq, N_MEM, H_X, HD_X))
            out["mem_v"].append(mv.reshape(n_seq, N_MEM, H_X, HD_X))
            x = _mem_attend(x, w["norm_mem_g"][l], w["mem_w_q"][l], w["mem_qnorm_g"][l], mk.reshape(n_seq, N_MEM, hw),
                            mv.reshape(n_seq, N_MEM, hw), w["mem_w_o"][l], tm=tm, nseq=1, tiles_per_mem=seq_len // tm)
        else:
            x = _mem_attend(x, w["norm_mem_g"][l], w["mem_w_q"][l], w["mem_qnorm_g"][l],
                            st["mem_k"][l].reshape(n_seq, N_MEM, hw), st["mem_v"][l].reshape(n_seq, N_MEM, hw),
                            w["mem_w_o"][l], tm=8 * seq_len, nseq=8, tiles_per_mem=1)
        ffw = (w["ffn_w_gate"][l], w["ffn_w_up"][l], w["ffn_conv_w"][l], w["ffn_conv_b"][l], w["ffn_w_down"][l])
        if prompt:
            x, gt = _ffn(x, w["norm_ffn_g"][l], *ffw, tm=tm, ffc=D_FF // 2, tiles_per_seq=seq_len // tm)
            tiles = seq_len // tm
            out["conv"].append(gt.reshape(n_seq, tiles, 8, D_FF)[:, -1, 8 - (CONV_W - 1):, :])
        else:
            c0 = st["conv"][l]
            z = jnp.zeros((n_seq, seq_len - 2, D_FF), F32)
            c2 = jnp.concatenate([c0, z], axis=1).reshape(t, D_FF)
            c1 = jnp.concatenate([c0[:, 1:2], jnp.zeros((n_seq, seq_len - 1, D_FF), F32)], axis=1).reshape(t, D_FF)
            x, gt = _ffn(x, w["norm_ffn_g"][l], *ffw, tm=256, ffc=D_FF // 2, tiles_per_seq=1, c1=c1, c2=c2)
            out["conv"].append(gt.reshape(n_seq, seq_len, D_FF)[:, seq_len - (CONV_W - 1):, :])
    return x, {name: jnp.stack(rows) for name, rows in out.items() if rows}


def kernel(x_prompt, x_sample, state_ret, state_hgrn, cache_swa_k, cache_swa_v, cache_mem_k, cache_mem_v,
           state_ffn_conv, mem_prompt, norm_mix_g, norm_mem_g, norm_ffn_g, ev_w_in, ev_w_out, ret_gn_g,
           mlp_norm_g, mlp_w_s, mlp_b_s, od_w_in, od_w_out, hgrn_lb_logits, hgrn_onorm_g, swa_qnorm_g,
           swa_knorm_g, swa_sinks, mem_w_q, mem_w_kv, mem_qnorm_g, mem_knorm_g, mem_w_o, ffn_w_gate,
           ffn_w_up, ffn_conv_w, ffn_conv_b, ffn_w_down):
    bf = lambda a: a.astype(BF16)
    w = dict(norm_mix_g=norm_mix_g, norm_mem_g=norm_mem_g, norm_ffn_g=norm_ffn_g, ev_w_in=bf(ev_w_in),
             ev_w_out=bf(ev_w_out), ret_gn_g=ret_gn_g, mlp_norm_g=mlp_norm_g, mlp_w_s=mlp_w_s, mlp_b_s=mlp_b_s,
             od_w_in=bf(od_w_in), od_w_out=bf(od_w_out), hgrn_lb_logits=hgrn_lb_logits, hgrn_onorm_g=hgrn_onorm_g,
             swa_qnorm_g=swa_qnorm_g, swa_knorm_g=swa_knorm_g, swa_sinks=swa_sinks, mem_w_q=bf(mem_w_q),
             mem_w_kv=bf(mem_w_kv), mem_qnorm_g=mem_qnorm_g, mem_knorm_g=mem_knorm_g, mem_w_o=bf(mem_w_o),
             ffn_w_gate=bf(ffn_w_gate), ffn_w_up=bf(ffn_w_up), ffn_conv_w=ffn_conv_w, ffn_conv_b=ffn_conv_b,
             ffn_w_down=bf(ffn_w_down))
    b, seq, d = x_prompt.shape
    db, dseq, _ = x_sample.shape
    pos_prompt = jnp.arange(seq, dtype=jnp.int32)
    pos_sample = PAST_LEN + jnp.arange(dseq, dtype=jnp.int32)
    y_p, ns_p = _run_group(x_prompt.reshape(b * seq, d), pos_prompt, b, seq, mem_prompt.reshape(b * N_MEM, d), None, w)
    st = dict(ret=state_ret, hgrn=state_hgrn, swa_k=cache_swa_k, swa_v=cache_swa_v, mem_k=cache_mem_k,
              mem_v=cache_mem_v, conv=state_ffn_conv)
    y_s, ns_s = _run_group(x_sample.reshape(db * dseq, d), pos_sample, db, dseq, None, st, w)
    return (y_p.reshape(b, seq, d), y_s.reshape(db, dseq, d), ns_p["ret"], ns_p["hgrn"], ns_p["swa_k"], ns_p["swa_v"],
            ns_p["mem_k"], ns_p["mem_v"], ns_p["conv"], ns_s["ret"], ns_s["chunk_v"], ns_s["hgrn"], ns_s["swa_k"],
            ns_s["swa_v"], ns_s["conv"])
```

```python
import functools
import math

import jax
import jax.numpy as jnp
import numpy as np
from jax import lax
from jax.experimental import pallas as pl
from jax.experimental.pallas import tpu as pltpu

F32 = jnp.float32
BF16 = jnp.bfloat16

D_MODEL = 1024
DEPTH = 2
PAST_LEN = 16384
HALF_W = D_MODEL // 2
H_A, DV_A, DK_A = 4, 128, 64
RET_CHUNK = 128
RET_THETA = 10000.0
G_B, DG_B = 4, 128
MLP_CHUNK = 128
H_C, DK_C, DV_C = 4, 128, 128
HGRN_CHUNK = 64
HGRN_SUB = 8
HD_D, HQ_D, HKV_D = 64, 8, 2
WINDOW = 128
ROPE_THETA = 500000.0
ROT_DIM_D = HD_D // 4
N_MEM, H_X, HD_X = 256, 4, 128
D_FF = 2816
CONV_W = 3
EPS = 1e-6
EVEN_IN = 2 * H_A * DK_A + 2 * H_A * DV_A + 2 * G_B * DG_B
ODD_IN = 4 * H_C * DK_C + HQ_D * HD_D + 2 * HKV_D * HD_D
SQRT_HALF = float(np.sqrt(0.5))

VMEM_LIMIT_BYTES = 56 * 1024 * 1024


def _cparams(sem):
    return pltpu.CompilerParams(dimension_semantics=sem, vmem_limit_bytes=VMEM_LIMIT_BYTES)


def _rms(x, g):
    return x * lax.rsqrt(jnp.mean(x * x, axis=-1, keepdims=True) + EPS) * g


def _ln(x, g):
    mu = jnp.mean(x, axis=-1, keepdims=True)
    xc = x - mu
    return xc * lax.rsqrt(jnp.mean(xc * xc, axis=-1, keepdims=True) + EPS) * g


def _sigmoid(x):
    return 1.0 / (1.0 + jnp.exp(-x))


def _silu(x):
    return x * _sigmoid(x)


def _gelu(x):
    return 0.5 * x * (1.0 + lax.erf(x * SQRT_HALF))


def _dot(a, b):
    return jnp.dot(a.astype(BF16), b.astype(BF16), preferred_element_type=F32)


def _dot_nt(a, b):
    return lax.dot_general(a.astype(BF16), b.astype(BF16), (((1,), (1,)), ((), ())), preferred_element_type=F32)


def _dot_tn(a, b):
    return lax.dot_general(a.astype(BF16), b.astype(BF16), (((0,), (0,)), ((), ())), preferred_element_type=F32)


def _split3(x):
    hi = x.astype(BF16)
    r1 = x - hi.astype(F32)
    mid = r1.astype(BF16)
    lo = (r1 - mid.astype(F32)).astype(BF16)
    return hi, mid, lo


def _norm_proj_body(x_ref, g_ref, w_ref, o_ref):
    h = _rms(x_ref[...], g_ref[...]).astype(BF16)
    n = o_ref.shape[1]
    for n0 in range(0, n, 512):
        n1 = min(n0 + 512, n)
        o_ref[:, n0:n1] = jnp.dot(h, w_ref[:, n0:n1], preferred_element_type=F32)


def _norm_proj(x, g, w, tm):
    t, d = x.shape
    n = w.shape[1]
    return pl.pallas_call(
        _norm_proj_body,
        grid=(t // tm,),
        in_specs=[pl.BlockSpec((tm, d), lambda i: (i, 0)),
                  pl.BlockSpec((1, d), lambda i: (0, 0)),
                  pl.BlockSpec((d, n), lambda i: (0, 0))],
        out_specs=pl.BlockSpec((tm, n), lambda i: (i, 0)),
        out_shape=jax.ShapeDtypeStruct((t, n), F32),
        compiler_params=_cparams(("arbitrary",)),
        name="norm_proj",
    )(x, g.reshape(1, d), w)


def _out_proj_body(o1_ref, o2_ref, w_ref, x_ref, y_ref):
    k1 = o1_ref.shape[1]
    acc = jnp.dot(o1_ref[...].astype(BF16), w_ref[0:k1, :], preferred_element_type=F32)
    acc = acc + jnp.dot(o2_ref[...].astype(BF16), w_ref[k1:, :], preferred_element_type=F32)
    y_ref[...] = x_ref[...] + acc


def _out_proj(o1, o2, w, x, tm):
    t, d = x.shape
    k1, k2 = o1.shape[1], o2.shape[1]
    return pl.pallas_call(
        _out_proj_body,
        grid=(t // tm,),
        in_specs=[pl.BlockSpec((tm, k1), lambda i: (i, 0)),
                  pl.BlockSpec((tm, k2), lambda i: (i, 0)),
                  pl.BlockSpec((k1 + k2, d), lambda i: (0, 0)),
                  pl.BlockSpec((tm, d), lambda i: (i, 0))],
        out_specs=pl.BlockSpec((tm, d), lambda i: (i, 0)),
        out_shape=jax.ShapeDtypeStruct((t, d), F32),
        compiler_params=_cparams(("arbitrary",)),
        name="out_proj",
    )(o1, o2, w, x)


def _even_body(q_ref, k_ref, v_ref, ga_ref, ub_ref, vb_ref, cos_ref, sin_ref, dmask_ref, qdec_ref, kdec_ref,
               cdec_ref, gng_ref, mng_ref, ws_ref, bs_ref, s0_ref, oa_ref, ob_ref, sout_ref, *rest,
               c, nseg, carry, emit_v):
    if emit_v:
        vrows_ref, st_ref = rest
    else:
        (st_ref,) = rest
    step = pl.program_id(1)
    if carry:
        @pl.when(step == 0)
        def _():
            st_ref[...] = s0_ref[0]

    lane = lax.broadcasted_iota(jnp.int32, (c, H_A * DK_A), 1)
    first_half = (lane % DK_A) < (DK_A // 2)
    tril = lax.broadcasted_iota(jnp.int32, (c, c), 1) <= lax.broadcasted_iota(jnp.int32, (c, c), 0)
    width = H_A * DK_A

    def seg(s, carry_val):
        rows = pl.ds(pl.multiple_of(s * c, c), c)
        cos = cos_ref[rows, :]
        sin = sin_ref[rows, :]

        def rope(x):
            rot = jnp.where(first_half, pltpu.roll(x, width - DK_A // 2, 1), pltpu.roll(x, DK_A // 2, 1))
            return x * cos + rot * sin

        q = rope(q_ref[rows, :])
        k = rope(k_ref[rows, :]) * (DK_A ** -0.5)
        v = v_ref[rows, :]
        qd = q * qdec_ref[...]
        kd = k * kdec_ref[...]
        ga = ga_ref[rows, :]
        for h in range(H_A):
            ks = slice(h * DK_A, (h + 1) * DK_A)
            vs = slice(h * DV_A, (h + 1) * DV_A)
            s_old = st_ref[h] if carry else s0_ref[s, h]
            vh = v[:, vs]
            a = _dot_nt(q[:, ks], k[:, ks]) * dmask_ref[h]
            o = _dot(a, vh) + _dot(qd[:, ks], s_old)
            s_new = cdec_ref[h, 0:1, :] * s_old + _dot_tn(kd[:, ks], vh)
            if carry:
                st_ref[h] = s_new
            else:
                sout_ref[s, h] = s_new
            gh = ga[:, vs]
            oa_ref[rows, vs] = _silu(gh) * _ln(o, gng_ref[:, vs])
        u = _gelu(ub_ref[rows, :])
        vg = _gelu(vb_ref[rows, :])
        for g in range(G_B):
            gs = slice(g * DG_B, (g + 1) * DG_B)
            vr = _ln(vg[:, gs], mng_ref[:, gs])
            if emit_v:
                vrows_ref[rows, gs] = vr
            w = jnp.where(tril, ws_ref[g], 0.0)
            ob_ref[rows, gs] = u[:, gs] * (_dot(w, vr) + bs_ref[:, gs])
        return carry_val

    lax.fori_loop(0, nseg, seg, 0)
    if carry:
        @pl.when(step == pl.num_programs(1) - 1)
        def _():
            sout_ref[0] = st_ref[...]


def _retention_consts(c):
    lg = jnp.log1p(-jnp.exp2(-5.0 - jnp.arange(H_A, dtype=F32)))
    idx = jnp.arange(c, dtype=F32)
    rel = idx[:, None] - idx[None, :]
    dmask = jnp.where(rel >= 0, jnp.exp(rel[None] * lg[:, None, None]), 0.0)
    q_dec = jnp.exp((idx + 1.0)[None, :] * lg[:, None])
    k_dec = jnp.exp((c - 1.0 - idx)[None, :] * lg[:, None])
    c_dec = jnp.exp(c * lg)
    qdec = jnp.repeat(q_dec.T, DK_A, axis=1)
    kdec = jnp.repeat(k_dec.T, DK_A, axis=1)
    cdec = jnp.broadcast_to(c_dec[:, None, None], (H_A, 8, DV_A))
    return dmask, qdec, kdec, cdec


def _rope_tables(pos, rot_dim, theta, head_dim, reps):
    half = rot_dim // 2
    inv = theta ** (-jnp.arange(half, dtype=F32) / half)
    ang = pos.astype(F32)[:, None] * inv[None, :]
    cos = jnp.cos(ang)
    sin = jnp.sin(ang)
    l = pos.shape[0]
    pad = head_dim - rot_dim
    cos_h = jnp.concatenate([cos, cos, jnp.ones((l, pad), F32)], axis=1)
    sin_h = jnp.concatenate([-sin, sin, jnp.zeros((l, pad), F32)], axis=1)
    return jnp.tile(cos_h, (1, reps)), jnp.tile(sin_h, (1, reps))


def _even_mixer(proj, pos, s0, gn_g, mn_g, w_s, b_s, *, n_seq, seq_len, c, nseg, carry, emit_v):
    t = proj.shape[0]
    r = nseg * c
    dmask, qdec, kdec, cdec = _retention_consts(c)
    cos, sin = _rope_tables(pos, DK_A, RET_THETA, DK_A, H_A)
    if carry:
        steps = seq_len // r
        grid = (n_seq, steps)
        row_map = lambda b, i: b * steps + i
        tab_map = lambda b, i: (i, 0)
        s_blk = 1
    else:
        reps = r // seq_len
        cos, sin = jnp.tile(cos, (reps, 1)), jnp.tile(sin, (reps, 1))
        grid = (t // r, 1)
        row_map = lambda b, i: b
        tab_map = lambda b, i: (0, 0)
        s_blk = nseg
    bs_full = jnp.repeat(b_s[:, :c].T, DG_B, axis=1)
    col = lambda w, j: pl.BlockSpec((r, w), lambda b, i, j=j: (row_map(b, i), j))
    full = lambda shape: pl.BlockSpec(shape, lambda b, i: (0,) * len(shape))
    st_spec = pl.BlockSpec((s_blk, H_A, DK_A, DV_A), lambda b, i: (b, 0, 0, 0))
    out_specs = [col(512, 0), col(512, 0), st_spec]
    out_shape = [jax.ShapeDtypeStruct((t, 512), F32), jax.ShapeDtypeStruct((t, 512), F32),
                 jax.ShapeDtypeStruct(s0.shape, F32)]
    if emit_v:
        out_specs.append(col(512, 0))
        out_shape.append(jax.ShapeDtypeStruct((t, 512), F32))
    return pl.pallas_call(
        functools.partial(_even_body, c=c, nseg=nseg, carry=carry, emit_v=emit_v),
        grid=grid,
        in_specs=[col(256, 0), col(256, 1), col(512, 1), col(512, 2), col(512, 3), col(512, 4),
                  pl.BlockSpec((r, 256), tab_map), pl.BlockSpec((r, 256), tab_map),
                  full((H_A, c, c)), full((c, 256)), full((c, 256)), full((H_A, 8, DV_A)),
                  full((1, 512)), full((1, 512)), full((G_B, c, c)), full((c, 512)), st_spec],
        out_specs=out_specs,
        out_shape=out_shape,
        scratch_shapes=[pltpu.VMEM((H_A, DK_A, DV_A), F32)],
        compiler_params=_cparams(("arbitrary", "arbitrary")),
        name="even_mixer",
    )(proj, proj, proj, proj, proj, proj, cos, sin, dmask, qdec, kdec, cdec,
      gn_g.reshape(1, 512), mn_g.reshape(1, 512), w_s[:, :c, :c], bs_full, s0)


def _hgrn_body(qc_ref, fc_ref, ic_ref, gc_ref, lbl_ref, ong_ref, s0_ref, oc_ref, sout_ref, st_ref,
               *, c, nseg, carry, layer):
    step = pl.program_id(1)
    if carry:
        @pl.when(step == 0)
        def _():
            st_ref[...] = s0_ref[0]

    lbl = lbl_ref[...]
    e = jnp.exp(lbl - jnp.max(lbl, axis=0, keepdims=True))
    sm = e / jnp.sum(e, axis=0, keepdims=True)
    acc = sm[0:1, :]
    first = acc
    for i in range(1, layer + 1):
        acc = acc + sm[i:i + 1, :]
    lb = acc - first

    sub = min(HGRN_SUB, c)
    nsub = c // sub
    row = lax.broadcasted_iota(jnp.int32, (c, c), 0)
    colm = lax.broadcasted_iota(jnp.int32, (c, c), 1)
    tri = (colm <= row).astype(BF16)
    rowc = lax.broadcasted_iota(jnp.int32, (c, 1), 0)
    band_masks = [(colm == row - d) & ((row % sub) >= d) for d in range(sub)]

    def seg(s, carry_val):
        rows = pl.ds(pl.multiple_of(s * c, c), c)
        f = lb + (1.0 - lb) * _sigmoid(fc_ref[rows, :])
        g = jnp.log(f)
        k = 1.0 - f
        q = _silu(qc_ref[rows, :])
        v = ic_ref[rows, :]
        gc = gc_ref[rows, :]
        ghi, gmid, glo = _split3(g)
        b = (jnp.dot(tri, ghi, preferred_element_type=F32) + jnp.dot(tri, gmid, preferred_element_type=F32)
             + jnp.dot(tri, glo, preferred_element_type=F32))
        for h in range(H_C):
            hs = slice(h * DK_C, (h + 1) * DK_C)
            qh, kh, fh, bh, vh = q[:, hs], k[:, hs], f[:, hs], b[:, hs], v[:, hs]
            s_old = st_ref[h] if carry else s0_ref[s, h]
            a = jnp.zeros((c, c), F32)
            p = None
            for d in range(sub):
                if d == 0:
                    term = qh * kh
                else:
                    rf = fh if d == 1 else pltpu.roll(fh, d - 1, 0)
                    p = rf if d == 1 else p * rf
                    term = qh * p * pltpu.roll(kh, d, 0)
                band = jnp.sum(term, axis=-1, keepdims=True)
                a = a + jnp.where(band_masks[d], band, 0.0)
            if nsub > 1:
                bend = jnp.concatenate(
                    [jnp.broadcast_to(bh[(j + 1) * sub - 1:(j + 1) * sub, :], (sub, DK_C)) for j in range(nsub)], axis=0)
                kt = kh * jnp.exp(bend - bh)
                qs, ks = [], []
                for j in range(nsub - 1):
                    be = bh[(j + 1) * sub - 1:(j + 1) * sub, :]
                    later = rowc >= (j + 1) * sub
                    own = (rowc >= j * sub) & (rowc < (j + 1) * sub)
                    qs.append(jnp.where(later, qh * jnp.exp(jnp.minimum(bh - be, 0.0)), 0.0).astype(BF16))
                    ks.append(jnp.where(own, kt, 0.0).astype(BF16))
                a = a + lax.dot_general(jnp.concatenate(qs, axis=1), jnp.concatenate(ks, axis=1),
                                        (((1,), (1,)), ((), ())), preferred_element_type=F32)
            o = _dot(a, vh) + _dot(qh * jnp.exp(bh), s_old)
            bl = bh[c - 1:c, :]
            decay = jnp.broadcast_to(jnp.exp(bl), (DK_C, DK_C)).T
            s_new = decay * s_old + _dot_tn(kh * jnp.exp(bl - bh), vh)
            if carry:
                st_ref[h] = s_new
            else:
                sout_ref[s, h] = s_new
            oc_ref[rows, hs] = _rms(o, ong_ref[:, hs]) * _silu(gc[:, hs])
        return carry_val

    lax.fori_loop(0, nseg, seg, 0)
    if carry:
        @pl.when(step == pl.num_programs(1) - 1)
        def _():
            sout_ref[0] = st_ref[...]


def _hgrn_mixer(proj, lb_logits, on_g, s0, *, n_seq, seq_len, c, nseg, carry, layer):
    t = proj.shape[0]
    r = nseg * c
    if carry:
        steps = seq_len // r
        grid = (n_seq, steps)
        row_map = lambda b, i: b * steps + i
        s_blk = 1
    else:
        grid = (t // r, 1)
        row_map = lambda b, i: b
        s_blk = nseg
    col = lambda j: pl.BlockSpec((r, 512), lambda b, i, j=j: (row_map(b, i), j))
    full = lambda shape: pl.BlockSpec(shape, lambda b, i: (0,) * len(shape))
    st_spec = pl.BlockSpec((s_blk, H_C, DK_C, DV_C), lambda b, i: (b, 0, 0, 0))
    return pl.pallas_call(
        functools.partial(_hgrn_body, c=c, nseg=nseg, carry=carry, layer=layer),
        grid=grid,
        in_specs=[col(0), col(1), col(2), col(3), full((DEPTH, 512)), full((1, 512)), st_spec],
        out_specs=[col(0), st_spec],
        out_shape=[jax.ShapeDtypeStruct((t, 512), F32), jax.ShapeDtypeStruct(s0.shape, F32)],
        scratch_shapes=[pltpu.VMEM((H_C, DK_C, DV_C), F32)],
        compiler_params=_cparams(("arbitrary", "arbitrary")),
        name="hgrn_mixer",
    )(proj, proj, proj, proj, lb_logits, on_g.reshape(1, 512), s0)


def _head_norm(x, g):
    rows, width = x.shape
    lo = lax.broadcasted_iota(jnp.int32, (rows, 128), 1) < HD_D
    outs = []
    for t in range(width // 128):
        xt = x[:, t * 128:(t + 1) * 128]
        sq = xt * xt
        ss_lo = jnp.sum(jnp.where(lo, sq, 0.0), axis=-1, keepdims=True)
        ss_hi = jnp.sum(jnp.where(lo, 0.0, sq), axis=-1, keepdims=True)
        scale = jnp.where(lo, lax.rsqrt(ss_lo * (1.0 / HD_D) + EPS), lax.rsqrt(ss_hi * (1.0 / HD_D) + EPS))
        outs.append(xt * scale)
    return jnp.concatenate(outs, axis=1) * g


def _rope_partial(x, cos, sin):
    width = x.shape[1]
    half = ROT_DIM_D // 2
    lane = lax.broadcasted_iota(jnp.int32, x.shape, 1)
    rot = jnp.where((lane % HD_D) < half, pltpu.roll(x, width - half, 1), pltpu.roll(x, half, 1))
    return x * cos + rot * sin


def _swa_prompt_body(q_ref, k_ref, v_ref, cos_ref, sin_ref, qng_ref, kng_ref, sink_ref, o_ref, kout_ref, vout_ref,
                     kprev_ref, vprev_ref, *, nblk):
    step = pl.program_id(1)

    @pl.when(step == 0)
    def _():
        kprev_ref[...] = jnp.zeros_like(kprev_ref)
        vprev_ref[...] = jnp.zeros_like(vprev_ref)

    cos = cos_ref[...]
    sin = sin_ref[...]
    rep = HQ_D // HKV_D
    q = _rope_partial(_head_norm(q_ref[...], qng_ref[...]), jnp.concatenate([cos] * (HQ_D // 2), axis=1),
                      jnp.concatenate([sin] * (HQ_D // 2), axis=1)).astype(BF16)
    k = _rope_partial(_head_norm(k_ref[...], kng_ref[...]), cos, sin)
    v = v_ref[...]
    kk = jnp.concatenate([kprev_ref[...], k], axis=0)
    vv = jnp.concatenate([vprev_ref[...], v], axis=0)
    lo = lax.broadcasted_iota(jnp.int32, kk.shape, 1) < HD_D
    kk_sw = pltpu.roll(kk, HD_D, 1)
    vv_sw = pltpu.roll(vv, HD_D, 1)
    kx = [[jnp.where(lo, kk, 0.0).astype(BF16), jnp.where(lo, 0.0, kk_sw).astype(BF16)],
          [jnp.where(lo, kk_sw, 0.0).astype(BF16), jnp.where(lo, 0.0, kk).astype(BF16)]]
    vx = [[jnp.where(lo, vv, 0.0).astype(BF16), jnp.where(lo, 0.0, vv_sw).astype(BF16)],
          [jnp.where(lo, vv_sw, 0.0).astype(BF16), jnp.where(lo, 0.0, vv).astype(BF16)]]
    qi = lax.broadcasted_iota(jnp.int32, (WINDOW, 2 * WINDOW), 0)
    ci = lax.broadcasted_iota(jnp.int32, (WINDOW, 2 * WINDOW), 1)
    cur = (ci >= WINDOW) & ((ci - WINDOW) <= qi)
    prev = (ci < WINDOW) & (ci > qi)
    for b in range(nblk):
        mask = (prev & (step > 0)) | cur if b == 0 else prev | cur
        rows = slice(b * WINDOW, (b + 1) * WINDOW)
        krows = slice(b * WINDOW, (b + 2) * WINDOW)
        for t in range(HQ_D // 2):
            g = (2 * t) // rep
            qt = q[rows, t * 128:(t + 1) * 128]
            acc = None
            for half in range(2):
                s = lax.dot_general(qt, kx[g][half][krows], (((1,), (1,)), ((), ())),
                                    preferred_element_type=F32) * (HD_D ** -0.5)
                s = jnp.where(mask, s, -jnp.inf)
                sink = sink_ref[2 * t + half:2 * t + half + 1, 0:1]
                m = jnp.maximum(jnp.max(s, axis=-1, keepdims=True), sink)
                p = jnp.exp(s - m)
                den = jnp.sum(p, axis=-1, keepdims=True) + jnp.exp(sink - m)
                part = jnp.dot(p.astype(BF16), vx[g][half][krows], preferred_element_type=F32) / den
                acc = part if acc is None else acc + part
            o_ref[rows, t * 128:(t + 1) * 128] = acc
    r = nblk * WINDOW
    kprev_ref[...] = k[r - WINDOW:, :]
    vprev_ref[...] = v[r - WINDOW:, :]
    kout_ref[0] = k[r - WINDOW:, :]
    vout_ref[0] = v[r - WINDOW:, :]


def _swa_prompt(proj, pos, qn_g, kn_g, sinks, *, n_seq, seq_len, nblk):
    t = proj.shape[0]
    r = nblk * WINDOW
    steps = seq_len // r
    cos, sin = _rope_tables(pos, ROT_DIM_D, ROPE_THETA, HD_D, 2)
    kcol = (4 * H_C * DK_C + HQ_D * HD_D) // 128
    rowb = lambda w, j: pl.BlockSpec((r, w), lambda b, i, j=j: (b * steps + i, j))
    full = lambda shape: pl.BlockSpec(shape, lambda b, i: (0,) * len(shape))
    kv_out = pl.BlockSpec((1, WINDOW, 128), lambda b, i: (b, 0, 0))
    return pl.pallas_call(
        functools.partial(_swa_prompt_body, nblk=nblk),
        grid=(n_seq, steps),
        in_specs=[rowb(512, 4), rowb(128, kcol), rowb(128, kcol + 1),
                  pl.BlockSpec((r, 128), lambda b, i: (i, 0)), pl.BlockSpec((r, 128), lambda b, i: (i, 0)),
                  full((1, 512)), full((1, 128)), full((HQ_D, 128))],
        out_specs=[rowb(512, 0), kv_out, kv_out],
        out_shape=[jax.ShapeDtypeStruct((t, 512), F32), jax.ShapeDtypeStruct((n_seq, WINDOW, 128), F32),
                   jax.ShapeDtypeStruct((n_seq, WINDOW, 128), F32)],
        scratch_shapes=[pltpu.VMEM((WINDOW, 128), F32), pltpu.VMEM((WINDOW, 128), F32)],
        compiler_params=_cparams(("arbitrary", "arbitrary")),
        name="swa_prompt",
    )(proj, proj, proj, cos, sin, jnp.tile(qn_g, HQ_D).reshape(1, 512), jnp.tile(kn_g, HKV_D).reshape(1, 128),
      jnp.broadcast_to(sinks[:, None], (HQ_D, 128)))


def _swa_sample_body(q_ref, k_ref, v_ref, kc_ref, vc_ref, cos_ref, sin_ref, qng_ref, kng_ref, sink_ref,
                     o_ref, kout_ref, vout_ref, qs_ref, ks_ref, *, nseq, l):
    rep = HQ_D // HKV_D
    cos = cos_ref[...]
    sin = sin_ref[...]
    qs_ref[...] = _rope_partial(_head_norm(q_ref[...], qng_ref[...]), jnp.concatenate([cos] * (HQ_D // 2), axis=1),
                                jnp.concatenate([sin] * (HQ_D // 2), axis=1))
    ks_ref[...] = _rope_partial(_head_norm(k_ref[...], kng_ref[...]), cos, sin)
    qi = lax.broadcasted_iota(jnp.int32, (rep * l, WINDOW), 0) % l
    c1 = lax.broadcasted_iota(jnp.int32, (rep * l, WINDOW), 1)
    mask1 = c1 > qi
    qi2 = lax.broadcasted_iota(jnp.int32, (rep * l, l), 0) % l
    c2 = lax.broadcasted_iota(jnp.int32, (rep * l, l), 1)
    mask2 = c2 <= qi2

    def seq(s, carry_val):
        rows = pl.ds(pl.multiple_of(s * l, l), l)
        q = qs_ref[rows, :]
        k = ks_ref[rows, :]
        v = v_ref[rows, :]
        kb = kc_ref[s]
        vb = vc_ref[s]
        outs = []
        for g in range(HKV_D):
            gs = slice(g * HD_D, (g + 1) * HD_D)
            qg = jnp.concatenate([q[:, (g * rep + r) * HD_D:(g * rep + r + 1) * HD_D] for r in range(rep)], axis=0)
            s1 = jnp.where(mask1, _dot_nt(qg, kb[:, gs]) * (HD_D ** -0.5), -jnp.inf)
            s2 = jnp.where(mask2, _dot_nt(qg, k[:, gs]) * (HD_D ** -0.5), -jnp.inf)
            sink = sink_ref[g]
            m = jnp.maximum(jnp.maximum(jnp.max(s1, axis=-1, keepdims=True), jnp.max(s2, axis=-1, keepdims=True)),
                            sink[:, 0:1])
            p1 = jnp.exp(s1 - m)
            p2 = jnp.exp(s2 - m)
            den = (jnp.sum(p1, axis=-1, keepdims=True) + jnp.sum(p2, axis=-1, keepdims=True)
                   + jnp.exp(sink[:, 0:1] - m))
            og = (_dot(p1, vb[:, gs]) + _dot(p2, v[:, gs])) / den
            outs.extend([og[r * l:(r + 1) * l, :] for r in range(rep)])
        o_ref[rows, :] = jnp.concatenate(outs, axis=1)
        kout_ref[s, 0:WINDOW - l, :] = kb[l:, :]
        kout_ref[s, WINDOW - l:, :] = k
        vout_ref[s, 0:WINDOW - l, :] = vb[l:, :]
        vout_ref[s, WINDOW - l:, :] = v
        return carry_val

    lax.fori_loop(0, nseq, seq, 0)


def _swa_sample(proj, pos, kcache, vcache, qn_g, kn_g, sinks, *, nseq):
    t = proj.shape[0]
    l = pos.shape[0]
    r = nseq * l
    rep = HQ_D // HKV_D
    cos, sin = _rope_tables(pos, ROT_DIM_D, ROPE_THETA, HD_D, 2)
    cos, sin = jnp.tile(cos, (nseq, 1)), jnp.tile(sin, (nseq, 1))
    sink_rows = jnp.broadcast_to(jnp.repeat(sinks.reshape(HKV_D, rep), l, axis=1)[:, :, None], (HKV_D, rep * l, 128))
    rowb = lambda w, j: pl.BlockSpec((r, w), lambda i, j=j: (i, j))
    full = lambda shape: pl.BlockSpec(shape, lambda i: (0,) * len(shape))
    cache = pl.BlockSpec((nseq, WINDOW, 128), lambda i: (i, 0, 0))
    return pl.pallas_call(
        functools.partial(_swa_sample_body, nseq=nseq, l=l),
        grid=(t // r,),
        in_specs=[rowb(512, 4), rowb(128, 20), rowb(128, 21), cache, cache,
                  full((r, 128)), full((r, 128)), full((1, 512)), full((1, 128)), full((HKV_D, rep * l, 128))],
        out_specs=[rowb(512, 0), cache, cache],
        out_shape=[jax.ShapeDtypeStruct((t, 512), F32), jax.ShapeDtypeStruct(kcache.shape, F32),
                   jax.ShapeDtypeStruct(vcache.shape, F32)],
        scratch_shapes=[pltpu.VMEM((r, 512), F32), pltpu.VMEM((r, 128), F32)],
        compiler_params=_cparams(("arbitrary",)),
        name="swa_sample",
    )(proj, proj, proj, kcache, vcache, cos, sin, jnp.tile(qn_g, HQ_D).reshape(1, 512),
      jnp.tile(kn_g, HKV_D).reshape(1, 128), sink_rows)


def _mem_kv_body(m_ref, w_ref, g_ref, k_ref, v_ref):
    kv = jnp.dot(m_ref[...].astype(BF16), w_ref[...], preferred_element_type=F32)
    hw = H_X * HD_X
    for h in range(H_X):
        hs = slice(h * HD_X, (h + 1) * HD_X)
        k_ref[:, hs] = _rms(kv[:, hs], g_ref[...])
    v_ref[...] = kv[:, hw:]


def _mem_kv(mem, w, g):
    t = mem.shape[0]
    hw = H_X * HD_X
    return pl.pallas_call(
        _mem_kv_body,
        out_shape=[jax.ShapeDtypeStruct((t, hw), F32), jax.ShapeDtypeStruct((t, hw), F32)],
        compiler_params=pltpu.CompilerParams(vmem_limit_bytes=VMEM_LIMIT_BYTES),
        name="mem_kv",
    )(mem, w, g.reshape(1, HD_X))


def _mem_attend_body(x_ref, g_ref, wq_ref, qng_ref, mk_ref, mv_ref, wo_ref, y_ref, att_ref, *, nseq):
    x = x_ref[...]
    q = jnp.dot(_rms(x, g_ref[...]).astype(BF16), wq_ref[...], preferred_element_type=F32)
    rows_per = x.shape[0] // nseq
    for s in range(nseq):
        rs = slice(s * rows_per, (s + 1) * rows_per)
        for h in range(H_X):
            hs = slice(h * HD_X, (h + 1) * HD_X)
            qh = _rms(q[rs, hs], qng_ref[...])
            sc = _dot_nt(qh, mk_ref[s, :, hs]) * (HD_X ** -0.5)
            m = jnp.max(sc, axis=-1, keepdims=True)
            p = jnp.exp(sc - m)
            att_ref[rs, hs] = _dot(p, mv_ref[s, :, hs]) / jnp.sum(p, axis=-1, keepdims=True)
    y_ref[...] = x + jnp.dot(att_ref[...].astype(BF16), wo_ref[...], preferred_element_type=F32)


def _mem_attend(x, g, wq, qn_g, mk, mv, wo, *, tm, nseq, tiles_per_mem):
    t, d = x.shape
    hw = H_X * HD_X
    full = lambda shape: pl.BlockSpec(shape, lambda i: (0,) * len(shape))
    mem_spec = pl.BlockSpec((nseq, N_MEM, hw), lambda i: (i // tiles_per_mem, 0, 0))
    return pl.pallas_call(
        functools.partial(_mem_attend_body, nseq=nseq),
        grid=(t // tm,),
        in_specs=[pl.BlockSpec((tm, d), lambda i: (i, 0)), full((1, d)), full((d, hw)), full((1, HD_X)),
                  mem_spec, mem_spec, full((hw, d))],
        out_specs=pl.BlockSpec((tm, d), lambda i: (i, 0)),
        out_shape=jax.ShapeDtypeStruct((t, d), F32),
        scratch_shapes=[pltpu.VMEM((tm, hw), F32)],
        compiler_params=_cparams(("arbitrary",)),
        name="mem_attend",
    )(x, g.reshape(1, d), wq, qn_g.reshape(1, HD_X), mk, mv, wo)


def _mem_attend_cached_body(x_ref, g_ref, wq_ref, qng_ref, mk_ref, mv_ref, wo_ref, y_ref, att_ref, *, nseq):
    x = x_ref[...]
    q = jnp.dot(_rms(x, g_ref[...]).astype(BF16), wq_ref[...], preferred_element_type=F32)
    l = x.shape[0] // nseq
    rows_all = H_X * l
    cols = N_MEM * H_X
    own = (lax.broadcasted_iota(jnp.int32, (rows_all, cols), 1) % H_X
           == lax.broadcasted_iota(jnp.int32, (rows_all, cols), 0) // l)
    for s in range(nseq):
        rs = slice(s * l, (s + 1) * l)
        qs = jnp.concatenate([_rms(q[rs, h * HD_X:(h + 1) * HD_X], qng_ref[...]) for h in range(H_X)], axis=0)
        sc = jnp.where(own, _dot_nt(qs, mk_ref[0, s]) * (HD_X ** -0.5), -jnp.inf)
        m = jnp.max(sc, axis=-1, keepdims=True)
        p = jnp.exp(sc - m)
        o = _dot(p, mv_ref[0, s]) / jnp.sum(p, axis=-1, keepdims=True)
        for h in range(H_X):
            att_ref[rs, h * HD_X:(h + 1) * HD_X] = o[h * l:(h + 1) * l, :]
    y_ref[...] = x + jnp.dot(att_ref[...].astype(BF16), wo_ref[...], preferred_element_type=F32)


def _mem_attend_cached(x, g, wq, qn_g, mk, mv, wo, *, layer, nseq):
    t, d = x.shape
    hw = H_X * HD_X
    tm = nseq * (t // mk.shape[1])
    full = lambda shape: pl.BlockSpec(shape, lambda i: (0,) * len(shape))
    mem_spec = pl.BlockSpec((1, nseq, N_MEM * H_X, HD_X), lambda i: (layer, i, 0, 0))
    return pl.pallas_call(
        functools.partial(_mem_attend_cached_body, nseq=nseq),
        grid=(t // tm,),
        in_specs=[pl.BlockSpec((tm, d), lambda i: (i, 0)), full((1, d)), full((d, hw)), full((1, HD_X)),
                  mem_spec, mem_spec, full((hw, d))],
        out_specs=pl.BlockSpec((tm, d), lambda i: (i, 0)),
        out_shape=jax.ShapeDtypeStruct((t, d), F32),
        scratch_shapes=[pltpu.VMEM((tm, hw), F32)],
        compiler_params=_cparams(("arbitrary",)),
        name="mem_attend_cached",
    )(x, g.reshape(1, d), wq, qn_g.reshape(1, HD_X), mk, mv, wo)


def _ffn_body(*refs, tm, ffc, tiles_per_seq, per_row_state):
    if per_row_state:
        x_ref, g_ref, wg_ref, wu_ref, cw_ref, cb_ref, wd_ref, c1_ref, c2_ref, y_ref, gt_ref, tail_ref = refs
    else:
        x_ref, g_ref, wg_ref, wu_ref, cw_ref, cb_ref, wd_ref, y_ref, gt_ref, tail_ref = refs
    ff = wg_ref.shape[1]
    x = x_ref[...]
    h = _rms(x, g_ref[...]).astype(BF16)
    if not per_row_state:
        @pl.when(pl.program_id(0) % tiles_per_seq == 0)
        def _():
            tail_ref[...] = jnp.zeros_like(tail_ref)

    row8 = lax.broadcasted_iota(jnp.int32, (8, ffc), 0)
    acc = None
    for c0 in range(0, ff, ffc):
        cs = slice(c0, c0 + ffc)
        gate = jnp.dot(h, wg_ref[:, cs], preferred_element_type=F32)
        up = jnp.dot(h, wu_ref[:, cs], preferred_element_type=F32)
        r1 = pltpu.roll(gate, 1, 0)
        r2 = pltpu.roll(gate, 2, 0)
        if per_row_state:
            pos = lax.broadcasted_iota(jnp.int32, gate.shape, 0) % 8
            g1 = jnp.where(pos == 0, c1_ref[:, cs], r1)
            g2 = jnp.where(pos < 2, c2_ref[:, cs], r2)
            gt_ref[:, cs] = gate
        else:
            prev = tail_ref[:, cs]
            top1 = jnp.where(row8 == 0, prev[7:8, :], r1[0:8, :])
            top2 = jnp.where(row8 == 0, prev[6:7, :], jnp.where(row8 == 1, prev[7:8, :], r2[0:8, :]))
            g1 = jnp.concatenate([top1, r1[8:, :]], axis=0)
            g2 = jnp.concatenate([top2, r2[8:, :]], axis=0)
            tail_ref[:, cs] = gate[tm - 8:, :]
            gt_ref[:, cs] = gate[tm - 8:, :]
        conv = cb_ref[:, cs] + cw_ref[0:1, cs] * g2 + cw_ref[1:2, cs] * g1 + cw_ref[2:3, cs] * gate
        act = (_gelu(conv) * up).astype(BF16)
        part = jnp.dot(act, wd_ref[cs, :], preferred_element_type=F32)
        acc = part if acc is None else acc + part
    y_ref[...] = x + acc


def _ffn(x, g, wg, wu, cw, cb, wd, *, tm, ffc, tiles_per_seq, c1=None, c2=None):
    t, d = x.shape
    ff = wg.shape[1]
    per_row_state = c1 is not None
    tail_n = tm if per_row_state else 8
    resident = lambda shape: pl.BlockSpec(shape, lambda i: (0,) * len(shape), pipeline_mode=pl.Buffered(1))
    in_specs = [pl.BlockSpec((tm, d), lambda i: (i, 0)), resident((1, d)), resident((d, ff)), resident((d, ff)),
                resident((8, ff)), resident((1, ff)), resident((ff, d))]
    args = [x, g.reshape(1, d), wg, wu, jnp.pad(cw, ((0, 8 - CONV_W), (0, 0))), cb.reshape(1, ff), wd]
    if per_row_state:
        in_specs += [pl.BlockSpec((tm, ff), lambda i: (i, 0)), pl.BlockSpec((tm, ff), lambda i: (i, 0))]
        args += [c1, c2]
    return pl.pallas_call(
        functools.partial(_ffn_body, tm=tm, ffc=ffc, tiles_per_seq=tiles_per_seq, per_row_state=per_row_state),
        grid=(t // tm,),
        in_specs=in_specs,
        out_specs=[pl.BlockSpec((tm, d), lambda i: (i, 0)), pl.BlockSpec((tail_n, ff), lambda i: (i, 0))],
        out_shape=[jax.ShapeDtypeStruct((t, d), F32), jax.ShapeDtypeStruct((t // tm * tail_n, ff), F32)],
        scratch_shapes=[pltpu.VMEM((8, ff), F32)],
        compiler_params=_cparams(("arbitrary",)),
        name="ffn",
    )(*args)


def _run_group(x, pos, n_seq, seq_len, mem, st, w):
    prompt = st is None
    t = x.shape[0]
    tm = 512 if prompt else t
    out = {k: [] for k in ("ret", "chunk_v", "hgrn", "swa_k", "swa_v", "mem_k", "mem_v", "conv")}
    for l in range(DEPTH):
        j = l // 2
        if l % 2 == 0:
            proj = _norm_proj(x, w["norm_mix_g"][l], w["ev_w_in"][j], tm)
            if prompt:
                c = math.gcd(seq_len, RET_CHUNK)
                s0 = jnp.zeros((n_seq, H_A, DK_A, DV_A), F32)
                o_a, o_b, s_new = _even_mixer(proj, pos, s0, w["ret_gn_g"][j], w["mlp_norm_g"][j], w["mlp_w_s"][j],
                                              w["mlp_b_s"][j], n_seq=n_seq, seq_len=seq_len, c=c, nseg=4, carry=True,
                                              emit_v=False)
            else:
                o_a, o_b, s_new, v_rows = _even_mixer(proj, pos, st["ret"][j], w["ret_gn_g"][j], w["mlp_norm_g"][j],
                                                      w["mlp_w_s"][j], w["mlp_b_s"][j], n_seq=n_seq, seq_len=seq_len,
                                                      c=seq_len, nseg=16, carry=False, emit_v=True)
                out["chunk_v"].append(v_rows.reshape(n_seq, seq_len, G_B, DG_B))
            out["ret"].append(s_new)
            x = _out_proj(o_a, o_b, w["ev_w_out"][j], x, tm)
        else:
            proj = _norm_proj(x, w["norm_mix_g"][l], w["od_w_in"][j], tm)
            if prompt:
                c = math.gcd(seq_len, HGRN_CHUNK)
                s0 = jnp.zeros((n_seq, H_C, DK_C, DV_C), F32)
                o_c, s_new = _hgrn_mixer(proj, w["hgrn_lb_logits"], w["hgrn_onorm_g"][j], s0, n_seq=n_seq,
                                         seq_len=seq_len, c=c, nseg=8, carry=True, layer=l)
                o_d, k_new, v_new = _swa_prompt(proj, pos, w["swa_qnorm_g"][j], w["swa_knorm_g"][j], w["swa_sinks"][j],
                                                n_seq=n_seq, seq_len=seq_len, nblk=4)
            else:
                o_c, s_new = _hgrn_mixer(proj, w["hgrn_lb_logits"], w["hgrn_onorm_g"][j], st["hgrn"][j], n_seq=n_seq,
                                         seq_len=seq_len, c=seq_len, nseg=16, carry=False, layer=l)
                o_d, k_new, v_new = _swa_sample(proj, pos, st["swa_k"][j].reshape(n_seq, WINDOW, HKV_D * HD_D),
                                                st["swa_v"][j].reshape(n_seq, WINDOW, HKV_D * HD_D),
                                                w["swa_qnorm_g"][j], w["swa_knorm_g"][j], w["swa_sinks"][j], nseq=16)
            out["hgrn"].append(s_new)
            out["swa_k"].append(k_new.reshape(n_seq, WINDOW, HKV_D, HD_D))
            out["swa_v"].append(v_new.reshape(n_seq, WINDOW, HKV_D, HD_D))
            x = _out_proj(o_c, o_d, w["od_w_out"][j], x, tm)
        hw = H_X * HD_X
        if prompt:
            mk, mv = _mem_kv(mem, w["mem_w_kv"][l], w["mem_knorm_g"][l])
            out["mem_k"].append(mk.reshape(n_seq, N_MEM, H_X, HD_X))
            out["mem_v"].append(mv.reshape(n_seq, N_MEM, H_X, HD_X))
            x = _mem_attend(x, w["norm_mem_g"][l], w["mem_w_q"][l], w["mem_qnorm_g"][l], mk.reshape(n_seq, N_MEM, hw),
                            mv.reshape(n_seq, N_MEM, hw), w["mem_w_o"][l], tm=tm, nseq=1, tiles_per_mem=seq_len // tm)
        else:
            x = _mem_attend_cached(x, w["norm_mem_g"][l], w["mem_w_q"][l], w["mem_qnorm_g"][l],
                                   st["mem_k"].reshape(DEPTH, n_seq, N_MEM * H_X, HD_X),
                                   st["mem_v"].reshape(DEPTH, n_seq, N_MEM * H_X, HD_X), w["mem_w_o"][l],
                                   layer=l, nseq=8)
        ffw = (w["ffn_w_gate"][l], w["ffn_w_up"][l], w["ffn_conv_w"][l], w["ffn_conv_b"][l], w["ffn_w_down"][l])
        if prompt:
            x, gt = _ffn(x, w["norm_ffn_g"][l], *ffw, tm=tm, ffc=256, tiles_per_seq=seq_len // tm)
            tiles = seq_len // tm
            out["conv"].append(gt.reshape(n_seq, tiles, 8, D_FF)[:, -1, 8 - (CONV_W - 1):, :])
        else:
            c0 = st["conv"][l]
            z = jnp.zeros((n_seq, seq_len - 2, D_FF), F32)
            c2 = jnp.concatenate([c0, z], axis=1).reshape(t, D_FF)
            c1 = jnp.concatenate([c0[:, 1:2], jnp.zeros((n_seq, seq_len - 1, D_FF), F32)], axis=1).reshape(t, D_FF)
            x, gt = _ffn(x, w["norm_ffn_g"][l], *ffw, tm=256, ffc=256, tiles_per_seq=1, c1=c1, c2=c2)
            out["conv"].append(gt.reshape(n_seq, seq_len, D_FF)[:, seq_len - (CONV_W - 1):, :])
    return x, {name: jnp.stack(rows) for name, rows in out.items() if rows}


def kernel(x_prompt, x_sample, state_ret, state_hgrn, cache_swa_k, cache_swa_v, cache_mem_k, cache_mem_v,
           state_ffn_conv, mem_prompt, norm_mix_g, norm_mem_g, norm_ffn_g, ev_w_in, ev_w_out, ret_gn_g,
           mlp_norm_g, mlp_w_s, mlp_b_s, od_w_in, od_w_out, hgrn_lb_logits, hgrn_onorm_g, swa_qnorm_g,
           swa_knorm_g, swa_sinks, mem_w_q, mem_w_kv, mem_qnorm_g, mem_knorm_g, mem_w_o, ffn_w_gate,
           ffn_w_up, ffn_conv_w, ffn_conv_b, ffn_w_down):
    bf = lambda a: a.astype(BF16)
    w = dict(norm_mix_g=norm_mix_g, norm_mem_g=norm_mem_g, norm_ffn_g=norm_ffn_g, ev_w_in=bf(ev_w_in),
             ev_w_out=bf(ev_w_out), ret_gn_g=ret_gn_g, mlp_norm_g=mlp_norm_g, mlp_w_s=mlp_w_s, mlp_b_s=mlp_b_s,
             od_w_in=bf(od_w_in), od_w_out=bf(od_w_out), hgrn_lb_logits=hgrn_lb_logits, hgrn_onorm_g=hgrn_onorm_g,
             swa_qnorm_g=swa_qnorm_g, swa_knorm_g=swa_knorm_g, swa_sinks=swa_sinks, mem_w_q=bf(mem_w_q),
             mem_w_kv=bf(mem_w_kv), mem_qnorm_g=mem_qnorm_g, mem_knorm_g=mem_knorm_g, mem_w_o=bf(mem_w_o),
             ffn_w_gate=bf(ffn_w_gate), ffn_w_up=bf(ffn_w_up), ffn_conv_w=ffn_conv_w, ffn_conv_b=ffn_conv_b,
             ffn_w_down=bf(ffn_w_down))
    b, seq, d = x_prompt.shape
    db, dseq, _ = x_sample.shape
    pos_prompt = jnp.arange(seq, dtype=jnp.int32)
    pos_sample = PAST_LEN + jnp.arange(dseq, dtype=jnp.int32)
    y_p, ns_p = _run_group(x_prompt.reshape(b * seq, d), pos_prompt, b, seq, mem_prompt.reshape(b * N_MEM, d), None, w)
    st = dict(ret=state_ret, hgrn=state_hgrn, swa_k=cache_swa_k, swa_v=cache_swa_v, mem_k=cache_mem_k,
              mem_v=cache_mem_v, conv=state_ffn_conv)
    y_s, ns_s = _run_group(x_sample.reshape(db * dseq, d), pos_sample, db, dseq, None, st, w)
    return (y_p.reshape(b, seq, d), y_s.reshape(db, dseq, d), ns_p["ret"], ns_p["hgrn"], ns_p["swa_k"], ns_p["swa_v"],
            ns_p["mem_k"], ns_p["mem_v"], ns_p["conv"], ns_s["ret"], ns_s["chunk_v"], ns_s["hgrn"], ns_s["swa_k"],
            ns_s["swa_v"], ns_s["conv"])
```

```python
import functools
import math

import jax
import jax.numpy as jnp
import numpy as np
from jax import lax
from jax.experimental import pallas as pl
from jax.experimental.pallas import tpu as pltpu

F32 = jnp.float32
BF16 = jnp.bfloat16

D_MODEL = 1024
DEPTH = 2
PAST_LEN = 16384
H_A, DV_A, DK_A = 4, 128, 64
RET_CHUNK = 128
RET_THETA = 10000.0
G_B, DG_B = 4, 128
MLP_CHUNK = 128
H_C, DK_C, DV_C = 4, 128, 128
HGRN_CHUNK = 64
HGRN_SUB = 8
HD_D, HQ_D, HKV_D = 64, 8, 2
WINDOW = 128
ROPE_THETA = 500000.0
ROT_DIM_D = HD_D // 4
N_MEM, H_X, HD_X = 256, 4, 128
D_FF = 2816
CONV_W = 3
FFN_DOWN_GROUP = 4
EPS = 1e-6
SQRT_HALF = float(np.sqrt(0.5))
EV_Q, EV_K, EV_V, EV_G, EV_U, EV_VB, EV_END = 0, 256, 512, 1024, 1536, 2048, 2560
OD_Q, OD_F, OD_I, OD_G, OD_QD, OD_KD, OD_VD, OD_END = 0, 512, 1024, 1536, 2048, 2560, 2688, 2816

VMEM_LIMIT_BYTES = 56 * 1024 * 1024


def _cparams(sem):
    return pltpu.CompilerParams(dimension_semantics=sem, vmem_limit_bytes=VMEM_LIMIT_BYTES)


def _resident(shape):
    return pl.BlockSpec(shape, lambda *_: (0,) * len(shape), pipeline_mode=pl.Buffered(1))


def _rms(x, g):
    return x * lax.rsqrt(jnp.mean(x * x, axis=-1, keepdims=True) + EPS) * g


def _ln(x, g):
    mu = jnp.mean(x, axis=-1, keepdims=True)
    xc = x - mu
    return xc * lax.rsqrt(jnp.mean(xc * xc, axis=-1, keepdims=True) + EPS) * g


def _sigmoid(x):
    return 1.0 / (1.0 + jnp.exp(-x))


def _silu(x):
    return x * _sigmoid(x)


def _gelu(x):
    return 0.5 * x * (1.0 + lax.erf(x * SQRT_HALF))


def _dot(a, b):
    return jnp.dot(a.astype(BF16), b.astype(BF16), preferred_element_type=F32)


def _dot_nt(a, b):
    return lax.dot_general(a.astype(BF16), b.astype(BF16), (((1,), (1,)), ((), ())), preferred_element_type=F32)


def _dot_tn(a, b):
    return lax.dot_general(a.astype(BF16), b.astype(BF16), (((0,), (0,)), ((), ())), preferred_element_type=F32)


def _split3(x):
    hi = x.astype(BF16)
    r1 = x - hi.astype(F32)
    mid = r1.astype(BF16)
    lo = (r1 - mid.astype(F32)).astype(BF16)
    return hi, mid, lo


def _norm_proj_body(x_ref, g_ref, w_ref, o_ref):
    h = _rms(x_ref[...], g_ref[...]).astype(BF16)
    n = o_ref.shape[1]
    for n0 in range(0, n, 512):
        n1 = min(n0 + 512, n)
        o_ref[:, n0:n1] = jnp.dot(h, w_ref[:, n0:n1], preferred_element_type=F32)


def _norm_proj(x, g, w, tm):
    t, d = x.shape
    n = w.shape[1]
    return pl.pallas_call(
        _norm_proj_body,
        grid=(t // tm,),
        in_specs=[pl.BlockSpec((tm, d), lambda i: (i, 0)),
                  pl.BlockSpec((1, d), lambda i: (0, 0)),
                  pl.BlockSpec((d, n), lambda i: (0, 0))],
        out_specs=pl.BlockSpec((tm, n), lambda i: (i, 0)),
        out_shape=jax.ShapeDtypeStruct((t, n), F32),
        compiler_params=_cparams(("arbitrary",)),
        name="norm_proj",
    )(x, g.reshape(1, d), w)


def _out_proj_body(*refs):
    *o_refs, w_ref, x_ref, y_ref = refs
    acc = x_ref[...]
    k0 = 0
    for o_ref in o_refs:
        k1 = k0 + o_ref.shape[1]
        acc = acc + jnp.dot(o_ref[...].astype(BF16), w_ref[k0:k1, :], preferred_element_type=F32)
        k0 = k1
    y_ref[...] = acc


def _out_proj(os, w, x, tm):
    t, d = x.shape
    return pl.pallas_call(
        _out_proj_body,
        grid=(t // tm,),
        in_specs=[pl.BlockSpec((tm, o.shape[1]), lambda i: (i, 0)) for o in os]
        + [pl.BlockSpec(w.shape, lambda i: (0, 0)), pl.BlockSpec((tm, d), lambda i: (i, 0))],
        out_specs=pl.BlockSpec((tm, d), lambda i: (i, 0)),
        out_shape=jax.ShapeDtypeStruct((t, d), F32),
        compiler_params=_cparams(("arbitrary",)),
        name="out_proj",
    )(*os, w, x)


def _retention_consts(c):
    lg = np.log1p(-np.exp2(-5.0 - np.arange(H_A, dtype=np.float64)))
    idx = np.arange(c, dtype=np.float64)
    rel = idx[:, None] - idx[None, :]
    dmask = np.where(rel >= 0, np.exp(rel[None] * lg[:, None, None]), 0.0)
    q_dec = np.exp((idx + 1.0)[None, :] * lg[:, None])
    k_dec = np.exp((c - 1.0 - idx)[None, :] * lg[:, None])
    c_dec = np.exp(c * lg)
    qdec = np.repeat(q_dec.T, DK_A, axis=1)
    kdec = np.repeat(k_dec.T, DK_A, axis=1)
    cdec = np.broadcast_to(c_dec[:, None, None], (H_A, 8, DV_A))
    return tuple(jnp.asarray(a, F32) for a in (dmask, qdec, kdec, cdec))


def _rope_tables(pos, rot_dim, theta, head_dim, reps, row_reps=1):
    half = rot_dim // 2
    inv = theta ** (-np.arange(half, dtype=np.float64) / half)
    ang = pos.astype(np.float64)[:, None] * inv[None, :]
    cos = np.cos(ang)
    sin = np.sin(ang)
    l = pos.shape[0]
    pad = head_dim - rot_dim
    cos_h = np.concatenate([cos, cos, np.ones((l, pad))], axis=1)
    sin_h = np.concatenate([-sin, sin, np.zeros((l, pad))], axis=1)
    return (jnp.asarray(np.tile(cos_h, (row_reps, reps)), F32), jnp.asarray(np.tile(sin_h, (row_reps, reps)), F32))


def _even_consts(c, gn_g, mn_g, w_s, b_s):
    dmask, qdec, kdec, cdec = _retention_consts(c)
    bs_full = jnp.repeat(b_s[:, :c].T, DG_B, axis=1)
    arrays = [dmask, qdec, kdec, cdec, gn_g.reshape(1, H_A * DV_A), mn_g.reshape(1, G_B * DG_B), w_s[:, :c, :c], bs_full]
    return arrays, [a.shape for a in arrays]


def _even_segment(pj, cos, sin, consts, s_old):
    dmask_ref, qdec_ref, kdec_ref, cdec_ref, gng_ref, mng_ref, ws_ref, bs_ref = consts
    c = pj.shape[0]
    width = H_A * DK_A
    lane = lax.broadcasted_iota(jnp.int32, (c, width), 1)
    first_half = (lane % DK_A) < (DK_A // 2)
    tril = lax.broadcasted_iota(jnp.int32, (c, c), 1) <= lax.broadcasted_iota(jnp.int32, (c, c), 0)

    def rope(x):
        rot = jnp.where(first_half, pltpu.roll(x, width - DK_A // 2, 1), pltpu.roll(x, DK_A // 2, 1))
        return x * cos + rot * sin

    q = rope(pj[:, EV_Q:EV_K])
    k = rope(pj[:, EV_K:EV_V]) * (DK_A ** -0.5)
    v = pj[:, EV_V:EV_G]
    ga = pj[:, EV_G:EV_U]
    qd = q * qdec_ref[...]
    kd = k * kdec_ref[...]
    outs, s_new = [], []
    for h in range(H_A):
        ks = slice(h * DK_A, (h + 1) * DK_A)
        vs = slice(h * DV_A, (h + 1) * DV_A)
        vh = v[:, vs]
        a = _dot_nt(q[:, ks], k[:, ks]) * dmask_ref[h]
        o = _dot(a, vh) + _dot(qd[:, ks], s_old[h])
        s_new.append(cdec_ref[h, 0:1, :] * s_old[h] + _dot_tn(kd[:, ks], vh))
        outs.append(_silu(ga[:, vs]) * _ln(o, gng_ref[:, vs]))
    u = _gelu(pj[:, EV_U:EV_VB])
    vg = _gelu(pj[:, EV_VB:EV_END])
    vrows = []
    for g in range(G_B):
        gs = slice(g * DG_B, (g + 1) * DG_B)
        vr = _ln(vg[:, gs], mng_ref[:, gs])
        vrows.append(vr)
        w = jnp.where(tril, ws_ref[g], 0.0)
        outs.append(u[:, gs] * (_dot(w, vr) + bs_ref[:, gs]))
    return jnp.concatenate(outs, axis=1), s_new, jnp.concatenate(vrows, axis=1)


def _even_body(pj_ref, cos_ref, sin_ref, dmask_ref, qdec_ref, kdec_ref, cdec_ref, gng_ref, mng_ref, ws_ref, bs_ref,
               s0_ref, o_ref, sout_ref, vrows_ref, *, c, nseg):
    consts = (dmask_ref, qdec_ref, kdec_ref, cdec_ref, gng_ref, mng_ref, ws_ref, bs_ref)

    def seg(s, carry_val):
        rows = pl.ds(pl.multiple_of(s * c, c), c)
        o, s_new, vrows = _even_segment(pj_ref[rows, :], cos_ref[rows, :], sin_ref[rows, :], consts,
                                        [s0_ref[s, h] for h in range(H_A)])
        for h in range(H_A):
            sout_ref[s, h] = s_new[h]
        o_ref[rows, :] = o
        vrows_ref[rows, :] = vrows
        return carry_val

    lax.fori_loop(0, nseg, seg, 0)


def _even_mixer(proj, pos, s0, gn_g, mn_g, w_s, b_s, *, nseg):
    t = proj.shape[0]
    c = pos.shape[0]
    r = nseg * c
    consts, shapes = _even_consts(c, gn_g, mn_g, w_s, b_s)
    cos, sin = _rope_tables(pos, DK_A, RET_THETA, DK_A, H_A, row_reps=nseg)
    full = lambda shape: pl.BlockSpec(shape, lambda i: (0,) * len(shape))
    st_spec = pl.BlockSpec((nseg, H_A, DK_A, DV_A), lambda i: (i, 0, 0, 0))
    return pl.pallas_call(
        functools.partial(_even_body, c=c, nseg=nseg),
        grid=(t // r,),
        in_specs=[pl.BlockSpec((r, EV_END), lambda i: (i, 0)), full((r, 256)), full((r, 256))]
        + [full(s) for s in shapes] + [st_spec],
        out_specs=[pl.BlockSpec((r, 1024), lambda i: (i, 0)), st_spec, pl.BlockSpec((r, 512), lambda i: (i, 0))],
        out_shape=[jax.ShapeDtypeStruct((t, 1024), F32), jax.ShapeDtypeStruct(s0.shape, F32),
                   jax.ShapeDtypeStruct((t, 512), F32)],
        compiler_params=_cparams(("arbitrary",)),
        name="even_mixer",
    )(proj, cos, sin, *consts, s0)


def _even_layer_body(x_ref, ng_ref, win_ref, wout_ref, cos_ref, sin_ref, dmask_ref, qdec_ref, kdec_ref, cdec_ref,
                     gng_ref, mng_ref, ws_ref, bs_ref, y_ref, sout_ref, st_ref, *, c, pb):
    step = pl.program_id(1)

    @pl.when(step == 0)
    def _():
        st_ref[...] = jnp.zeros_like(st_ref)

    consts = (dmask_ref, qdec_ref, kdec_ref, cdec_ref, gng_ref, mng_ref, ws_ref, bs_ref)
    x = x_ref[...]
    h = _rms(x, ng_ref[...]).astype(BF16)
    ngrp = x.shape[0] // pb
    project = lambda g: jnp.dot(h[g * pb:(g + 1) * pb, :], win_ref[...], preferred_element_type=F32)
    proj = project(0)
    for g in range(ngrp):
        nxt = project(g + 1) if g + 1 < ngrp else None
        for j in range(pb // c):
            rows = slice(g * pb + j * c, g * pb + (j + 1) * c)
            o, s_new, _ = _even_segment(proj[j * c:(j + 1) * c, :], cos_ref[rows, :], sin_ref[rows, :], consts,
                                        [st_ref[hh] for hh in range(H_A)])
            for hh in range(H_A):
                st_ref[hh] = s_new[hh]
            y_ref[rows, :] = x[rows, :] + jnp.dot(o.astype(BF16), wout_ref[...], preferred_element_type=F32)
        proj = nxt

    @pl.when(step == pl.num_programs(1) - 1)
    def _():
        sout_ref[0] = st_ref[...]


def _even_layer(x, pos, ng, w_in, w_out, gn_g, mn_g, w_s, b_s, *, n_seq, r, pb):
    t, d = x.shape
    seq_len = pos.shape[0]
    c = math.gcd(seq_len, RET_CHUNK)
    assert c == min(MLP_CHUNK, seq_len) and pb % c == 0 and r % pb == 0 and seq_len % r == 0
    steps = seq_len // r
    consts, shapes = _even_consts(c, gn_g, mn_g, w_s, b_s)
    cos, sin = _rope_tables(pos, DK_A, RET_THETA, DK_A, H_A)
    rowb = pl.BlockSpec((r, d), lambda b, i: (b * steps + i, 0))
    tab = pl.BlockSpec((r, 256), lambda b, i: (i, 0))
    st_spec = pl.BlockSpec((1, H_A, DK_A, DV_A), lambda b, i: (b, 0, 0, 0))
    return pl.pallas_call(
        functools.partial(_even_layer_body, c=c, pb=pb),
        grid=(n_seq, steps),
        in_specs=[rowb, _resident((1, d)), _resident(w_in.shape), _resident(w_out.shape), tab, tab]
        + [_resident(s) for s in shapes],
        out_specs=[rowb, st_spec],
        out_shape=[jax.ShapeDtypeStruct((t, d), F32), jax.ShapeDtypeStruct((n_seq, H_A, DK_A, DV_A), F32)],
        scratch_shapes=[pltpu.VMEM((H_A, DK_A, DV_A), F32)],
        compiler_params=_cparams(("arbitrary", "arbitrary")),
        name="even_layer",
    )(x, ng.reshape(1, d), w_in, w_out, cos, sin, *consts)


def _hgrn_lower_bound(lbl, layer):
    e = jnp.exp(lbl - jnp.max(lbl, axis=0, keepdims=True))
    sm = e / jnp.sum(e, axis=0, keepdims=True)
    acc = sm[0:1, :]
    first = acc
    for i in range(1, layer + 1):
        acc = acc + sm[i:i + 1, :]
    return acc - first


def _hgrn_segment(pj, lb, ong_ref, s_old):
    c = pj.shape[0]
    sub = min(HGRN_SUB, c)
    nsub = c // sub
    row = lax.broadcasted_iota(jnp.int32, (c, c), 0)
    colm = lax.broadcasted_iota(jnp.int32, (c, c), 1)
    tri = (colm <= row).astype(BF16)
    rowc = lax.broadcasted_iota(jnp.int32, (c, 1), 0)

    f = lb + (1.0 - lb) * _sigmoid(pj[:, OD_F:OD_I])
    g = jnp.log(f)
    k = 1.0 - f
    q = _silu(pj[:, OD_Q:OD_F])
    v = pj[:, OD_I:OD_G]
    gc = pj[:, OD_G:OD_QD]
    ghi, gmid, glo = _split3(g)
    b = (jnp.dot(tri, ghi, preferred_element_type=F32) + jnp.dot(tri, gmid, preferred_element_type=F32)
         + jnp.dot(tri, glo, preferred_element_type=F32))
    outs, s_new = [], []
    for h in range(H_C):
        hs = slice(h * DK_C, (h + 1) * DK_C)
        qh, kh, fh, bh, vh = q[:, hs], k[:, hs], f[:, hs], b[:, hs], v[:, hs]
        a = jnp.zeros((c, c), F32)
        p = None
        for d in range(sub):
            if d == 0:
                term = qh * kh
            else:
                rf = fh if d == 1 else pltpu.roll(fh, d - 1, 0)
                p = rf if d == 1 else p * rf
                term = qh * p * pltpu.roll(kh, d, 0)
            band = jnp.sum(term, axis=-1, keepdims=True)
            a = a + jnp.where((colm == row - d) & ((row % sub) >= d), band, 0.0)
        if nsub > 1:
            bend = jnp.concatenate(
                [jnp.broadcast_to(bh[(j + 1) * sub - 1:(j + 1) * sub, :], (sub, DK_C)) for j in range(nsub)], axis=0)
            kt = kh * jnp.exp(bend - bh)
            qs, ks = [], []
            for j in range(nsub - 1):
                be = bh[(j + 1) * sub - 1:(j + 1) * sub, :]
                later = rowc >= (j + 1) * sub
                own = (rowc >= j * sub) & (rowc < (j + 1) * sub)
                qs.append(jnp.where(later, qh * jnp.exp(jnp.minimum(bh - be, 0.0)), 0.0).astype(BF16))
                ks.append(jnp.where(own, kt, 0.0).astype(BF16))
            a = a + lax.dot_general(jnp.concatenate(qs, axis=1), jnp.concatenate(ks, axis=1),
                                    (((1,), (1,)), ((), ())), preferred_element_type=F32)
        o = _dot(a, vh) + _dot(qh * jnp.exp(bh), s_old[h])
        bl = bh[c - 1:c, :]
        decay = jnp.broadcast_to(jnp.exp(bl), (DK_C, DK_C)).T
        s_new.append(decay * s_old[h] + _dot_tn(kh * jnp.exp(bl - bh), vh))
        outs.append(_rms(o, ong_ref[:, hs]) * _silu(gc[:, hs]))
    return jnp.concatenate(outs, axis=1), s_new


def _hgrn_body(pj_ref, lbl_ref, ong_ref, s0_ref, oc_ref, sout_ref, *, c, nseg, layer):
    lb = _hgrn_lower_bound(lbl_ref[...], layer)

    def seg(s, carry_val):
        rows = pl.ds(pl.multiple_of(s * c, c), c)
        o, s_new = _hgrn_segment(pj_ref[rows, :], lb, ong_ref, [s0_ref[s, h] for h in range(H_C)])
        for h in range(H_C):
            sout_ref[s, h] = s_new[h]
        oc_ref[rows, :] = o
        return carry_val

    lax.fori_loop(0, nseg, seg, 0)


def _hgrn_mixer(proj, lb_logits, on_g, s0, *, c, nseg, layer):
    t = proj.shape[0]
    r = nseg * c
    full = lambda shape: pl.BlockSpec(shape, lambda i: (0,) * len(shape))
    st_spec = pl.BlockSpec((nseg, H_C, DK_C, DV_C), lambda i: (i, 0, 0, 0))
    return pl.pallas_call(
        functools.partial(_hgrn_body, c=c, nseg=nseg, layer=layer),
        grid=(t // r,),
        in_specs=[pl.BlockSpec((r, OD_QD), lambda i: (i, 0)), full((DEPTH, 512)), full((1, 512)), st_spec],
        out_specs=[pl.BlockSpec((r, 512), lambda i: (i, 0)), st_spec],
        out_shape=[jax.ShapeDtypeStruct((t, 512), F32), jax.ShapeDtypeStruct(s0.shape, F32)],
        compiler_params=_cparams(("arbitrary",)),
        name="hgrn_mixer",
    )(proj, lb_logits, on_g.reshape(1, 512), s0)


def _head_norm(x, g):
    rows, width = x.shape
    lo = lax.broadcasted_iota(jnp.int32, (rows, 128), 1) < HD_D
    outs = []
    for t in range(width // 128):
        xt = x[:, t * 128:(t + 1) * 128]
        sq = xt * xt
        ss_lo = jnp.sum(jnp.where(lo, sq, 0.0), axis=-1, keepdims=True)
        ss_hi = jnp.sum(jnp.where(lo, 0.0, sq), axis=-1, keepdims=True)
        scale = jnp.where(lo, lax.rsqrt(ss_lo * (1.0 / HD_D) + EPS), lax.rsqrt(ss_hi * (1.0 / HD_D) + EPS))
        outs.append(xt * scale)
    return jnp.concatenate(outs, axis=1) * g


def _rope_partial(x, cos, sin):
    width = x.shape[1]
    half = ROT_DIM_D // 2
    lane = lax.broadcasted_iota(jnp.int32, x.shape, 1)
    rot = jnp.where((lane % HD_D) < half, pltpu.roll(x, width - half, 1), pltpu.roll(x, half, 1))
    return x * cos + rot * sin


def _swa_block(q, k, v, kprev, vprev, cos, sin, qng_ref, kng_ref, sink_ref, allow_prev):
    rep = HQ_D // HKV_D
    qn = _rope_partial(_head_norm(q, qng_ref[...]), jnp.concatenate([cos] * (HQ_D // 2), axis=1),
                       jnp.concatenate([sin] * (HQ_D // 2), axis=1)).astype(BF16)
    kn = _rope_partial(_head_norm(k, kng_ref[...]), cos, sin)
    kk = jnp.concatenate([kprev, kn], axis=0)
    vv = jnp.concatenate([vprev, v], axis=0)
    lo = lax.broadcasted_iota(jnp.int32, kk.shape, 1) < HD_D
    kk_sw = pltpu.roll(kk, HD_D, 1)
    vv_sw = pltpu.roll(vv, HD_D, 1)
    kx = [[jnp.where(lo, kk, 0.0).astype(BF16), jnp.where(lo, 0.0, kk_sw).astype(BF16)],
          [jnp.where(lo, kk_sw, 0.0).astype(BF16), jnp.where(lo, 0.0, kk).astype(BF16)]]
    vx = [[jnp.where(lo, vv, 0.0).astype(BF16), jnp.where(lo, 0.0, vv_sw).astype(BF16)],
          [jnp.where(lo, vv_sw, 0.0).astype(BF16), jnp.where(lo, 0.0, vv).astype(BF16)]]
    qi = lax.broadcasted_iota(jnp.int32, (WINDOW, 2 * WINDOW), 0)
    ci = lax.broadcasted_iota(jnp.int32, (WINDOW, 2 * WINDOW), 1)
    cur = (ci >= WINDOW) & ((ci - WINDOW) <= qi)
    prev = (ci < WINDOW) & (ci > qi)
    mask = prev | cur if allow_prev is True else (prev & allow_prev) | cur
    tiles = []
    for t in range(HQ_D // 2):
        g = (2 * t) // rep
        qt = qn[:, t * 128:(t + 1) * 128]
        acc = None
        for half in range(2):
            s = lax.dot_general(qt, kx[g][half], (((1,), (1,)), ((), ())), preferred_element_type=F32) * (HD_D ** -0.5)
            s = jnp.where(mask, s, -jnp.inf)
            sink = sink_ref[2 * t + half:2 * t + half + 1, 0:1]
            m = jnp.maximum(jnp.max(s, axis=-1, keepdims=True), sink)
            p = jnp.exp(s - m)
            den = jnp.sum(p, axis=-1, keepdims=True) + jnp.exp(sink - m)
            part = jnp.dot(p.astype(BF16), vx[g][half], preferred_element_type=F32) / den
            acc = part if acc is None else acc + part
        tiles.append(acc)
    return jnp.concatenate(tiles, axis=1), kn


def _swa_sample_body(q_ref, k_ref, v_ref, kc_ref, vc_ref, cos_ref, sin_ref, qng_ref, kng_ref, sink_ref,
                     o_ref, kout_ref, vout_ref, qs_ref, ks_ref, *, nseq, l):
    rep = HQ_D // HKV_D
    cos = cos_ref[...]
    sin = sin_ref[...]
    qs_ref[...] = _rope_partial(_head_norm(q_ref[...], qng_ref[...]), jnp.concatenate([cos] * (HQ_D // 2), axis=1),
                                jnp.concatenate([sin] * (HQ_D // 2), axis=1))
    ks_ref[...] = _rope_partial(_head_norm(k_ref[...], kng_ref[...]), cos, sin)
    qi = lax.broadcasted_iota(jnp.int32, (rep * l, WINDOW), 0) % l
    c1 = lax.broadcasted_iota(jnp.int32, (rep * l, WINDOW), 1)
    mask1 = c1 > qi
    qi2 = lax.broadcasted_iota(jnp.int32, (rep * l, l), 0) % l
    c2 = lax.broadcasted_iota(jnp.int32, (rep * l, l), 1)
    mask2 = c2 <= qi2

    def seq(s, carry_val):
        rows = pl.ds(pl.multiple_of(s * l, l), l)
        q = qs_ref[rows, :]
        k = ks_ref[rows, :]
        v = v_ref[rows, :]
        kb = kc_ref[s]
        vb = vc_ref[s]
        outs = []
        for g in range(HKV_D):
            gs = slice(g * HD_D, (g + 1) * HD_D)
            qg = jnp.concatenate([q[:, (g * rep + r) * HD_D:(g * rep + r + 1) * HD_D] for r in range(rep)], axis=0)
            s1 = jnp.where(mask1, _dot_nt(qg, kb[:, gs]) * (HD_D ** -0.5), -jnp.inf)
            s2 = jnp.where(mask2, _dot_nt(qg, k[:, gs]) * (HD_D ** -0.5), -jnp.inf)
            sink = sink_ref[g]
            m = jnp.maximum(jnp.maximum(jnp.max(s1, axis=-1, keepdims=True), jnp.max(s2, axis=-1, keepdims=True)),
                            sink[:, 0:1])
            p1 = jnp.exp(s1 - m)
            p2 = jnp.exp(s2 - m)
            den = (jnp.sum(p1, axis=-1, keepdims=True) + jnp.sum(p2, axis=-1, keepdims=True)
                   + jnp.exp(sink[:, 0:1] - m))
            og = (_dot(p1, vb[:, gs]) + _dot(p2, v[:, gs])) / den
            outs.extend([og[r * l:(r + 1) * l, :] for r in range(rep)])
        o_ref[rows, :] = jnp.concatenate(outs, axis=1)
        kout_ref[s, 0:WINDOW - l, :] = kb[l:, :]
        kout_ref[s, WINDOW - l:, :] = k
        vout_ref[s, 0:WINDOW - l, :] = vb[l:, :]
        vout_ref[s, WINDOW - l:, :] = v
        return carry_val

    lax.fori_loop(0, nseq, seq, 0)


def _swa_gains(qn_g, kn_g):
    return (jnp.tile(qn_g, HQ_D).reshape(1, HQ_D * HD_D), jnp.tile(kn_g, HKV_D).reshape(1, HKV_D * HD_D))


def _swa_sample(proj, pos, kcache, vcache, qn_g, kn_g, sinks, *, nseq):
    t = proj.shape[0]
    l = pos.shape[0]
    r = nseq * l
    rep = HQ_D // HKV_D
    cos, sin = _rope_tables(pos, ROT_DIM_D, ROPE_THETA, HD_D, 2, row_reps=nseq)
    sink_rows = jnp.broadcast_to(jnp.repeat(sinks.reshape(HKV_D, rep), l, axis=1)[:, :, None], (HKV_D, rep * l, 128))
    qng, kng = _swa_gains(qn_g, kn_g)
    rowb = lambda w, j: pl.BlockSpec((r, w), lambda i, j=j: (i, j))
    full = lambda shape: pl.BlockSpec(shape, lambda i: (0,) * len(shape))
    cache = pl.BlockSpec((nseq, WINDOW, 128), lambda i: (i, 0, 0))
    return pl.pallas_call(
        functools.partial(_swa_sample_body, nseq=nseq, l=l),
        grid=(t // r,),
        in_specs=[rowb(512, OD_QD // 512), rowb(128, OD_KD // 128), rowb(128, OD_VD // 128), cache, cache,
                  full((r, 128)), full((r, 128)), full((1, 512)), full((1, 128)), full((HKV_D, rep * l, 128))],
        out_specs=[rowb(512, 0), cache, cache],
        out_shape=[jax.ShapeDtypeStruct((t, 512), F32), jax.ShapeDtypeStruct(kcache.shape, F32),
                   jax.ShapeDtypeStruct(vcache.shape, F32)],
        scratch_shapes=[pltpu.VMEM((r, 512), F32), pltpu.VMEM((r, 128), F32)],
        compiler_params=_cparams(("arbitrary",)),
        name="swa_sample",
    )(proj, proj, proj, kcache, vcache, cos, sin, qng, kng, sink_rows)


def _odd_layer_body(x_ref, ng_ref, win_ref, wout_ref, lbl_ref, ong_ref, cos_ref, sin_ref, qng_ref, kng_ref, sink_ref,
                    y_ref, sout_ref, kout_ref, vout_ref, st_ref, kprev_ref, vprev_ref, *, c, pb, layer):
    step = pl.program_id(1)

    @pl.when(step == 0)
    def _():
        st_ref[...] = jnp.zeros_like(st_ref)
        kprev_ref[...] = jnp.zeros_like(kprev_ref)
        vprev_ref[...] = jnp.zeros_like(vprev_ref)

    x = x_ref[...]
    h = _rms(x, ng_ref[...]).astype(BF16)
    lb = _hgrn_lower_bound(lbl_ref[...], layer)
    ngrp = x.shape[0] // pb
    project = lambda g: jnp.dot(h[g * pb:(g + 1) * pb, :], win_ref[...], preferred_element_type=F32)
    proj = project(0)
    kprev, vprev = kprev_ref[...], vprev_ref[...]
    for g in range(ngrp):
        nxt = project(g + 1) if g + 1 < ngrp else None
        for u in range(pb // WINDOW):
            r0 = g * pb + u * WINDOW
            rows = slice(r0, r0 + WINDOW)
            pj = proj[u * WINDOW:(u + 1) * WINDOW, :]
            o_c = []
            for j in range(WINDOW // c):
                o, s_new = _hgrn_segment(pj[j * c:(j + 1) * c, :], lb, ong_ref, [st_ref[hh] for hh in range(H_C)])
                for hh in range(H_C):
                    st_ref[hh] = s_new[hh]
                o_c.append(o)
            v = pj[:, OD_VD:OD_END]
            o_d, kn = _swa_block(pj[:, OD_QD:OD_KD], pj[:, OD_KD:OD_VD], v, kprev, vprev, cos_ref[rows, :],
                                 sin_ref[rows, :], qng_ref, kng_ref, sink_ref, True if r0 > 0 else step > 0)
            kprev, vprev = kn, v
            o = jnp.concatenate([jnp.concatenate(o_c, axis=0), o_d], axis=1).astype(BF16)
            y_ref[rows, :] = x[rows, :] + jnp.dot(o, wout_ref[...], preferred_element_type=F32)
        proj = nxt
    kprev_ref[...] = kprev
    vprev_ref[...] = vprev
    kout_ref[0] = kprev
    vout_ref[0] = vprev

    @pl.when(step == pl.num_programs(1) - 1)
    def _():
        sout_ref[0] = st_ref[...]


def _odd_layer(x, pos, ng, w_in, w_out, lb_logits, on_g, qn_g, kn_g, sinks, *, n_seq, r, pb, layer):
    t, d = x.shape
    seq_len = pos.shape[0]
    c = math.gcd(seq_len, HGRN_CHUNK)
    assert WINDOW % c == 0 and pb % WINDOW == 0 and r % pb == 0 and seq_len % r == 0
    steps = seq_len // r
    cos, sin = _rope_tables(pos, ROT_DIM_D, ROPE_THETA, HD_D, 2)
    qng, kng = _swa_gains(qn_g, kn_g)
    rowb = pl.BlockSpec((r, d), lambda b, i: (b * steps + i, 0))
    tab = pl.BlockSpec((r, 128), lambda b, i: (i, 0))
    st_spec = pl.BlockSpec((1, H_C, DK_C, DV_C), lambda b, i: (b, 0, 0, 0))
    kv_out = pl.BlockSpec((1, WINDOW, 128), lambda b, i: (b, 0, 0))
    return pl.pallas_call(
        functools.partial(_odd_layer_body, c=c, pb=pb, layer=layer),
        grid=(n_seq, steps),
        in_specs=[rowb, _resident((1, d)), _resident(w_in.shape), _resident(w_out.shape), _resident((DEPTH, 512)),
                  _resident((1, 512)), tab, tab, _resident((1, 512)), _resident((1, 128)), _resident((HQ_D, 128))],
        out_specs=[rowb, st_spec, kv_out, kv_out],
        out_shape=[jax.ShapeDtypeStruct((t, d), F32), jax.ShapeDtypeStruct((n_seq, H_C, DK_C, DV_C), F32),
                   jax.ShapeDtypeStruct((n_seq, WINDOW, 128), F32), jax.ShapeDtypeStruct((n_seq, WINDOW, 128), F32)],
        scratch_shapes=[pltpu.VMEM((H_C, DK_C, DV_C), F32), pltpu.VMEM((WINDOW, 128), F32),
                        pltpu.VMEM((WINDOW, 128), F32)],
        compiler_params=_cparams(("arbitrary", "arbitrary")),
        name="odd_layer",
    )(x, ng.reshape(1, d), w_in, w_out, lb_logits, on_g.reshape(1, 512), cos, sin, qng, kng,
      jnp.broadcast_to(sinks[:, None], (HQ_D, 128)))


def _mem_kv_body(m_ref, w_ref, g_ref, k_ref, v_ref):
    kv = jnp.dot(m_ref[...].astype(BF16), w_ref[...], preferred_element_type=F32)
    hw = H_X * HD_X
    for h in range(H_X):
        hs = slice(h * HD_X, (h + 1) * HD_X)
        k_ref[:, hs] = _rms(kv[:, hs], g_ref[...])
    v_ref[...] = kv[:, hw:]


def _mem_kv(mem, w, g):
    t = mem.shape[0]
    hw = H_X * HD_X
    return pl.pallas_call(
        _mem_kv_body,
        out_shape=[jax.ShapeDtypeStruct((t, hw), F32), jax.ShapeDtypeStruct((t, hw), F32)],
        compiler_params=pltpu.CompilerParams(vmem_limit_bytes=VMEM_LIMIT_BYTES),
        name="mem_kv",
    )(mem, w, g.reshape(1, HD_X))


def _mem_attend_body(x_ref, g_ref, wq_ref, qng_ref, mk_ref, mv_ref, wo_ref, y_ref, att_ref, *, nseq):
    x = x_ref[...]
    q = jnp.dot(_rms(x, g_ref[...]).astype(BF16), wq_ref[...], preferred_element_type=F32)
    rows_per = x.shape[0] // nseq
    for s in range(nseq):
        rs = slice(s * rows_per, (s + 1) * rows_per)
        for h in range(H_X):
            hs = slice(h * HD_X, (h + 1) * HD_X)
            qh = _rms(q[rs, hs], qng_ref[...])
            sc = _dot_nt(qh, mk_ref[s, :, hs]) * (HD_X ** -0.5)
            m = jnp.max(sc, axis=-1, keepdims=True)
            p = jnp.exp(sc - m)
            att_ref[rs, hs] = _dot(p, mv_ref[s, :, hs]) / jnp.sum(p, axis=-1, keepdims=True)
    y_ref[...] = x + jnp.dot(att_ref[...].astype(BF16), wo_ref[...], preferred_element_type=F32)


def _mem_attend(x, g, wq, qn_g, mk, mv, wo, *, tm, nseq, tiles_per_mem):
    t, d = x.shape
    hw = H_X * HD_X
    full = lambda shape: pl.BlockSpec(shape, lambda i: (0,) * len(shape))
    mem_spec = pl.BlockSpec((nseq, N_MEM, hw), lambda i: (i // tiles_per_mem, 0, 0))
    return pl.pallas_call(
        functools.partial(_mem_attend_body, nseq=nseq),
        grid=(t // tm,),
        in_specs=[pl.BlockSpec((tm, d), lambda i: (i, 0)), full((1, d)), full((d, hw)), full((1, HD_X)),
                  mem_spec, mem_spec, full((hw, d))],
        out_specs=pl.BlockSpec((tm, d), lambda i: (i, 0)),
        out_shape=jax.ShapeDtypeStruct((t, d), F32),
        scratch_shapes=[pltpu.VMEM((tm, hw), F32)],
        compiler_params=_cparams(("arbitrary",)),
        name="mem_attend",
    )(x, g.reshape(1, d), wq, qn_g.reshape(1, HD_X), mk, mv, wo)


def _mem_attend_cached_body(x_ref, g_ref, wq_ref, qng_ref, mk_ref, mv_ref, wo_ref, y_ref, att_ref, *, nseq):
    x = x_ref[...]
    q = jnp.dot(_rms(x, g_ref[...]).astype(BF16), wq_ref[...], preferred_element_type=F32)
    l = x.shape[0] // nseq
    rows_all = H_X * l
    cols = N_MEM * H_X
    own = (lax.broadcasted_iota(jnp.int32, (rows_all, cols), 1) % H_X
           == lax.broadcasted_iota(jnp.int32, (rows_all, cols), 0) // l)
    for s in range(nseq):
        rs = slice(s * l, (s + 1) * l)
        qs = jnp.concatenate([_rms(q[rs, h * HD_X:(h + 1) * HD_X], qng_ref[...]) for h in range(H_X)], axis=0)
        sc = jnp.where(own, _dot_nt(qs, mk_ref[0, s]) * (HD_X ** -0.5), -jnp.inf)
        m = jnp.max(sc, axis=-1, keepdims=True)
        p = jnp.exp(sc - m)
        o = _dot(p, mv_ref[0, s]) / jnp.sum(p, axis=-1, keepdims=True)
        for h in range(H_X):
            att_ref[rs, h * HD_X:(h + 1) * HD_X] = o[h * l:(h + 1) * l, :]
    y_ref[...] = x + jnp.dot(att_ref[...].astype(BF16), wo_ref[...], preferred_element_type=F32)


def _mem_attend_cached(x, g, wq, qn_g, mk, mv, wo, *, layer, nseq):
    t, d = x.shape
    hw = H_X * HD_X
    tm = nseq * (t // mk.shape[1])
    full = lambda shape: pl.BlockSpec(shape, lambda i: (0,) * len(shape))
    mem_spec = pl.BlockSpec((1, nseq, N_MEM * H_X, HD_X), lambda i: (layer, i, 0, 0))
    return pl.pallas_call(
        functools.partial(_mem_attend_cached_body, nseq=nseq),
        grid=(t // tm,),
        in_specs=[pl.BlockSpec((tm, d), lambda i: (i, 0)), full((1, d)), full((d, hw)), full((1, HD_X)),
                  mem_spec, mem_spec, full((hw, d))],
        out_specs=pl.BlockSpec((tm, d), lambda i: (i, 0)),
        out_shape=jax.ShapeDtypeStruct((t, d), F32),
        scratch_shapes=[pltpu.VMEM((tm, hw), F32)],
        compiler_params=_cparams(("arbitrary",)),
        name="mem_attend_cached",
    )(x, g.reshape(1, d), wq, qn_g.reshape(1, HD_X), mk, mv, wo)


def _ffn_body(*refs, tm, ffc, tiles_per_seq, per_row_state):
    if per_row_state:
        x_ref, g_ref, wg_ref, wu_ref, cw_ref, cb_ref, wd_ref, c0_ref, y_ref, gt_ref, tail_ref = refs
    else:
        x_ref, g_ref, wg_ref, wu_ref, cw_ref, cb_ref, wd_ref, y_ref, gt_ref, tail_ref = refs
    ff = wg_ref.shape[1]
    x = x_ref[...]
    h = _rms(x, g_ref[...]).astype(BF16)
    hist = CONV_W - 1
    if per_row_state:
        ns = tm // 8
        r_in = lax.broadcasted_iota(jnp.int32, (tm, hist * ns), 0)
        c_in = lax.broadcasted_iota(jnp.int32, (tm, hist * ns), 1)
        same = (r_in // 8) == (c_in // hist)
        sel1 = (same & (r_in % 8 == 0) & (c_in % hist == hist - 1)).astype(BF16)
        sel2 = (same & (r_in % 8 < hist) & (c_in % hist == r_in % 8)).astype(BF16)
        r_out = lax.broadcasted_iota(jnp.int32, (hist * ns, tm), 0)
        c_out = lax.broadcasted_iota(jnp.int32, (hist * ns, tm), 1)
        sel_out = (((c_out // 8) == (r_out // hist)) & (c_out % 8 == 8 - hist + r_out % hist)).astype(BF16)

        def select(sel, val):
            return sum(jnp.dot(sel, piece, preferred_element_type=F32) for piece in _split3(val))
    else:
        @pl.when(pl.program_id(0) % tiles_per_seq == 0)
        def _():
            tail_ref[...] = jnp.zeros_like(tail_ref)

    row8 = lax.broadcasted_iota(jnp.int32, (8, ffc), 0)
    acc = None

    def down(acts):
        a = acts[0][0] if len(acts) == 1 else jnp.concatenate([p[0] for p in acts], axis=1)
        part = jnp.dot(a, wd_ref[acts[0][1].start:acts[-1][1].stop, :], preferred_element_type=F32)
        return part if acc is None else acc + part

    filling, ready = [], None
    for c0 in range(0, ff, ffc):
        cs = slice(c0, c0 + ffc)
        gate = jnp.dot(h, wg_ref[:, cs], preferred_element_type=F32)
        up = jnp.dot(h, wu_ref[:, cs], preferred_element_type=F32)
        if ready is not None:
            acc = down(ready)
            ready = None
        r1 = pltpu.roll(gate, 1, 0)
        r2 = pltpu.roll(gate, 2, 0)
        if per_row_state:
            pos = lax.broadcasted_iota(jnp.int32, gate.shape, 0) % 8
            c0c = c0_ref[:, cs]
            g1 = jnp.where(pos == 0, select(sel1, c0c), r1)
            g2 = jnp.where(pos < hist, select(sel2, c0c), r2)
            gt_ref[:, cs] = select(sel_out, gate)
        else:
            prev = tail_ref[:, cs]
            top1 = jnp.where(row8 == 0, prev[7:8, :], r1[0:8, :])
            top2 = jnp.where(row8 == 0, prev[6:7, :], jnp.where(row8 == 1, prev[7:8, :], r2[0:8, :]))
            g1 = jnp.concatenate([top1, r1[8:, :]], axis=0)
            g2 = jnp.concatenate([top2, r2[8:, :]], axis=0)
            tail_ref[:, cs] = gate[tm - 8:, :]
            gt_ref[:, cs] = gate[tm - 8:, :]
        conv = cb_ref[:, cs] + cw_ref[0:1, cs] * g2 + cw_ref[1:2, cs] * g1 + cw_ref[2:3, cs] * gate
        filling.append(((_gelu(conv) * up).astype(BF16), cs))
        if len(filling) == FFN_DOWN_GROUP:
            filling, ready = [], filling
    for acts in (ready, filling):
        if acts:
            acc = down(acts)
    y_ref[...] = x + acc


def _ffn(x, g, wg, wu, cw, cb, wd, *, tm, ffc, tiles_per_seq, c0=None):
    t, d = x.shape
    ff = wg.shape[1]
    per_row_state = c0 is not None
    tail_n = tm // 8 * (CONV_W - 1) if per_row_state else 8
    in_specs = [pl.BlockSpec((tm, d), lambda i: (i, 0)), _resident((1, d)), _resident((d, ff)), _resident((d, ff)),
                _resident((8, ff)), _resident((1, ff)), _resident((ff, d))]
    args = [x, g.reshape(1, d), wg, wu, jnp.pad(cw, ((0, 8 - CONV_W), (0, 0))), cb.reshape(1, ff), wd]
    if per_row_state:
        in_specs += [pl.BlockSpec((tail_n, ff), lambda i: (i, 0))]
        args += [c0]
    return pl.pallas_call(
        functools.partial(_ffn_body, tm=tm, ffc=ffc, tiles_per_seq=tiles_per_seq, per_row_state=per_row_state),
        grid=(t // tm,),
        in_specs=in_specs,
        out_specs=[pl.BlockSpec((tm, d), lambda i: (i, 0)), pl.BlockSpec((tail_n, ff), lambda i: (i, 0))],
        out_shape=[jax.ShapeDtypeStruct((t, d), F32), jax.ShapeDtypeStruct((t // tm * tail_n, ff), F32)],
        scratch_shapes=[pltpu.VMEM((8, ff), F32)],
        compiler_params=_cparams(("arbitrary",)),
        name="ffn",
    )(*args)


def _run_prompt(x, pos, n_seq, mem, w):
    seq_len = pos.shape[0]
    tm = 512
    hw = H_X * HD_X
    out = {k: [] for k in ("ret", "hgrn", "swa_k", "swa_v", "mem_k", "mem_v", "conv")}
    for l in range(DEPTH):
        j = l // 2
        if l % 2 == 0:
            x, s_new = _even_layer(x, pos, w["norm_mix_g"][l], w["ev_w_in"][j], w["ev_w_out"][j], w["ret_gn_g"][j],
                                   w["mlp_norm_g"][j], w["mlp_w_s"][j], w["mlp_b_s"][j], n_seq=n_seq, r=1024, pb=256)
            out["ret"].append(s_new)
        else:
            x, s_new, k_new, v_new = _odd_layer(x, pos, w["norm_mix_g"][l], w["od_w_in"][j], w["od_w_out"][j],
                                                w["hgrn_lb_logits"], w["hgrn_onorm_g"][j], w["swa_qnorm_g"][j],
                                                w["swa_knorm_g"][j], w["swa_sinks"][j], n_seq=n_seq, r=512, pb=128,
                                                layer=l)
            out["hgrn"].append(s_new)
            out["swa_k"].append(k_new.reshape(n_seq, WINDOW, HKV_D, HD_D))
            out["swa_v"].append(v_new.reshape(n_seq, WINDOW, HKV_D, HD_D))
        mk, mv = _mem_kv(mem, w["mem_w_kv"][l], w["mem_knorm_g"][l])
        out["mem_k"].append(mk.reshape(n_seq, N_MEM, H_X, HD_X))
        out["mem_v"].append(mv.reshape(n_seq, N_MEM, H_X, HD_X))
        x = _mem_attend(x, w["norm_mem_g"][l], w["mem_w_q"][l], w["mem_qnorm_g"][l], mk.reshape(n_seq, N_MEM, hw),
                        mv.reshape(n_seq, N_MEM, hw), w["mem_w_o"][l], tm=tm, nseq=1, tiles_per_mem=seq_len // tm)
        x, gt = _ffn(x, w["norm_ffn_g"][l], w["ffn_w_gate"][l], w["ffn_w_up"][l], w["ffn_conv_w"][l],
                     w["ffn_conv_b"][l], w["ffn_w_down"][l], tm=tm, ffc=256, tiles_per_seq=seq_len // tm)
        out["conv"].append(gt.reshape(n_seq, seq_len // tm, 8, D_FF)[:, -1, 8 - (CONV_W - 1):, :])
    return x, {name: jnp.stack(rows) for name, rows in out.items()}


def _run_sample(x, pos, n_seq, st, w):
    seq_len = pos.shape[0]
    assert seq_len == 8
    t = x.shape[0]
    out = {k: [] for k in ("ret", "chunk_v", "hgrn", "swa_k", "swa_v", "conv")}
    for l in range(DEPTH):
        j = l // 2
        if l % 2 == 0:
            proj = _norm_proj(x, w["norm_mix_g"][l], w["ev_w_in"][j], t)
            o, s_new, v_rows = _even_mixer(proj, pos, st["ret"][j], w["ret_gn_g"][j], w["mlp_norm_g"][j],
                                           w["mlp_w_s"][j], w["mlp_b_s"][j], nseg=16)
            out["chunk_v"].append(v_rows.reshape(n_seq, seq_len, G_B, DG_B))
            out["ret"].append(s_new)
            x = _out_proj([o], w["ev_w_out"][j], x, t)
        else:
            proj = _norm_proj(x, w["norm_mix_g"][l], w["od_w_in"][j], t)
            o_c, s_new = _hgrn_mixer(proj, w["hgrn_lb_logits"], w["hgrn_onorm_g"][j], st["hgrn"][j], c=seq_len,
                                     nseg=16, layer=l)
            o_d, k_new, v_new = _swa_sample(proj, pos, st["swa_k"][j].reshape(n_seq, WINDOW, HKV_D * HD_D),
                                            st["swa_v"][j].reshape(n_seq, WINDOW, HKV_D * HD_D),
                                            w["swa_qnorm_g"][j], w["swa_knorm_g"][j], w["swa_sinks"][j], nseq=16)
            out["hgrn"].append(s_new)
            out["swa_k"].append(k_new.reshape(n_seq, WINDOW, HKV_D, HD_D))
            out["swa_v"].append(v_new.reshape(n_seq, WINDOW, HKV_D, HD_D))
            x = _out_proj([o_c, o_d], w["od_w_out"][j], x, t)
        x = _mem_attend_cached(x, w["norm_mem_g"][l], w["mem_w_q"][l], w["mem_qnorm_g"][l],
                               st["mem_k"].reshape(DEPTH, n_seq, N_MEM * H_X, HD_X),
                               st["mem_v"].reshape(DEPTH, n_seq, N_MEM * H_X, HD_X), w["mem_w_o"][l], layer=l, nseq=8)
        c0 = st["conv"][l].reshape(n_seq * (CONV_W - 1), D_FF)
        x, gt = _ffn(x, w["norm_ffn_g"][l], w["ffn_w_gate"][l], w["ffn_w_up"][l], w["ffn_conv_w"][l],
                     w["ffn_conv_b"][l], w["ffn_w_down"][l], tm=256, ffc=256, tiles_per_seq=1, c0=c0)
        out["conv"].append(gt.reshape(n_seq, CONV_W - 1, D_FF))
    return x, {name: jnp.stack(rows) for name, rows in out.items()}


def kernel(x_prompt, x_sample, state_ret, state_hgrn, cache_swa_k, cache_swa_v, cache_mem_k, cache_mem_v,
           state_ffn_conv, mem_prompt, norm_mix_g, norm_mem_g, norm_ffn_g, ev_w_in, ev_w_out, ret_gn_g,
           mlp_norm_g, mlp_w_s, mlp_b_s, od_w_in, od_w_out, hgrn_lb_logits, hgrn_onorm_g, swa_qnorm_g,
           swa_knorm_g, swa_sinks, mem_w_q, mem_w_kv, mem_qnorm_g, mem_knorm_g, mem_w_o, ffn_w_gate,
           ffn_w_up, ffn_conv_w, ffn_conv_b, ffn_w_down):
    bf = lambda a: a.astype(BF16)
    w = dict(norm_mix_g=norm_mix_g, norm_mem_g=norm_mem_g, norm_ffn_g=norm_ffn_g, ev_w_in=bf(ev_w_in),
             ev_w_out=bf(ev_w_out), ret_gn_g=ret_gn_g, mlp_norm_g=mlp_norm_g, mlp_w_s=mlp_w_s, mlp_b_s=mlp_b_s,
             od_w_in=bf(od_w_in), od_w_out=bf(od_w_out), hgrn_lb_logits=hgrn_lb_logits, hgrn_onorm_g=hgrn_onorm_g,
             swa_qnorm_g=swa_qnorm_g, swa_knorm_g=swa_knorm_g, swa_sinks=swa_sinks, mem_w_q=bf(mem_w_q),
             mem_w_kv=bf(mem_w_kv), mem_qnorm_g=mem_qnorm_g, mem_knorm_g=mem_knorm_g, mem_w_o=bf(mem_w_o),
             ffn_w_gate=bf(ffn_w_gate), ffn_w_up=bf(ffn_w_up), ffn_conv_w=ffn_conv_w, ffn_conv_b=ffn_conv_b,
             ffn_w_down=bf(ffn_w_down))
    b, seq, d = x_prompt.shape
    db, dseq, _ = x_sample.shape
    pos_prompt = np.arange(seq)
    pos_sample = PAST_LEN + np.arange(dseq)
    y_p, ns_p = _run_prompt(x_prompt.reshape(b * seq, d), pos_prompt, b, mem_prompt.reshape(b * N_MEM, d), w)
    st = dict(ret=state_ret, hgrn=state_hgrn, swa_k=cache_swa_k, swa_v=cache_swa_v, mem_k=cache_mem_k,
              mem_v=cache_mem_v, conv=state_ffn_conv)
    y_s, ns_s = _run_sample(x_sample.reshape(db * dseq, d), pos_sample, db, st, w)
    return (y_p.reshape(b, seq, d), y_s.reshape(db, dseq, d), ns_p["ret"], ns_p["hgrn"], ns_p["swa_k"], ns_p["swa_v"],
            ns_p["mem_k"], ns_p["mem_v"], ns_p["conv"], ns_s["ret"], ns_s["chunk_v"], ns_s["hgrn"], ns_s["swa_k"],
            ns_s["swa_v"], ns_s["conv"])
```

```python
import functools
import math

import jax
import jax.numpy as jnp
import numpy as np
from jax import lax
from jax.experimental import pallas as pl
from jax.experimental.pallas import tpu as pltpu

F32 = jnp.float32
BF16 = jnp.bfloat16

D_MODEL = 1024
DEPTH = 2
PAST_LEN = 16384
H_A, DV_A, DK_A = 4, 128, 64
RET_CHUNK = 128
RET_THETA = 10000.0
G_B, DG_B = 4, 128
MLP_CHUNK = 128
H_C, DK_C, DV_C = 4, 128, 128
HGRN_CHUNK = 64
HD_D, HQ_D, HKV_D = 64, 8, 2
WINDOW = 128
ROPE_THETA = 500000.0
ROT_DIM_D = HD_D // 4
N_MEM, H_X, HD_X = 256, 4, 128
D_FF = 2816
CONV_W = 3
FFN_DOWN_GROUP = 4
EPS = 1e-6
SQRT_HALF = float(np.sqrt(0.5))
EV_Q, EV_K, EV_V, EV_G, EV_U, EV_VB, EV_END = 0, 256, 512, 1024, 1536, 2048, 2560
OD_Q, OD_F, OD_I, OD_G, OD_QD, OD_KD, OD_VD, OD_END = 0, 512, 1024, 1536, 2048, 2560, 2688, 2816

VMEM_LIMIT_BYTES = 56 * 1024 * 1024


def _cparams(sem):
    return pltpu.CompilerParams(dimension_semantics=sem, vmem_limit_bytes=VMEM_LIMIT_BYTES)


def _resident(shape):
    return pl.BlockSpec(shape, lambda *_: (0,) * len(shape), pipeline_mode=pl.Buffered(1))


def _rms(x, g):
    return x * lax.rsqrt(jnp.mean(x * x, axis=-1, keepdims=True) + EPS) * g


def _ln(x, g):
    mu = jnp.mean(x, axis=-1, keepdims=True)
    xc = x - mu
    return xc * lax.rsqrt(jnp.mean(xc * xc, axis=-1, keepdims=True) + EPS) * g


def _sigmoid(x):
    return 1.0 / (1.0 + jnp.exp(-x))


def _silu(x):
    return x * _sigmoid(x)


def _gelu(x):
    return 0.5 * x * (1.0 + lax.erf(x * SQRT_HALF))


def _dot(a, b):
    return jnp.dot(a.astype(BF16), b.astype(BF16), preferred_element_type=F32)


def _dot_nt(a, b):
    return lax.dot_general(a.astype(BF16), b.astype(BF16), (((1,), (1,)), ((), ())), preferred_element_type=F32)


def _dot_tn(a, b):
    return lax.dot_general(a.astype(BF16), b.astype(BF16), (((0,), (0,)), ((), ())), preferred_element_type=F32)


def _split3(x):
    hi = x.astype(BF16)
    r1 = x - hi.astype(F32)
    mid = r1.astype(BF16)
    lo = (r1 - mid.astype(F32)).astype(BF16)
    return hi, mid, lo


def _norm_proj_body(x_ref, g_ref, w_ref, o_ref):
    h = _rms(x_ref[...], g_ref[...]).astype(BF16)
    n = o_ref.shape[1]
    for n0 in range(0, n, 512):
        n1 = min(n0 + 512, n)
        o_ref[:, n0:n1] = jnp.dot(h, w_ref[:, n0:n1], preferred_element_type=F32)


def _norm_proj(x, g, w, tm):
    t, d = x.shape
    n = w.shape[1]
    return pl.pallas_call(
        _norm_proj_body,
        grid=(t // tm,),
        in_specs=[pl.BlockSpec((tm, d), lambda i: (i, 0)),
                  pl.BlockSpec((1, d), lambda i: (0, 0)),
                  pl.BlockSpec((d, n), lambda i: (0, 0))],
        out_specs=pl.BlockSpec((tm, n), lambda i: (i, 0)),
        out_shape=jax.ShapeDtypeStruct((t, n), F32),
        compiler_params=_cparams(("arbitrary",)),
        name="norm_proj",
    )(x, g.reshape(1, d), w)


def _out_proj_body(*refs):
    *o_refs, w_ref, x_ref, y_ref = refs
    acc = x_ref[...]
    k0 = 0
    for o_ref in o_refs:
        k1 = k0 + o_ref.shape[1]
        acc = acc + jnp.dot(o_ref[...].astype(BF16), w_ref[k0:k1, :], preferred_element_type=F32)
        k0 = k1
    y_ref[...] = acc


def _out_proj(os, w, x, tm):
    t, d = x.shape
    return pl.pallas_call(
        _out_proj_body,
        grid=(t // tm,),
        in_specs=[pl.BlockSpec((tm, o.shape[1]), lambda i: (i, 0)) for o in os]
        + [pl.BlockSpec(w.shape, lambda i: (0, 0)), pl.BlockSpec((tm, d), lambda i: (i, 0))],
        out_specs=pl.BlockSpec((tm, d), lambda i: (i, 0)),
        out_shape=jax.ShapeDtypeStruct((t, d), F32),
        compiler_params=_cparams(("arbitrary",)),
        name="out_proj",
    )(*os, w, x)


def _retention_consts(c, nseq=1):
    lg = np.log1p(-np.exp2(-5.0 - np.arange(H_A, dtype=np.float64)))
    idx = np.arange(c, dtype=np.float64)
    rel = idx[:, None] - idx[None, :]
    dmask = np.where(rel >= 0, np.exp(rel[None] * lg[:, None, None]), 0.0)
    q_dec = np.exp((idx + 1.0)[None, :] * lg[:, None])
    k_dec = np.exp((c - 1.0 - idx)[None, :] * lg[:, None])
    c_dec = np.exp(c * lg)
    qdec = np.tile(np.repeat(q_dec.T, DK_A, axis=1), (nseq, 1))
    kdec = np.tile(np.repeat(k_dec.T, DK_A, axis=1), (nseq, 1))
    dmask = np.stack([np.kron(np.eye(nseq), dmask[h]) for h in range(H_A)])
    cdec = np.broadcast_to(c_dec[:, None, None], (H_A, 8, DV_A))
    return tuple(jnp.asarray(a, F32) for a in (dmask, qdec, kdec, cdec))


def _rope_tables(pos, rot_dim, theta, head_dim, reps, row_reps=1):
    half = rot_dim // 2
    inv = theta ** (-np.arange(half, dtype=np.float64) / half)
    ang = pos.astype(np.float64)[:, None] * inv[None, :]
    cos = np.cos(ang)
    sin = np.sin(ang)
    l = pos.shape[0]
    pad = head_dim - rot_dim
    cos_h = np.concatenate([cos, cos, np.ones((l, pad))], axis=1)
    sin_h = np.concatenate([-sin, sin, np.zeros((l, pad))], axis=1)
    return (jnp.asarray(np.tile(cos_h, (row_reps, reps)), F32), jnp.asarray(np.tile(sin_h, (row_reps, reps)), F32))


def _even_consts(c, gn_g, mn_g, w_s, b_s, nseq=1):
    dmask, qdec, kdec, cdec = _retention_consts(c, nseq)
    bs_full = jnp.tile(jnp.repeat(b_s[:, :c].T, DG_B, axis=1), (nseq, 1))
    arrays = [dmask, qdec, kdec, cdec, gn_g.reshape(1, H_A * DV_A), mn_g.reshape(1, G_B * DG_B),
              jnp.tile(w_s[:, :c, :c], (1, nseq, nseq)), bs_full]
    return arrays, [a.shape for a in arrays]


def _retention_state_batched(qd, kd, v, s0_ref, cdec_ref, nseq):
    n = qd.shape[0]
    own = (lax.broadcasted_iota(jnp.int32, (n, nseq * 128), 1) // 128
           == lax.broadcasted_iota(jnp.int32, (n, nseq * 128), 0) // (n // nseq))
    place = lambda x: jnp.where(own, jnp.concatenate([x] * nseq, axis=1), 0.0)
    inter, s_new = [None] * H_A, [None] * H_A
    for p in range(H_A // 2):
        h0, h1 = 2 * p, 2 * p + 1
        lanes = slice(p * 128, (p + 1) * 128)
        s_a, s_b = s0_ref[:, h0], s0_ref[:, h1]
        z = jnp.zeros_like(s_a)
        slab = jnp.concatenate([jnp.concatenate([s_a, z], axis=2), jnp.concatenate([z, s_b], axis=2)], axis=1)
        res = _dot(place(qd[:, lanes]), slab.reshape(nseq * 2 * DK_A, 2 * DV_A))
        inter[h0], inter[h1] = res[:, :DV_A], res[:, DV_A:]
        upd = _dot_tn(place(kd[:, lanes]), v[:, 2 * p * DV_A:2 * (p + 1) * DV_A]).reshape(nseq, 2 * DK_A, 2 * DV_A)
        s_new[h0] = cdec_ref[h0, 0:1, :] * s_a + upd[:, :DK_A, :DV_A]
        s_new[h1] = cdec_ref[h1, 0:1, :] * s_b + upd[:, DK_A:, DV_A:]
    return inter, s_new


def _even_segment(pj, cos, sin, consts, s_old, nseq=1):
    dmask_ref, qdec_ref, kdec_ref, cdec_ref, gng_ref, mng_ref, ws_ref, bs_ref = consts
    c = pj.shape[0]
    width = H_A * DK_A
    lane = lax.broadcasted_iota(jnp.int32, (c, width), 1)
    first_half = (lane % DK_A) < (DK_A // 2)
    row = lax.broadcasted_iota(jnp.int32, (c, c), 0)
    col = lax.broadcasted_iota(jnp.int32, (c, c), 1)
    tril = (col <= row) & (row // (c // nseq) == col // (c // nseq))

    def rope(x):
        rot = jnp.where(first_half, pltpu.roll(x, width - DK_A // 2, 1), pltpu.roll(x, DK_A // 2, 1))
        return x * cos + rot * sin

    q = rope(pj[:, EV_Q:EV_K])
    k = rope(pj[:, EV_K:EV_V]) * (DK_A ** -0.5)
    v = pj[:, EV_V:EV_G]
    ga = pj[:, EV_G:EV_U]
    qd = q * qdec_ref[...]
    kd = k * kdec_ref[...]
    ks = [slice(h * DK_A, (h + 1) * DK_A) for h in range(H_A)]
    vs = [slice(h * DV_A, (h + 1) * DV_A) for h in range(H_A)]
    qk = [_dot_nt(q[:, ks[h]], k[:, ks[h]]) for h in range(H_A)]
    if nseq == 1:
        inter = [_dot(qd[:, ks[h]], s_old[h]) for h in range(H_A)]
        s_new = [cdec_ref[h, 0:1, :] * s_old[h] + _dot_tn(kd[:, ks[h]], v[:, vs[h]]) for h in range(H_A)]
    else:
        inter, s_new = _retention_state_batched(qd, kd, v, s_old, cdec_ref, nseq)
    vg = _gelu(pj[:, EV_VB:EV_END])
    vrows = [_ln(vg[:, gs], mng_ref[:, gs]) for gs in vs]
    mixed = [_dot(jnp.where(tril, ws_ref[g], 0.0), vrows[g]) for g in range(G_B)]
    yield s_new
    outs = []
    for h in range(H_A):
        o = _dot(qk[h] * dmask_ref[h], v[:, vs[h]]) + inter[h]
        outs.append(_silu(ga[:, vs[h]]) * _ln(o, gng_ref[:, vs[h]]))
    u = _gelu(pj[:, EV_U:EV_VB])
    for g, gs in enumerate(vs):
        outs.append(u[:, gs] * (mixed[g] + bs_ref[:, gs]))
    yield jnp.concatenate(outs, axis=1), s_new, jnp.concatenate(vrows, axis=1)


def _even_body(pj_ref, cos_ref, sin_ref, dmask_ref, qdec_ref, kdec_ref, cdec_ref, gng_ref, mng_ref, ws_ref, bs_ref,
               s0_ref, o_ref, sout_ref, vrows_ref, *, c, nseg):
    consts = (dmask_ref, qdec_ref, kdec_ref, cdec_ref, gng_ref, mng_ref, ws_ref, bs_ref)
    (o, s_new, vrows), = _interleave(_even_segment(pj_ref[...], cos_ref[...], sin_ref[...], consts, s0_ref, nseq=nseg))
    for h in range(H_A):
        sout_ref[:, h] = s_new[h]
    o_ref[...] = o
    vrows_ref[...] = vrows


def _even_mixer(proj, pos, s0, gn_g, mn_g, w_s, b_s, *, nseg):
    t = proj.shape[0]
    c = pos.shape[0]
    r = nseg * c
    consts, shapes = _even_consts(c, gn_g, mn_g, w_s, b_s, nseq=nseg)
    cos, sin = _rope_tables(pos, DK_A, RET_THETA, DK_A, H_A, row_reps=nseg)
    full = lambda shape: pl.BlockSpec(shape, lambda i: (0,) * len(shape))
    st_spec = pl.BlockSpec((nseg, H_A, DK_A, DV_A), lambda i: (i, 0, 0, 0))
    return pl.pallas_call(
        functools.partial(_even_body, c=c, nseg=nseg),
        grid=(t // r,),
        in_specs=[pl.BlockSpec((r, EV_END), lambda i: (i, 0)), full((r, 256)), full((r, 256))]
        + [full(s) for s in shapes] + [st_spec],
        out_specs=[pl.BlockSpec((r, 1024), lambda i: (i, 0)), st_spec, pl.BlockSpec((r, 512), lambda i: (i, 0))],
        out_shape=[jax.ShapeDtypeStruct((t, 1024), F32), jax.ShapeDtypeStruct(s0.shape, F32),
                   jax.ShapeDtypeStruct((t, 512), F32)],
        compiler_params=_cparams(("arbitrary",)),
        name="even_mixer",
    )(proj, cos, sin, *consts, s0)


def _even_layer_body(x_ref, ng_ref, win_ref, wout_ref, cos_ref, sin_ref, dmask_ref, qdec_ref, kdec_ref, cdec_ref,
                     gng_ref, mng_ref, ws_ref, bs_ref, y_ref, sout_ref, st_ref, *, c, pb):
    step = pl.program_id(1)

    @pl.when(step == 0)
    def _():
        st_ref[...] = jnp.zeros_like(st_ref)

    consts = (dmask_ref, qdec_ref, kdec_ref, cdec_ref, gng_ref, mng_ref, ws_ref, bs_ref)
    x = x_ref[...]
    h = _rms(x, ng_ref[...]).astype(BF16)
    ngrp = x.shape[0] // pb
    project = lambda g: jnp.dot(h[g * pb:(g + 1) * pb, :], win_ref[...], preferred_element_type=F32)
    proj = project(0)
    state = [st_ref[hh] for hh in range(H_A)]
    pending = None

    def finish(rows, seg):
        o, _, _ = next(seg)
        y_ref[rows, :] = x[rows, :] + jnp.dot(o.astype(BF16), wout_ref[...], preferred_element_type=F32)

    for g in range(ngrp):
        nxt = project(g + 1) if g + 1 < ngrp else None
        for j in range(pb // c):
            rows = slice(g * pb + j * c, g * pb + (j + 1) * c)
            seg = _even_segment(proj[j * c:(j + 1) * c, :], cos_ref[rows, :], sin_ref[rows, :], consts, state)
            state = next(seg)
            if pending is not None:
                finish(*pending)
            pending = (rows, seg)
        proj = nxt
    finish(*pending)
    for hh in range(H_A):
        st_ref[hh] = state[hh]

    @pl.when(step == pl.num_programs(1) - 1)
    def _():
        sout_ref[0] = st_ref[...]


def _even_layer(x, pos, ng, w_in, w_out, gn_g, mn_g, w_s, b_s, *, n_seq, r, pb):
    t, d = x.shape
    seq_len = pos.shape[0]
    c = math.gcd(seq_len, RET_CHUNK)
    assert c == min(MLP_CHUNK, seq_len) and pb % c == 0 and r % pb == 0 and seq_len % r == 0
    steps = seq_len // r
    consts, shapes = _even_consts(c, gn_g, mn_g, w_s, b_s)
    cos, sin = _rope_tables(pos, DK_A, RET_THETA, DK_A, H_A)
    rowb = pl.BlockSpec((r, d), lambda b, i: (b * steps + i, 0))
    tab = pl.BlockSpec((r, 256), lambda b, i: (i, 0))
    st_spec = pl.BlockSpec((1, H_A, DK_A, DV_A), lambda b, i: (b, 0, 0, 0))
    return pl.pallas_call(
        functools.partial(_even_layer_body, c=c, pb=pb),
        grid=(n_seq, steps),
        in_specs=[rowb, _resident((1, d)), _resident(w_in.shape), _resident(w_out.shape), tab, tab]
        + [_resident(s) for s in shapes],
        out_specs=[rowb, st_spec],
        out_shape=[jax.ShapeDtypeStruct((t, d), F32), jax.ShapeDtypeStruct((n_seq, H_A, DK_A, DV_A), F32)],
        scratch_shapes=[pltpu.VMEM((H_A, DK_A, DV_A), F32)],
        compiler_params=_cparams(("arbitrary", "arbitrary")),
        name="even_layer",
    )(x, ng.reshape(1, d), w_in, w_out, cos, sin, *consts)


def _hgrn_lower_bound(lbl, layer):
    e = jnp.exp(lbl - jnp.max(lbl, axis=0, keepdims=True))
    sm = e / jnp.sum(e, axis=0, keepdims=True)
    acc = sm[0:1, :]
    first = acc
    for i in range(1, layer + 1):
        acc = acc + sm[i:i + 1, :]
    return acc - first


def _block_row(x, blk, j):
    n, lanes = x.shape
    if blk % 16 == 0:
        return jnp.concatenate(
            [jnp.broadcast_to(x[i * blk + j:i * blk + j + 1, :], (blk, lanes)) for i in range(n // blk)], axis=0)
    x3 = x.reshape(n // 8, 8, lanes)
    pick = lambda jj: jnp.broadcast_to(x3[:, jj:jj + 1, :], x3.shape).reshape(n, lanes)
    if blk == 8:
        return pick(j)
    assert blk == 4
    upper = (lax.broadcasted_iota(jnp.int32, (n, 1), 0) % 8) >= 4
    return jnp.where(upper, pick(4 + j), pick(j))


def _hgrn_gates(pj, lb):
    f = lb + (1.0 - lb) * _sigmoid(pj[:, OD_F:OD_I])
    return f, jnp.log(f), 1.0 - f, _silu(pj[:, OD_Q:OD_F]), pj[:, OD_I:OD_G], pj[:, OD_G:OD_QD]


def _chunk_cumsum(g, c):
    n = g.shape[0]
    row = lax.broadcasted_iota(jnp.int32, (n, n), 0)
    col = lax.broadcasted_iota(jnp.int32, (n, n), 1)
    tri = ((col <= row) & (row // c == col // c)).astype(BF16)
    return sum(jnp.dot(tri, piece, preferred_element_type=F32) for piece in _split3(g))


def _hgrn_intra(q, k, f, b, c):
    return _hgrn_assemble(_hgrn_level_products(q, k, f, b, c))


def _hgrn_level_products(q, k, f, b, c):
    n = q.shape[0]
    rloc = lax.broadcasted_iota(jnp.int32, (n, 1), 0)
    diag = jnp.sum(q * k, axis=-1, keepdims=True)
    prods = []
    blk = 2
    while blk <= c:
        second = (rloc % blk) >= blk // 2
        if blk == 2:
            e = f
            ke = k
        else:
            dq = b - _block_row(b, blk, blk // 2 - 1)
            e = jnp.exp(jnp.where(second, dq, -dq))
            ke = k * e
        prods.append((blk, _dot_nt(jnp.where(second, q * e, 0.0), jnp.where(second, 0.0, ke))))
        blk *= 2
    return diag, prods


def _hgrn_assemble(level_products):
    diag, prods = level_products
    n = diag.shape[0]
    row = lax.broadcasted_iota(jnp.int32, (n, n), 0)
    col = lax.broadcasted_iota(jnp.int32, (n, n), 1)
    a = jnp.where(row == col, diag, 0.0)
    for blk, p in prods:
        a = a + jnp.where((row // blk) == (col // blk), p, 0.0)
    return a


def _interleave(*gens):
    results = [None] * len(gens)
    live = list(enumerate(gens))
    while live:
        still = []
        for i, gen in live:
            try:
                results[i] = next(gen)
                still.append((i, gen))
            except StopIteration:
                pass
        live = still
    return results


def _hgrn_tile(pj, lb, ong_ref, st_ref, c):
    n = pj.shape[0]
    heads = [slice(h * DK_C, (h + 1) * DK_C) for h in range(H_C)]
    chunks = [slice(j * c, (j + 1) * c) for j in range(n // c)]
    f, g, k, q, v, gc = _hgrn_gates(pj, lb)
    b = _chunk_cumsum(g, c)
    qe = q * jnp.exp(b)
    kd = k * jnp.exp(_block_row(b, c, c - 1) - b)
    levels = [_hgrn_level_products(q[:, hs], k[:, hs], f[:, hs], b[:, hs], c) for hs in heads]
    kv = [[_dot_tn(kd[rs, hs], v[rs, hs]) for rs in chunks] for hs in heads]
    yield None
    o_intra = [_dot(_hgrn_assemble(levels[h]), v[:, hs]) for h, hs in enumerate(heads)]
    yield None
    outs = []
    for h, hs in enumerate(heads):
        s = st_ref[h]
        parts = []
        for j, rs in enumerate(chunks):
            parts.append(o_intra[h][rs, :] + _dot(qe[rs, hs], s))
            bl = b[rs.stop - 1:rs.stop, hs]
            decay = jnp.broadcast_to(jnp.exp(bl), (DK_C, DK_C)).T
            s = decay * s + kv[h][j]
        st_ref[h] = s
        outs.append(_rms(jnp.concatenate(parts, axis=0), ong_ref[:, hs]) * _silu(gc[:, hs]))
    yield jnp.concatenate(outs, axis=1)


def _hgrn_body(pj_ref, lbl_ref, ong_ref, s0_ref, oc_ref, sout_ref, *, c, layer):
    pj = pj_ref[...]
    n = pj.shape[0]
    ns = n // c
    lb = _hgrn_lower_bound(lbl_ref[...], layer)
    f, g, k, q, v, gc = _hgrn_gates(pj, lb)
    b = _chunk_cumsum(g, c)
    blast = _block_row(b, c, c - 1)
    qe = q * jnp.exp(b)
    kd = k * jnp.exp(blast - b)
    own_cols = (lax.broadcasted_iota(jnp.int32, (n, ns * DK_C), 1) // DK_C
                == lax.broadcasted_iota(jnp.int32, (n, ns * DK_C), 0) // c)
    place = lambda x: jnp.where(own_cols, jnp.concatenate([x] * ns, axis=1), 0.0)
    seq_cols = (lax.broadcasted_iota(jnp.int32, (ns, ns * DK_C), 1) // DK_C
                == lax.broadcasted_iota(jnp.int32, (ns, ns * DK_C), 0))
    ones = jnp.ones((ns, DV_C), BF16)
    last_row = (lax.broadcasted_iota(jnp.int32, (ns, n), 1)
                == lax.broadcasted_iota(jnp.int32, (ns, n), 0) * c + (c - 1)).astype(BF16)
    b_last = sum(jnp.dot(last_row, piece, preferred_element_type=F32) for piece in _split3(b))
    for h in range(H_C):
        hs = slice(h * DK_C, (h + 1) * DK_C)
        vh = v[:, hs]
        s = s0_ref[:, h].reshape(ns * DK_C, DV_C)
        o = _dot(_hgrn_intra(q[:, hs], k[:, hs], f[:, hs], b[:, hs], c), vh) + _dot(place(qe[:, hs]), s)
        e_last = jnp.exp(b_last[:, hs])
        e_placed = jnp.where(seq_cols, jnp.concatenate([e_last] * ns, axis=1), 0.0)
        decay = sum(lax.dot_general(piece, ones, (((0,), (0,)), ((), ())), preferred_element_type=F32)
                    for piece in _split3(e_placed))
        s_new = decay * s + _dot_tn(place(kd[:, hs]), vh)
        sout_ref[:, h] = s_new.reshape(ns, DK_C, DV_C)
        oc_ref[:, hs] = _rms(o, ong_ref[:, hs]) * _silu(gc[:, hs])


def _hgrn_mixer(proj, lb_logits, on_g, s0, *, c, nseq, layer):
    t = proj.shape[0]
    r = nseq * c
    full = lambda shape: pl.BlockSpec(shape, lambda i: (0,) * len(shape))
    st_spec = pl.BlockSpec((nseq, H_C, DK_C, DV_C), lambda i: (i, 0, 0, 0))
    return pl.pallas_call(
        functools.partial(_hgrn_body, c=c, layer=layer),
        grid=(t // r,),
        in_specs=[pl.BlockSpec((r, OD_QD), lambda i: (i, 0)), full((DEPTH, 512)), full((1, 512)), st_spec],
        out_specs=[pl.BlockSpec((r, 512), lambda i: (i, 0)), st_spec],
        out_shape=[jax.ShapeDtypeStruct((t, 512), F32), jax.ShapeDtypeStruct(s0.shape, F32)],
        compiler_params=_cparams(("arbitrary",)),
        name="hgrn_mixer",
    )(proj, lb_logits, on_g.reshape(1, 512), s0)


def _head_norm(x, g):
    rows, width = x.shape
    lo = lax.broadcasted_iota(jnp.int32, (rows, 128), 1) < HD_D
    outs = []
    for t in range(width // 128):
        xt = x[:, t * 128:(t + 1) * 128]
        sq = xt * xt
        ss_lo = jnp.sum(jnp.where(lo, sq, 0.0), axis=-1, keepdims=True)
        ss_hi = jnp.sum(jnp.where(lo, 0.0, sq), axis=-1, keepdims=True)
        scale = jnp.where(lo, lax.rsqrt(ss_lo * (1.0 / HD_D) + EPS), lax.rsqrt(ss_hi * (1.0 / HD_D) + EPS))
        outs.append(xt * scale)
    return jnp.concatenate(outs, axis=1) * g


def _rope_partial(x, cos, sin):
    width = x.shape[1]
    half = ROT_DIM_D // 2
    lane = lax.broadcasted_iota(jnp.int32, x.shape, 1)
    rot = jnp.where((lane % HD_D) < half, pltpu.roll(x, width - half, 1), pltpu.roll(x, half, 1))
    return x * cos + rot * sin


def _swa_prep(q, k, cos, sin, qng_ref, kng_ref):
    qn = _rope_partial(_head_norm(q, qng_ref[...]), jnp.concatenate([cos] * (HQ_D // 2), axis=1),
                       jnp.concatenate([sin] * (HQ_D // 2), axis=1)).astype(BF16)
    return qn, _rope_partial(_head_norm(k, kng_ref[...]), cos, sin)


def _swa_place(x):
    lo = lax.broadcasted_iota(jnp.int32, x.shape, 1) < HD_D
    sw = pltpu.roll(x, HD_D, 1)
    return [[jnp.where(lo, x, 0.0).astype(BF16), jnp.where(lo, 0.0, sw).astype(BF16)],
            [jnp.where(lo, sw, 0.0).astype(BF16), jnp.where(lo, 0.0, x).astype(BF16)]]


def _swa_block(qn, k_prev, k_cur, v_prev, v_cur, sink_ref, allow_prev):
    rep = HQ_D // HKV_D
    lo = lax.broadcasted_iota(jnp.int32, (WINDOW, 128), 1) < HD_D
    qi = lax.broadcasted_iota(jnp.int32, (WINDOW, 2 * WINDOW), 0)
    ci = lax.broadcasted_iota(jnp.int32, (WINDOW, 2 * WINDOW), 1)
    cur = (ci >= WINDOW) & ((ci - WINDOW) <= qi)
    prev = (ci < WINDOW) & (ci > qi)
    mask = prev | cur if allow_prev is True else (prev & allow_prev) | cur
    keys = lambda g, half: jnp.concatenate([k_prev[g][half], k_cur[g][half]], axis=0)
    ones = [jnp.where(lo, 1.0, 0.0).astype(BF16), jnp.where(lo, 0.0, 1.0).astype(BF16)]
    vals = lambda g, half: jnp.concatenate(
        [jnp.concatenate([v_prev[g][half], v_cur[g][half]], axis=0), jnp.concatenate([ones[half]] * 2, axis=0)], axis=1)
    scores = [lax.dot_general(qn[:, (h // 2) * 128:(h // 2 + 1) * 128], keys(h // rep, h % 2),
                              (((1,), (1,)), ((), ())), preferred_element_type=F32) for h in range(HQ_D)]
    yield None
    pv, sink_term = [], []
    for h in range(HQ_D):
        s = jnp.where(mask, scores[h] * (HD_D ** -0.5), -jnp.inf)
        sink = sink_ref[h:h + 1, 0:1]
        m = jnp.maximum(jnp.max(s, axis=-1, keepdims=True), sink)
        pv.append(jnp.dot(jnp.exp(s - m).astype(BF16), vals(h // rep, h % 2), preferred_element_type=F32))
        sink_term.append(jnp.exp(sink - m))
    yield None
    tiles = []
    for t in range(HQ_D // 2):
        both = pv[2 * t] + pv[2 * t + 1]
        den = both[:, 128:] + jnp.where(lo, sink_term[2 * t], sink_term[2 * t + 1])
        tiles.append(both[:, :128] / den)
    yield jnp.concatenate(tiles, axis=1)


def _swa_sample_chain(q2, k_cache, k_new, v_cache, v_new, sinks2, half, l):
    rows = q2.shape[0]
    ns = rows // (2 * l)
    lo = lax.broadcasted_iota(jnp.int32, (1, 128), 1) < HD_D
    ones = jnp.where(lo, 1.0, 0.0) if half == 0 else jnp.where(lo, 0.0, 1.0)
    ext = lambda v: jnp.concatenate([v, jnp.broadcast_to(ones, v.shape).astype(BF16)], axis=1)
    nt = lambda a, b: lax.dot_general(a, b, (((1,), (1,)), ((), ())), preferred_element_type=F32)
    s1 = nt(q2, k_cache)
    s2 = nt(q2, k_new)
    yield None
    r1 = lax.broadcasted_iota(jnp.int32, s1.shape, 0)
    c1 = lax.broadcasted_iota(jnp.int32, s1.shape, 1)
    ok1 = ((r1 % (ns * l)) // l == c1 // WINDOW) & (c1 % WINDOW > r1 % l)
    r2 = lax.broadcasted_iota(jnp.int32, s2.shape, 0)
    c2 = lax.broadcasted_iota(jnp.int32, s2.shape, 1)
    ok2 = ((r2 % (ns * l)) // l == c2 // l) & (c2 % l <= r2 % l)
    s1 = jnp.where(ok1, s1 * (HD_D ** -0.5), -jnp.inf)
    s2 = jnp.where(ok2, s2 * (HD_D ** -0.5), -jnp.inf)
    sink = jnp.where(lax.broadcasted_iota(jnp.int32, (rows, 1), 0) < ns * l, sinks2[0], sinks2[1])
    m = jnp.maximum(jnp.maximum(jnp.max(s1, axis=-1, keepdims=True), jnp.max(s2, axis=-1, keepdims=True)), sink)
    res = (jnp.dot(jnp.exp(s1 - m).astype(BF16), ext(v_cache), preferred_element_type=F32)
           + jnp.dot(jnp.exp(s2 - m).astype(BF16), ext(v_new), preferred_element_type=F32))
    yield res, jnp.exp(sink - m)


def _swa_sample_body(q_ref, k_ref, v_ref, kc_ref, vc_ref, cos_ref, sin_ref, qng_ref, kng_ref, sink_ref,
                     o_ref, kout_ref, vout_ref, *, nseq, l, group):
    rep = HQ_D // HKV_D
    qn, kn = _swa_prep(q_ref[...], k_ref[...], cos_ref[...], sin_ref[...], qng_ref, kng_ref)
    v = v_ref[...]
    lo = lax.broadcasted_iota(jnp.int32, (group * l, 128), 1) < HD_D
    chains, where = [], []
    for gi in range(nseq // group):
        rs = slice(gi * group * l, (gi + 1) * group * l)
        ss = slice(gi * group, (gi + 1) * group)
        kc = _swa_place(kc_ref[ss].reshape(group * WINDOW, 128))
        vc = _swa_place(vc_ref[ss].reshape(group * WINDOW, 128))
        k_new, v_new = _swa_place(kn[rs, :]), _swa_place(v[rs, :])
        for g in range(HKV_D):
            tiles = [(g * rep) // 2, (g * rep) // 2 + 1]
            q2 = jnp.concatenate([qn[rs, t * 128:(t + 1) * 128] for t in tiles], axis=0)
            for half in range(2):
                sinks2 = [sink_ref[2 * t + half:2 * t + half + 1, 0:1] for t in tiles]
                chains.append(_swa_sample_chain(q2, kc[g][half], k_new[g][half], vc[g][half], v_new[g][half],
                                                sinks2, half, l))
                where.append((rs, tiles, half))
        kout_ref[ss, 0:WINDOW - l, :] = kc_ref[ss, l:WINDOW, :]
        kout_ref[ss, WINDOW - l:WINDOW, :] = kn[rs, :].reshape(group, l, 128)
        vout_ref[ss, 0:WINDOW - l, :] = vc_ref[ss, l:WINDOW, :]
        vout_ref[ss, WINDOW - l:WINDOW, :] = v[rs, :].reshape(group, l, 128)
    results = _interleave(*chains)
    for i in range(0, len(chains), 2):
        (rs, tiles, _), (res0, st0), (res1, st1) = where[i], results[i], results[i + 1]
        both = res0 + res1
        for j, t in enumerate(tiles):
            js = slice(j * group * l, (j + 1) * group * l)
            den = both[js, 128:] + jnp.where(lo, st0[js, :], st1[js, :])
            o_ref[rs, t * 128:(t + 1) * 128] = both[js, :128] / den


def _swa_gains(qn_g, kn_g):
    return (jnp.tile(qn_g, HQ_D).reshape(1, HQ_D * HD_D), jnp.tile(kn_g, HKV_D).reshape(1, HKV_D * HD_D))


def _swa_sample(proj, pos, kcache, vcache, qn_g, kn_g, sinks, *, nseq, group):
    t = proj.shape[0]
    l = pos.shape[0]
    r = nseq * l
    cos, sin = _rope_tables(pos, ROT_DIM_D, ROPE_THETA, HD_D, 2, row_reps=nseq)
    qng, kng = _swa_gains(qn_g, kn_g)
    rowb = lambda w, j: pl.BlockSpec((r, w), lambda i, j=j: (i, j))
    full = lambda shape: pl.BlockSpec(shape, lambda i: (0,) * len(shape))
    cache = pl.BlockSpec((nseq, WINDOW, 128), lambda i: (i, 0, 0))
    return pl.pallas_call(
        functools.partial(_swa_sample_body, nseq=nseq, l=l, group=group),
        grid=(t // r,),
        in_specs=[rowb(512, OD_QD // 512), rowb(128, OD_KD // 128), rowb(128, OD_VD // 128), cache, cache,
                  full((r, 128)), full((r, 128)), full((1, 512)), full((1, 128)), full((HQ_D, 128))],
        out_specs=[rowb(512, 0), cache, cache],
        out_shape=[jax.ShapeDtypeStruct((t, 512), F32), jax.ShapeDtypeStruct(kcache.shape, F32),
                   jax.ShapeDtypeStruct(vcache.shape, F32)],
        compiler_params=_cparams(("arbitrary",)),
        name="swa_sample",
    )(proj, proj, proj, kcache, vcache, cos, sin, qng, kng, jnp.broadcast_to(sinks[:, None], (HQ_D, 128)))


def _odd_layer_body(x_ref, ng_ref, win_ref, wout_ref, lbl_ref, ong_ref, cos_ref, sin_ref, qng_ref, kng_ref, sink_ref,
                    y_ref, sout_ref, kout_ref, vout_ref, st_ref, kprev_ref, vprev_ref, *, c, pb, layer):
    step = pl.program_id(1)

    @pl.when(step == 0)
    def _():
        st_ref[...] = jnp.zeros_like(st_ref)
        kprev_ref[...] = jnp.zeros_like(kprev_ref)
        vprev_ref[...] = jnp.zeros_like(vprev_ref)

    x = x_ref[...]
    h = _rms(x, ng_ref[...]).astype(BF16)
    lb = _hgrn_lower_bound(lbl_ref[...], layer)
    ngrp = x.shape[0] // pb
    project = lambda g: jnp.dot(h[g * pb:(g + 1) * pb, :], win_ref[...], preferred_element_type=F32)
    proj = project(0)
    k_last, v_last = kprev_ref[...], vprev_ref[...]
    k_prev, v_prev = _swa_place(k_last), _swa_place(v_last)
    for g in range(ngrp):
        nxt = project(g + 1) if g + 1 < ngrp else None
        grows = slice(g * pb, (g + 1) * pb)
        qn, kn = _swa_prep(proj[:, OD_QD:OD_KD], proj[:, OD_KD:OD_VD], cos_ref[grows, :], sin_ref[grows, :],
                           qng_ref, kng_ref)
        staged = []
        for u in range(pb // WINDOW):
            us = slice(u * WINDOW, (u + 1) * WINDOW)
            k_last, v_last = kn[us, :], proj[us, OD_VD:OD_END]
            k_cur, v_cur = _swa_place(k_last), _swa_place(v_last)
            staged.append(_hgrn_tile(proj[us, :], lb, ong_ref, st_ref, c))
            staged.append(_swa_block(qn[us, :], k_prev, k_cur, v_prev, v_cur, sink_ref,
                                     True if g + u > 0 else step > 0))
            k_prev, v_prev = k_cur, v_cur
        outs = _interleave(*staged)
        o = jnp.concatenate([jnp.concatenate(outs[0::2], axis=0), jnp.concatenate(outs[1::2], axis=0)], axis=1)
        y_ref[grows, :] = x[grows, :] + jnp.dot(o.astype(BF16), wout_ref[...], preferred_element_type=F32)
        proj = nxt
    kprev_ref[...] = k_last
    vprev_ref[...] = v_last
    kout_ref[0] = k_last
    vout_ref[0] = v_last

    @pl.when(step == pl.num_programs(1) - 1)
    def _():
        sout_ref[0] = st_ref[...]


def _odd_layer(x, pos, ng, w_in, w_out, lb_logits, on_g, qn_g, kn_g, sinks, *, n_seq, r, pb, layer):
    t, d = x.shape
    seq_len = pos.shape[0]
    c = math.gcd(seq_len, HGRN_CHUNK)
    assert WINDOW % c == 0 and pb % WINDOW == 0 and r % pb == 0 and seq_len % r == 0
    steps = seq_len // r
    cos, sin = _rope_tables(pos, ROT_DIM_D, ROPE_THETA, HD_D, 2)
    qng, kng = _swa_gains(qn_g, kn_g)
    rowb = pl.BlockSpec((r, d), lambda b, i: (b * steps + i, 0))
    tab = pl.BlockSpec((r, 128), lambda b, i: (i, 0))
    st_spec = pl.BlockSpec((1, H_C, DK_C, DV_C), lambda b, i: (b, 0, 0, 0))
    kv_out = pl.BlockSpec((1, WINDOW, 128), lambda b, i: (b, 0, 0))
    return pl.pallas_call(
        functools.partial(_odd_layer_body, c=c, pb=pb, layer=layer),
        grid=(n_seq, steps),
        in_specs=[rowb, _resident((1, d)), _resident(w_in.shape), _resident(w_out.shape), _resident((DEPTH, 512)),
                  _resident((1, 512)), tab, tab, _resident((1, 512)), _resident((1, 128)), _resident((HQ_D, 128))],
        out_specs=[rowb, st_spec, kv_out, kv_out],
        out_shape=[jax.ShapeDtypeStruct((t, d), F32), jax.ShapeDtypeStruct((n_seq, H_C, DK_C, DV_C), F32),
                   jax.ShapeDtypeStruct((n_seq, WINDOW, 128), F32), jax.ShapeDtypeStruct((n_seq, WINDOW, 128), F32)],
        scratch_shapes=[pltpu.VMEM((H_C, DK_C, DV_C), F32), pltpu.VMEM((WINDOW, 128), F32),
                        pltpu.VMEM((WINDOW, 128), F32)],
        compiler_params=_cparams(("arbitrary", "arbitrary")),
        name="odd_layer",
    )(x, ng.reshape(1, d), w_in, w_out, lb_logits, on_g.reshape(1, 512), cos, sin, qng, kng,
      jnp.broadcast_to(sinks[:, None], (HQ_D, 128)))


def _mem_kv_body(m_ref, w_ref, g_ref, k_ref, v_ref):
    kv = jnp.dot(m_ref[...].astype(BF16), w_ref[...], preferred_element_type=F32)
    hw = H_X * HD_X
    for h in range(H_X):
        hs = slice(h * HD_X, (h + 1) * HD_X)
        k_ref[:, hs] = _rms(kv[:, hs], g_ref[...])
    v_ref[...] = kv[:, hw:]


def _mem_kv(mem, w, g):
    t = mem.shape[0]
    hw = H_X * HD_X
    return pl.pallas_call(
        _mem_kv_body,
        out_shape=[jax.ShapeDtypeStruct((t, hw), F32), jax.ShapeDtypeStruct((t, hw), F32)],
        compiler_params=pltpu.CompilerParams(vmem_limit_bytes=VMEM_LIMIT_BYTES),
        name="mem_kv",
    )(mem, w, g.reshape(1, HD_X))


def _mem_attend_body(x_ref, g_ref, wq_ref, qng_ref, mk_ref, mv_ref, wo_ref, y_ref, att_ref, *, nseq):
    x = x_ref[...]
    q = jnp.dot(_rms(x, g_ref[...]).astype(BF16), wq_ref[...], preferred_element_type=F32)
    rows_per = x.shape[0] // nseq
    heads = [slice(h * HD_X, (h + 1) * HD_X) for h in range(H_X)]
    for s in range(nseq):
        rs = slice(s * rows_per, (s + 1) * rows_per)
        scores = [_dot_nt(_rms(q[rs, hs], qng_ref[...]), mk_ref[s, :, hs]) for hs in heads]
        for hs, sc in zip(heads, scores):
            sc = sc * (HD_X ** -0.5)
            p = jnp.exp(sc - jnp.max(sc, axis=-1, keepdims=True))
            att_ref[rs, hs] = _dot(p, mv_ref[s, :, hs]) / jnp.sum(p, axis=-1, keepdims=True)
    y_ref[...] = x + jnp.dot(att_ref[...].astype(BF16), wo_ref[...], preferred_element_type=F32)


def _mem_attend(x, g, wq, qn_g, mk, mv, wo, *, tm, nseq, tiles_per_mem):
    t, d = x.shape
    hw = H_X * HD_X
    full = lambda shape: pl.BlockSpec(shape, lambda i: (0,) * len(shape))
    mem_spec = pl.BlockSpec((nseq, N_MEM, hw), lambda i: (i // tiles_per_mem, 0, 0))
    return pl.pallas_call(
        functools.partial(_mem_attend_body, nseq=nseq),
        grid=(t // tm,),
        in_specs=[pl.BlockSpec((tm, d), lambda i: (i, 0)), full((1, d)), full((d, hw)), full((1, HD_X)),
                  mem_spec, mem_spec, full((hw, d))],
        out_specs=pl.BlockSpec((tm, d), lambda i: (i, 0)),
        out_shape=jax.ShapeDtypeStruct((t, d), F32),
        scratch_shapes=[pltpu.VMEM((tm, hw), F32)],
        compiler_params=_cparams(("arbitrary",)),
        name="mem_attend",
    )(x, g.reshape(1, d), wq, qn_g.reshape(1, HD_X), mk, mv, wo)


def _mem_attend_cached_body(x_ref, g_ref, wq_ref, qng_ref, mk_ref, mv_ref, wo_ref, y_ref, att_ref, *, nseq):
    x = x_ref[...]
    q = jnp.dot(_rms(x, g_ref[...]).astype(BF16), wq_ref[...], preferred_element_type=F32)
    l = x.shape[0] // nseq
    rows_all = H_X * l
    cols = N_MEM * H_X
    own = (lax.broadcasted_iota(jnp.int32, (rows_all, cols), 1) % H_X
           == lax.broadcasted_iota(jnp.int32, (rows_all, cols), 0) // l)
    for s in range(nseq):
        rs = slice(s * l, (s + 1) * l)
        qs = jnp.concatenate([_rms(q[rs, h * HD_X:(h + 1) * HD_X], qng_ref[...]) for h in range(H_X)], axis=0)
        sc = jnp.where(own, _dot_nt(qs, mk_ref[0, s]) * (HD_X ** -0.5), -jnp.inf)
        m = jnp.max(sc, axis=-1, keepdims=True)
        p = jnp.exp(sc - m)
        o = _dot(p, mv_ref[0, s]) / jnp.sum(p, axis=-1, keepdims=True)
        for h in range(H_X):
            att_ref[rs, h * HD_X:(h + 1) * HD_X] = o[h * l:(h + 1) * l, :]
    y_ref[...] = x + jnp.dot(att_ref[...].astype(BF16), wo_ref[...], preferred_element_type=F32)


def _mem_attend_cached(x, g, wq, qn_g, mk, mv, wo, *, layer, nseq):
    t, d = x.shape
    hw = H_X * HD_X
    tm = nseq * (t // mk.shape[1])
    full = lambda shape: pl.BlockSpec(shape, lambda i: (0,) * len(shape))
    mem_spec = pl.BlockSpec((1, nseq, N_MEM * H_X, HD_X), lambda i: (layer, i, 0, 0))
    return pl.pallas_call(
        functools.partial(_mem_attend_cached_body, nseq=nseq),
        grid=(t // tm,),
        in_specs=[pl.BlockSpec((tm, d), lambda i: (i, 0)), full((1, d)), full((d, hw)), full((1, HD_X)),
                  mem_spec, mem_spec, full((hw, d))],
        out_specs=pl.BlockSpec((tm, d), lambda i: (i, 0)),
        out_shape=jax.ShapeDtypeStruct((t, d), F32),
        scratch_shapes=[pltpu.VMEM((tm, hw), F32)],
        compiler_params=_cparams(("arbitrary",)),
        name="mem_attend_cached",
    )(x, g.reshape(1, d), wq, qn_g.reshape(1, HD_X), mk, mv, wo)


def _ffn_body(*refs, tm, ffc, tiles_per_seq, per_row_state):
    if per_row_state:
        x_ref, g_ref, wg_ref, wu_ref, cw_ref, cb_ref, wd_ref, c0_ref, y_ref, gt_ref, tail_ref = refs
    else:
        x_ref, g_ref, wg_ref, wu_ref, cw_ref, cb_ref, wd_ref, y_ref, gt_ref, tail_ref = refs
    ff = wg_ref.shape[1]
    x = x_ref[...]
    h = _rms(x, g_ref[...]).astype(BF16)
    hist = CONV_W - 1
    if per_row_state:
        ns = tm // 8
        r_in = lax.broadcasted_iota(jnp.int32, (tm, hist * ns), 0)
        c_in = lax.broadcasted_iota(jnp.int32, (tm, hist * ns), 1)
        same = (r_in // 8) == (c_in // hist)
        sel1 = (same & (r_in % 8 == 0) & (c_in % hist == hist - 1)).astype(BF16)
        sel2 = (same & (r_in % 8 < hist) & (c_in % hist == r_in % 8)).astype(BF16)
        r_out = lax.broadcasted_iota(jnp.int32, (hist * ns, tm), 0)
        c_out = lax.broadcasted_iota(jnp.int32, (hist * ns, tm), 1)
        sel_out = (((c_out // 8) == (r_out // hist)) & (c_out % 8 == 8 - hist + r_out % hist)).astype(BF16)

        def select(sel, val):
            return sum(jnp.dot(sel, piece, preferred_element_type=F32) for piece in _split3(val))
    else:
        @pl.when(pl.program_id(0) % tiles_per_seq == 0)
        def _():
            tail_ref[...] = jnp.zeros_like(tail_ref)

    row8 = lax.broadcasted_iota(jnp.int32, (8, ffc), 0)
    acc = None

    def down(acts):
        a = acts[0][0] if len(acts) == 1 else jnp.concatenate([p[0] for p in acts], axis=1)
        part = jnp.dot(a, wd_ref[acts[0][1].start:acts[-1][1].stop, :], preferred_element_type=F32)
        return part if acc is None else acc + part

    filling, ready = [], None
    for c0 in range(0, ff, ffc):
        cs = slice(c0, c0 + ffc)
        gate = jnp.dot(h, wg_ref[:, cs], preferred_element_type=F32)
        up = jnp.dot(h, wu_ref[:, cs], preferred_element_type=F32)
        if ready is not None:
            acc = down(ready)
            ready = None
        r1 = pltpu.roll(gate, 1, 0)
        r2 = pltpu.roll(gate, 2, 0)
        if per_row_state:
            pos = lax.broadcasted_iota(jnp.int32, gate.shape, 0) % 8
            c0c = c0_ref[:, cs]
            g1 = jnp.where(pos == 0, select(sel1, c0c), r1)
            g2 = jnp.where(pos < hist, select(sel2, c0c), r2)
            gt_ref[:, cs] = select(sel_out, gate)
        else:
            prev = tail_ref[:, cs]
            top1 = jnp.where(row8 == 0, prev[7:8, :], r1[0:8, :])
            top2 = jnp.where(row8 == 0, prev[6:7, :], jnp.where(row8 == 1, prev[7:8, :], r2[0:8, :]))
            g1 = jnp.concatenate([top1, r1[8:, :]], axis=0)
            g2 = jnp.concatenate([top2, r2[8:, :]], axis=0)
            tail_ref[:, cs] = gate[tm - 8:, :]
            gt_ref[:, cs] = gate[tm - 8:, :]
        conv = cb_ref[:, cs] + cw_ref[0:1, cs] * g2 + cw_ref[1:2, cs] * g1 + cw_ref[2:3, cs] * gate
        filling.append(((_gelu(conv) * up).astype(BF16), cs))
        if len(filling) == FFN_DOWN_GROUP:
            filling, ready = [], filling
    for acts in (ready, filling):
        if acts:
            acc = down(acts)
    y_ref[...] = x + acc


def _ffn(x, g, wg, wu, cw, cb, wd, *, tm, ffc, tiles_per_seq, c0=None):
    t, d = x.shape
    ff = wg.shape[1]
    per_row_state = c0 is not None
    tail_n = tm // 8 * (CONV_W - 1) if per_row_state else 8
    in_specs = [pl.BlockSpec((tm, d), lambda i: (i, 0)), _resident((1, d)), _resident((d, ff)), _resident((d, ff)),
                _resident((8, ff)), _resident((1, ff)), _resident((ff, d))]
    args = [x, g.reshape(1, d), wg, wu, jnp.pad(cw, ((0, 8 - CONV_W), (0, 0))), cb.reshape(1, ff), wd]
    if per_row_state:
        in_specs += [pl.BlockSpec((tail_n, ff), lambda i: (i, 0))]
        args += [c0]
    return pl.pallas_call(
        functools.partial(_ffn_body, tm=tm, ffc=ffc, tiles_per_seq=tiles_per_seq, per_row_state=per_row_state),
        grid=(t // tm,),
        in_specs=in_specs,
        out_specs=[pl.BlockSpec((tm, d), lambda i: (i, 0)), pl.BlockSpec((tail_n, ff), lambda i: (i, 0))],
        out_shape=[jax.ShapeDtypeStruct((t, d), F32), jax.ShapeDtypeStruct((t // tm * tail_n, ff), F32)],
        scratch_shapes=[pltpu.VMEM((8, ff), F32)],
        compiler_params=_cparams(("arbitrary",)),
        name="ffn",
    )(*args)


def _run_prompt(x, pos, n_seq, mem, w):
    seq_len = pos.shape[0]
    tm = 512
    hw = H_X * HD_X
    out = {k: [] for k in ("ret", "hgrn", "swa_k", "swa_v", "mem_k", "mem_v", "conv")}
    for l in range(DEPTH):
        j = l // 2
        if l % 2 == 0:
            x, s_new = _even_layer(x, pos, w["norm_mix_g"][l], w["ev_w_in"][j], w["ev_w_out"][j], w["ret_gn_g"][j],
                                   w["mlp_norm_g"][j], w["mlp_w_s"][j], w["mlp_b_s"][j], n_seq=n_seq, r=1024, pb=256)
            out["ret"].append(s_new)
        else:
            x, s_new, k_new, v_new = _odd_layer(x, pos, w["norm_mix_g"][l], w["od_w_in"][j], w["od_w_out"][j],
                                                w["hgrn_lb_logits"], w["hgrn_onorm_g"][j], w["swa_qnorm_g"][j],
                                                w["swa_knorm_g"][j], w["swa_sinks"][j], n_seq=n_seq, r=512, pb=256,
                                                layer=l)
            out["hgrn"].append(s_new)
            out["swa_k"].append(k_new.reshape(n_seq, WINDOW, HKV_D, HD_D))
            out["swa_v"].append(v_new.reshape(n_seq, WINDOW, HKV_D, HD_D))
        mk, mv = _mem_kv(mem, w["mem_w_kv"][l], w["mem_knorm_g"][l])
        out["mem_k"].append(mk.reshape(n_seq, N_MEM, H_X, HD_X))
        out["mem_v"].append(mv.reshape(n_seq, N_MEM, H_X, HD_X))
        x = _mem_attend(x, w["norm_mem_g"][l], w["mem_w_q"][l], w["mem_qnorm_g"][l], mk.reshape(n_seq, N_MEM, hw),
                        mv.reshape(n_seq, N_MEM, hw), w["mem_w_o"][l], tm=tm, nseq=1, tiles_per_mem=seq_len // tm)
        x, gt = _ffn(x, w["norm_ffn_g"][l], w["ffn_w_gate"][l], w["ffn_w_up"][l], w["ffn_conv_w"][l],
                     w["ffn_conv_b"][l], w["ffn_w_down"][l], tm=tm, ffc=256, tiles_per_seq=seq_len // tm)
        out["conv"].append(gt.reshape(n_seq, seq_len // tm, 8, D_FF)[:, -1, 8 - (CONV_W - 1):, :])
    return x, {name: jnp.stack(rows) for name, rows in out.items()}


def _run_sample(x, pos, n_seq, st, w):
    seq_len = pos.shape[0]
    assert seq_len == 8
    t = x.shape[0]
    out = {k: [] for k in ("ret", "chunk_v", "hgrn", "swa_k", "swa_v", "conv")}
    for l in range(DEPTH):
        j = l // 2
        if l % 2 == 0:
            proj = _norm_proj(x, w["norm_mix_g"][l], w["ev_w_in"][j], t)
            o, s_new, v_rows = _even_mixer(proj, pos, st["ret"][j], w["ret_gn_g"][j], w["mlp_norm_g"][j],
                                           w["mlp_w_s"][j], w["mlp_b_s"][j], nseg=16)
            out["chunk_v"].append(v_rows.reshape(n_seq, seq_len, G_B, DG_B))
            out["ret"].append(s_new)
            x = _out_proj([o], w["ev_w_out"][j], x, t)
        else:
            proj = _norm_proj(x, w["norm_mix_g"][l], w["od_w_in"][j], t)
            o_c, s_new = _hgrn_mixer(proj, w["hgrn_lb_logits"], w["hgrn_onorm_g"][j], st["hgrn"][j], c=seq_len,
                                     nseq=16, layer=l)
            o_d, k_new, v_new = _swa_sample(proj, pos, st["swa_k"][j].reshape(n_seq, WINDOW, HKV_D * HD_D),
                                            st["swa_v"][j].reshape(n_seq, WINDOW, HKV_D * HD_D),
                                            w["swa_qnorm_g"][j], w["swa_knorm_g"][j], w["swa_sinks"][j], nseq=16,
                                            group=8)
            out["hgrn"].append(s_new)
            out["swa_k"].append(k_new.reshape(n_seq, WINDOW, HKV_D, HD_D))
            out["swa_v"].append(v_new.reshape(n_seq, WINDOW, HKV_D, HD_D))
            x = _out_proj([o_c, o_d], w["od_w_out"][j], x, t)
        x = _mem_attend_cached(x, w["norm_mem_g"][l], w["mem_w_q"][l], w["mem_qnorm_g"][l],
                               st["mem_k"].reshape(DEPTH, n_seq, N_MEM * H_X, HD_X),
                               st["mem_v"].reshape(DEPTH, n_seq, N_MEM * H_X, HD_X), w["mem_w_o"][l], layer=l, nseq=8)
        c0 = st["conv"][l].reshape(n_seq * (CONV_W - 1), D_FF)
        x, gt = _ffn(x, w["norm_ffn_g"][l], w["ffn_w_gate"][l], w["ffn_w_up"][l], w["ffn_conv_w"][l],
                     w["ffn_conv_b"][l], w["ffn_w_down"][l], tm=512, ffc=256, tiles_per_seq=1, c0=c0)
        out["conv"].append(gt.reshape(n_seq, CONV_W - 1, D_FF))
    return x, {name: jnp.stack(rows) for name, rows in out.items()}


def kernel(x_prompt, x_sample, state_ret, state_hgrn, cache_swa_k, cache_swa_v, cache_mem_k, cache_mem_v,
           state_ffn_conv, mem_prompt, norm_mix_g, norm_mem_g, norm_ffn_g, ev_w_in, ev_w_out, ret_gn_g,
           mlp_norm_g, mlp_w_s, mlp_b_s, od_w_in, od_w_out, hgrn_lb_logits, hgrn_onorm_g, swa_qnorm_g,
           swa_knorm_g, swa_sinks, mem_w_q, mem_w_kv, mem_qnorm_g, mem_knorm_g, mem_w_o, ffn_w_gate,
           ffn_w_up, ffn_conv_w, ffn_conv_b, ffn_w_down):
    bf = lambda a: a.astype(BF16)
    w = dict(norm_mix_g=norm_mix_g, norm_mem_g=norm_mem_g, norm_ffn_g=norm_ffn_g, ev_w_in=bf(ev_w_in),
             ev_w_out=bf(ev_w_out), ret_gn_g=ret_gn_g, mlp_norm_g=mlp_norm_g, mlp_w_s=mlp_w_s, mlp_b_s=mlp_b_s,
             od_w_in=bf(od_w_in), od_w_out=bf(od_w_out), hgrn_lb_logits=hgrn_lb_logits, hgrn_onorm_g=hgrn_onorm_g,
             swa_qnorm_g=swa_qnorm_g, swa_knorm_g=swa_knorm_g, swa_sinks=swa_sinks, mem_w_q=bf(mem_w_q),
             mem_w_kv=bf(mem_w_kv), mem_qnorm_g=mem_qnorm_g, mem_knorm_g=mem_knorm_g, mem_w_o=bf(mem_w_o),
             ffn_w_gate=bf(ffn_w_gate), ffn_w_up=bf(ffn_w_up), ffn_conv_w=ffn_conv_w, ffn_conv_b=ffn_conv_b,
             ffn_w_down=bf(ffn_w_down))
    b, seq, d = x_prompt.shape
    db, dseq, _ = x_sample.shape
    pos_prompt = np.arange(seq)
    pos_sample = PAST_LEN + np.arange(dseq)
    y_p, ns_p = _run_prompt(x_prompt.reshape(b * seq, d), pos_prompt, b, mem_prompt.reshape(b * N_MEM, d), w)
    st = dict(ret=state_ret, hgrn=state_hgrn, swa_k=cache_swa_k, swa_v=cache_swa_v, mem_k=cache_mem_k,
              mem_v=cache_mem_v, conv=state_ffn_conv)
    y_s, ns_s = _run_sample(x_sample.reshape(db * dseq, d), pos_sample, db, st, w)
    return (y_p.reshape(b, seq, d), y_s.reshape(db, dseq, d), ns_p["ret"], ns_p["hgrn"], ns_p["swa_k"], ns_p["swa_v"],
            ns_p["mem_k"], ns_p["mem_v"], ns_p["conv"], ns_s["ret"], ns_s["chunk_v"], ns_s["hgrn"], ns_s["swa_k"],
            ns_s["swa_v"], ns_s["conv"])
```

```python
import functools
import math

import jax
import jax.numpy as jnp
import numpy as np
from jax import lax
from jax.experimental import pallas as pl
from jax.experimental.pallas import tpu as pltpu

F32 = jnp.float32
BF16 = jnp.bfloat16

D_MODEL = 1024
DEPTH = 2
PAST_LEN = 16384
H_A, DV_A, DK_A = 4, 128, 64
RET_CHUNK = 128
RET_THETA = 10000.0
G_B, DG_B = 4, 128
MLP_CHUNK = 128
H_C, DK_C, DV_C = 4, 128, 128
HGRN_CHUNK = 64
HD_D, HQ_D, HKV_D = 64, 8, 2
WINDOW = 128
ROPE_THETA = 500000.0
ROT_DIM_D = HD_D // 4
N_MEM, H_X, HD_X = 256, 4, 128
D_FF = 2816
CONV_W = 3
FFN_DOWN_GROUP = 4
EPS = 1e-6
SQRT_HALF = float(np.sqrt(0.5))
EV_Q, EV_K, EV_V, EV_G, EV_U, EV_VB, EV_END = 0, 256, 512, 1024, 1536, 2048, 2560
OD_Q, OD_F, OD_I, OD_G, OD_QD, OD_KD, OD_VD, OD_END = 0, 512, 1024, 1536, 2048, 2560, 2688, 2816

VMEM_LIMIT_BYTES = 56 * 1024 * 1024


def _cparams(sem):
    return pltpu.CompilerParams(dimension_semantics=sem, vmem_limit_bytes=VMEM_LIMIT_BYTES)


def _resident(shape):
    return pl.BlockSpec(shape, lambda *_: (0,) * len(shape), pipeline_mode=pl.Buffered(1))


def _rms(x, g):
    return x * lax.rsqrt(jnp.mean(x * x, axis=-1, keepdims=True) + EPS) * g


def _ln(x, g):
    mu = jnp.mean(x, axis=-1, keepdims=True)
    xc = x - mu
    return xc * lax.rsqrt(jnp.mean(xc * xc, axis=-1, keepdims=True) + EPS) * g


def _sigmoid(x):
    return 1.0 / (1.0 + jnp.exp(-x))


def _silu(x):
    return x * _sigmoid(x)


def _gelu(x):
    return 0.5 * x * (1.0 + lax.erf(x * SQRT_HALF))


def _dot(a, b):
    return jnp.dot(a.astype(BF16), b.astype(BF16), preferred_element_type=F32)


def _dot_nt(a, b):
    return lax.dot_general(a.astype(BF16), b.astype(BF16), (((1,), (1,)), ((), ())), preferred_element_type=F32)


def _dot_tn(a, b):
    return lax.dot_general(a.astype(BF16), b.astype(BF16), (((0,), (0,)), ((), ())), preferred_element_type=F32)


def _split3(x):
    hi = x.astype(BF16)
    r1 = x - hi.astype(F32)
    mid = r1.astype(BF16)
    lo = (r1 - mid.astype(F32)).astype(BF16)
    return hi, mid, lo


def _norm_proj_body(x_ref, g_ref, w_ref, o_ref):
    h = _rms(x_ref[...], g_ref[...]).astype(BF16)
    n = o_ref.shape[1]
    for n0 in range(0, n, 512):
        n1 = min(n0 + 512, n)
        o_ref[:, n0:n1] = jnp.dot(h, w_ref[:, n0:n1], preferred_element_type=F32)


def _norm_proj(x, g, w, tm):
    t, d = x.shape
    n = w.shape[1]
    return pl.pallas_call(
        _norm_proj_body,
        grid=(t // tm,),
        in_specs=[pl.BlockSpec((tm, d), lambda i: (i, 0)),
                  pl.BlockSpec((1, d), lambda i: (0, 0)),
                  pl.BlockSpec((d, n), lambda i: (0, 0))],
        out_specs=pl.BlockSpec((tm, n), lambda i: (i, 0)),
        out_shape=jax.ShapeDtypeStruct((t, n), F32),
        compiler_params=_cparams(("arbitrary",)),
        name="norm_proj",
    )(x, g.reshape(1, d), w)


def _out_proj_body(*refs):
    *o_refs, w_ref, x_ref, y_ref = refs
    acc = x_ref[...]
    k0 = 0
    for o_ref in o_refs:
        k1 = k0 + o_ref.shape[1]
        acc = acc + jnp.dot(o_ref[...].astype(BF16), w_ref[k0:k1, :], preferred_element_type=F32)
        k0 = k1
    y_ref[...] = acc


def _out_proj(os, w, x, tm):
    t, d = x.shape
    return pl.pallas_call(
        _out_proj_body,
        grid=(t // tm,),
        in_specs=[pl.BlockSpec((tm, o.shape[1]), lambda i: (i, 0)) for o in os]
        + [pl.BlockSpec(w.shape, lambda i: (0, 0)), pl.BlockSpec((tm, d), lambda i: (i, 0))],
        out_specs=pl.BlockSpec((tm, d), lambda i: (i, 0)),
        out_shape=jax.ShapeDtypeStruct((t, d), F32),
        compiler_params=_cparams(("arbitrary",)),
        name="out_proj",
    )(*os, w, x)


def _retention_consts(c, nseq=1):
    lg = np.log1p(-np.exp2(-5.0 - np.arange(H_A, dtype=np.float64)))
    idx = np.arange(c, dtype=np.float64)
    rel = idx[:, None] - idx[None, :]
    dmask = np.where(rel >= 0, np.exp(rel[None] * lg[:, None, None]), 0.0)
    q_dec = np.exp((idx + 1.0)[None, :] * lg[:, None])
    k_dec = np.exp((c - 1.0 - idx)[None, :] * lg[:, None])
    c_dec = np.exp(c * lg)
    qdec = np.tile(np.repeat(q_dec.T, DK_A, axis=1), (nseq, 1))
    kdec = np.tile(np.repeat(k_dec.T, DK_A, axis=1), (nseq, 1))
    dmask = np.stack([np.kron(np.eye(nseq), dmask[h]) for h in range(H_A)])
    cdec = np.broadcast_to(c_dec[:, None, None], (H_A, 8, DV_A))
    return tuple(jnp.asarray(a, F32) for a in (dmask, qdec, kdec, cdec))


def _rope_tables(pos, rot_dim, theta, head_dim, reps, row_reps=1):
    half = rot_dim // 2
    inv = theta ** (-np.arange(half, dtype=np.float64) / half)
    ang = pos.astype(np.float64)[:, None] * inv[None, :]
    cos = np.cos(ang)
    sin = np.sin(ang)
    l = pos.shape[0]
    pad = head_dim - rot_dim
    cos_h = np.concatenate([cos, cos, np.ones((l, pad))], axis=1)
    sin_h = np.concatenate([-sin, sin, np.zeros((l, pad))], axis=1)
    return (jnp.asarray(np.tile(cos_h, (row_reps, reps)), F32), jnp.asarray(np.tile(sin_h, (row_reps, reps)), F32))


def _even_consts(c, gn_g, mn_g, w_s, b_s, nseq=1):
    dmask, qdec, kdec, cdec = _retention_consts(c, nseq)
    bs_full = jnp.tile(jnp.repeat(b_s[:, :c].T, DG_B, axis=1), (nseq, 1))
    arrays = [dmask, qdec, kdec, cdec, gn_g.reshape(1, H_A * DV_A), mn_g.reshape(1, G_B * DG_B),
              jnp.tile(w_s[:, :c, :c], (1, nseq, nseq)), bs_full]
    return arrays, [a.shape for a in arrays]


def _retention_state_batched(qd, kd, v, s0_ref, cdec_ref, nseq):
    n = qd.shape[0]
    own = (lax.broadcasted_iota(jnp.int32, (n, nseq * 128), 1) // 128
           == lax.broadcasted_iota(jnp.int32, (n, nseq * 128), 0) // (n // nseq))
    place = lambda x: jnp.where(own, jnp.concatenate([x] * nseq, axis=1), 0.0)
    inter, s_new = [None] * H_A, [None] * H_A
    for p in range(H_A // 2):
        h0, h1 = 2 * p, 2 * p + 1
        lanes = slice(p * 128, (p + 1) * 128)
        s_a, s_b = s0_ref[:, h0], s0_ref[:, h1]
        z = jnp.zeros_like(s_a)
        slab = jnp.concatenate([jnp.concatenate([s_a, z], axis=2), jnp.concatenate([z, s_b], axis=2)], axis=1)
        res = _dot(place(qd[:, lanes]), slab.reshape(nseq * 2 * DK_A, 2 * DV_A))
        inter[h0], inter[h1] = res[:, :DV_A], res[:, DV_A:]
        upd = _dot_tn(place(kd[:, lanes]), v[:, 2 * p * DV_A:2 * (p + 1) * DV_A]).reshape(nseq, 2 * DK_A, 2 * DV_A)
        s_new[h0] = cdec_ref[h0, 0:1, :] * s_a + upd[:, :DK_A, :DV_A]
        s_new[h1] = cdec_ref[h1, 0:1, :] * s_b + upd[:, DK_A:, DV_A:]
    return inter, s_new


def _even_segment(pj, cos, sin, consts, s_old, nseq=1):
    dmask_ref, qdec_ref, kdec_ref, cdec_ref, gng_ref, mng_ref, ws_ref, bs_ref = consts
    c = pj.shape[0]
    width = H_A * DK_A
    lane = lax.broadcasted_iota(jnp.int32, (c, width), 1)
    first_half = (lane % DK_A) < (DK_A // 2)
    row = lax.broadcasted_iota(jnp.int32, (c, c), 0)
    col = lax.broadcasted_iota(jnp.int32, (c, c), 1)
    tril = (col <= row) & (row // (c // nseq) == col // (c // nseq))

    def rope(x):
        rot = jnp.where(first_half, pltpu.roll(x, width - DK_A // 2, 1), pltpu.roll(x, DK_A // 2, 1))
        return x * cos + rot * sin

    q = rope(pj[:, EV_Q:EV_K])
    k = rope(pj[:, EV_K:EV_V]) * (DK_A ** -0.5)
    v = pj[:, EV_V:EV_G]
    ga = pj[:, EV_G:EV_U]
    qd = q * qdec_ref[...]
    kd = k * kdec_ref[...]
    ks = [slice(h * DK_A, (h + 1) * DK_A) for h in range(H_A)]
    vs = [slice(h * DV_A, (h + 1) * DV_A) for h in range(H_A)]
    qk = [_dot_nt(q[:, ks[h]], k[:, ks[h]]) for h in range(H_A)]
    if nseq == 1:
        inter = [_dot(qd[:, ks[h]], s_old[h]) for h in range(H_A)]
        s_new = [cdec_ref[h, 0:1, :] * s_old[h] + _dot_tn(kd[:, ks[h]], v[:, vs[h]]) for h in range(H_A)]
    else:
        inter, s_new = _retention_state_batched(qd, kd, v, s_old, cdec_ref, nseq)
    vg = _gelu(pj[:, EV_VB:EV_END])
    vrows = [_ln(vg[:, gs], mng_ref[:, gs]) for gs in vs]
    mixed = [_dot(jnp.where(tril, ws_ref[g], 0.0), vrows[g]) for g in range(G_B)]
    yield s_new
    outs = []
    for h in range(H_A):
        o = _dot(qk[h] * dmask_ref[h], v[:, vs[h]]) + inter[h]
        outs.append(_silu(ga[:, vs[h]]) * _ln(o, gng_ref[:, vs[h]]))
    u = _gelu(pj[:, EV_U:EV_VB])
    for g, gs in enumerate(vs):
        outs.append(u[:, gs] * (mixed[g] + bs_ref[:, gs]))
    yield jnp.concatenate(outs, axis=1), s_new, jnp.concatenate(vrows, axis=1)


def _even_body(pj_ref, cos_ref, sin_ref, dmask_ref, qdec_ref, kdec_ref, cdec_ref, gng_ref, mng_ref, ws_ref, bs_ref,
               s0_ref, o_ref, sout_ref, vrows_ref, *, c, nseg):
    consts = (dmask_ref, qdec_ref, kdec_ref, cdec_ref, gng_ref, mng_ref, ws_ref, bs_ref)
    (o, s_new, vrows), = _interleave(_even_segment(pj_ref[...], cos_ref[...], sin_ref[...], consts, s0_ref, nseq=nseg))
    for h in range(H_A):
        sout_ref[:, h] = s_new[h]
    o_ref[...] = o
    vrows_ref[...] = vrows


def _even_mixer(proj, pos, s0, gn_g, mn_g, w_s, b_s, *, nseg):
    t = proj.shape[0]
    c = pos.shape[0]
    r = nseg * c
    consts, shapes = _even_consts(c, gn_g, mn_g, w_s, b_s, nseq=nseg)
    cos, sin = _rope_tables(pos, DK_A, RET_THETA, DK_A, H_A, row_reps=nseg)
    full = lambda shape: pl.BlockSpec(shape, lambda i: (0,) * len(shape))
    st_spec = pl.BlockSpec((nseg, H_A, DK_A, DV_A), lambda i: (i, 0, 0, 0))
    return pl.pallas_call(
        functools.partial(_even_body, c=c, nseg=nseg),
        grid=(t // r,),
        in_specs=[pl.BlockSpec((r, EV_END), lambda i: (i, 0)), full((r, 256)), full((r, 256))]
        + [full(s) for s in shapes] + [st_spec],
        out_specs=[pl.BlockSpec((r, 1024), lambda i: (i, 0)), st_spec, pl.BlockSpec((r, 512), lambda i: (i, 0))],
        out_shape=[jax.ShapeDtypeStruct((t, 1024), F32), jax.ShapeDtypeStruct(s0.shape, F32),
                   jax.ShapeDtypeStruct((t, 512), F32)],
        compiler_params=_cparams(("arbitrary",)),
        name="even_mixer",
    )(proj, cos, sin, *consts, s0)


def _even_layer_body(x_ref, ng_ref, win_ref, wout_ref, cos_ref, sin_ref, dmask_ref, qdec_ref, kdec_ref, cdec_ref,
                     gng_ref, mng_ref, ws_ref, bs_ref, y_ref, sout_ref, st_ref, *, c, pb):
    step = pl.program_id(1)

    @pl.when(step == 0)
    def _():
        st_ref[...] = jnp.zeros_like(st_ref)

    consts = (dmask_ref, qdec_ref, kdec_ref, cdec_ref, gng_ref, mng_ref, ws_ref, bs_ref)
    x = x_ref[...]
    h = _rms(x, ng_ref[...]).astype(BF16)
    ngrp = x.shape[0] // pb
    project = lambda g: jnp.dot(h[g * pb:(g + 1) * pb, :], win_ref[...], preferred_element_type=F32)
    proj = project(0)
    state = [st_ref[hh] for hh in range(H_A)]
    pending = None

    def finish(rows, seg):
        o, _, _ = next(seg)
        y_ref[rows, :] = x[rows, :] + jnp.dot(o.astype(BF16), wout_ref[...], preferred_element_type=F32)

    for g in range(ngrp):
        nxt = project(g + 1) if g + 1 < ngrp else None
        for j in range(pb // c):
            rows = slice(g * pb + j * c, g * pb + (j + 1) * c)
            seg = _even_segment(proj[j * c:(j + 1) * c, :], cos_ref[rows, :], sin_ref[rows, :], consts, state)
            state = next(seg)
            if pending is not None:
                finish(*pending)
            pending = (rows, seg)
        proj = nxt
    finish(*pending)
    for hh in range(H_A):
        st_ref[hh] = state[hh]

    @pl.when(step == pl.num_programs(1) - 1)
    def _():
        sout_ref[0] = st_ref[...]


def _even_layer(x, pos, ng, w_in, w_out, gn_g, mn_g, w_s, b_s, *, n_seq, r, pb):
    t, d = x.shape
    seq_len = pos.shape[0]
    c = math.gcd(seq_len, RET_CHUNK)
    assert c == min(MLP_CHUNK, seq_len) and pb % c == 0 and r % pb == 0 and seq_len % r == 0
    steps = seq_len // r
    consts, shapes = _even_consts(c, gn_g, mn_g, w_s, b_s)
    cos, sin = _rope_tables(pos, DK_A, RET_THETA, DK_A, H_A)
    rowb = pl.BlockSpec((r, d), lambda b, i: (b * steps + i, 0))
    tab = pl.BlockSpec((r, 256), lambda b, i: (i, 0))
    st_spec = pl.BlockSpec((1, H_A, DK_A, DV_A), lambda b, i: (b, 0, 0, 0))
    return pl.pallas_call(
        functools.partial(_even_layer_body, c=c, pb=pb),
        grid=(n_seq, steps),
        in_specs=[rowb, _resident((1, d)), _resident(w_in.shape), _resident(w_out.shape), tab, tab]
        + [_resident(s) for s in shapes],
        out_specs=[rowb, st_spec],
        out_shape=[jax.ShapeDtypeStruct((t, d), F32), jax.ShapeDtypeStruct((n_seq, H_A, DK_A, DV_A), F32)],
        scratch_shapes=[pltpu.VMEM((H_A, DK_A, DV_A), F32)],
        compiler_params=_cparams(("arbitrary", "arbitrary")),
        name="even_layer",
    )(x, ng.reshape(1, d), w_in, w_out, cos, sin, *consts)


def _hgrn_lower_bound(lbl, layer):
    e = jnp.exp(lbl - jnp.max(lbl, axis=0, keepdims=True))
    sm = e / jnp.sum(e, axis=0, keepdims=True)
    acc = sm[0:1, :]
    first = acc
    for i in range(1, layer + 1):
        acc = acc + sm[i:i + 1, :]
    return acc - first


def _block_row(x, blk, j):
    n, lanes = x.shape
    if blk % 16 == 0:
        return jnp.concatenate(
            [jnp.broadcast_to(x[i * blk + j:i * blk + j + 1, :], (blk, lanes)) for i in range(n // blk)], axis=0)
    x3 = x.reshape(n // 8, 8, lanes)
    pick = lambda jj: jnp.broadcast_to(x3[:, jj:jj + 1, :], x3.shape).reshape(n, lanes)
    if blk == 8:
        return pick(j)
    assert blk == 4
    upper = (lax.broadcasted_iota(jnp.int32, (n, 1), 0) % 8) >= 4
    return jnp.where(upper, pick(4 + j), pick(j))


def _hgrn_gates(pj, lb):
    f = lb + (1.0 - lb) * _sigmoid(pj[:, OD_F:OD_I])
    return f, jnp.log(f), 1.0 - f, _silu(pj[:, OD_Q:OD_F]), pj[:, OD_I:OD_G], pj[:, OD_G:OD_QD]


def _chunk_cumsum(g, c):
    n = g.shape[0]
    row = lax.broadcasted_iota(jnp.int32, (n, n), 0)
    col = lax.broadcasted_iota(jnp.int32, (n, n), 1)
    tri = ((col <= row) & (row // c == col // c)).astype(BF16)
    return sum(jnp.dot(tri, piece, preferred_element_type=F32) for piece in _split3(g))


def _hgrn_intra(q, k, f, b, c):
    return _hgrn_assemble(_hgrn_level_products(q, k, f, b, c))


def _hgrn_level_products(q, k, f, b, c):
    n = q.shape[0]
    rloc = lax.broadcasted_iota(jnp.int32, (n, 1), 0)
    diag = jnp.sum(q * k, axis=-1, keepdims=True)
    prods = []
    blk = 2
    while blk <= c:
        second = (rloc % blk) >= blk // 2
        if blk == 2:
            e = f
            ke = k
        else:
            dq = b - _block_row(b, blk, blk // 2 - 1)
            e = jnp.exp(jnp.where(second, dq, -dq))
            ke = k * e
        prods.append((blk, _dot_nt(jnp.where(second, q * e, 0.0), jnp.where(second, 0.0, ke))))
        blk *= 2
    return diag, prods


def _hgrn_assemble(level_products):
    diag, prods = level_products
    n = diag.shape[0]
    row = lax.broadcasted_iota(jnp.int32, (n, n), 0)
    col = lax.broadcasted_iota(jnp.int32, (n, n), 1)
    a = jnp.where(row == col, diag, 0.0)
    for blk, p in prods:
        a = a + jnp.where((row // blk) == (col // blk), p, 0.0)
    return a


def _interleave(*gens):
    results = [None] * len(gens)
    live = list(enumerate(gens))
    while live:
        still = []
        for i, gen in live:
            try:
                results[i] = next(gen)
                still.append((i, gen))
            except StopIteration:
                pass
        live = still
    return results


def _hgrn_tile(pj, lb, ong_ref, st_ref, c):
    n = pj.shape[0]
    heads = [slice(h * DK_C, (h + 1) * DK_C) for h in range(H_C)]
    chunks = [slice(j * c, (j + 1) * c) for j in range(n // c)]
    f, g, k, q, v, gc = _hgrn_gates(pj, lb)
    b = _chunk_cumsum(g, c)
    qe = q * jnp.exp(b)
    kd = k * jnp.exp(_block_row(b, c, c - 1) - b)
    levels = [_hgrn_level_products(q[:, hs], k[:, hs], f[:, hs], b[:, hs], c) for hs in heads]
    kv = [[_dot_tn(kd[rs, hs], v[rs, hs]) for rs in chunks] for hs in heads]
    yield None
    o_intra = [_dot(_hgrn_assemble(levels[h]), v[:, hs]) for h, hs in enumerate(heads)]
    yield None
    outs = []
    for h, hs in enumerate(heads):
        s = st_ref[h]
        parts = []
        for j, rs in enumerate(chunks):
            parts.append(o_intra[h][rs, :] + _dot(qe[rs, hs], s))
            bl = b[rs.stop - 1:rs.stop, hs]
            decay = jnp.broadcast_to(jnp.exp(bl), (DK_C, DK_C)).T
            s = decay * s + kv[h][j]
        st_ref[h] = s
        outs.append(_rms(jnp.concatenate(parts, axis=0), ong_ref[:, hs]) * _silu(gc[:, hs]))
    yield jnp.concatenate(outs, axis=1)


def _hgrn_body(pj_ref, lbl_ref, ong_ref, s0_ref, oc_ref, sout_ref, *, c, layer):
    pj = pj_ref[...]
    n = pj.shape[0]
    ns = n // c
    lb = _hgrn_lower_bound(lbl_ref[...], layer)
    f, g, k, q, v, gc = _hgrn_gates(pj, lb)
    b = _chunk_cumsum(g, c)
    blast = _block_row(b, c, c - 1)
    qe = q * jnp.exp(b)
    kd = k * jnp.exp(blast - b)
    own_cols = (lax.broadcasted_iota(jnp.int32, (n, ns * DK_C), 1) // DK_C
                == lax.broadcasted_iota(jnp.int32, (n, ns * DK_C), 0) // c)
    place = lambda x: jnp.where(own_cols, jnp.concatenate([x] * ns, axis=1), 0.0)
    seq_cols = (lax.broadcasted_iota(jnp.int32, (ns, ns * DK_C), 1) // DK_C
                == lax.broadcasted_iota(jnp.int32, (ns, ns * DK_C), 0))
    ones = jnp.ones((ns, DV_C), BF16)
    last_row = (lax.broadcasted_iota(jnp.int32, (ns, n), 1)
                == lax.broadcasted_iota(jnp.int32, (ns, n), 0) * c + (c - 1)).astype(BF16)
    b_last = sum(jnp.dot(last_row, piece, preferred_element_type=F32) for piece in _split3(b))
    for h in range(H_C):
        hs = slice(h * DK_C, (h + 1) * DK_C)
        vh = v[:, hs]
        s = s0_ref[:, h].reshape(ns * DK_C, DV_C)
        o = _dot(_hgrn_intra(q[:, hs], k[:, hs], f[:, hs], b[:, hs], c), vh) + _dot(place(qe[:, hs]), s)
        e_last = jnp.exp(b_last[:, hs])
        e_placed = jnp.where(seq_cols, jnp.concatenate([e_last] * ns, axis=1), 0.0)
        decay = sum(lax.dot_general(piece, ones, (((0,), (0,)), ((), ())), preferred_element_type=F32)
                    for piece in _split3(e_placed))
        s_new = decay * s + _dot_tn(place(kd[:, hs]), vh)
        sout_ref[:, h] = s_new.reshape(ns, DK_C, DV_C)
        oc_ref[:, hs] = _rms(o, ong_ref[:, hs]) * _silu(gc[:, hs])


def _hgrn_mixer(proj, lb_logits, on_g, s0, *, c, nseq, layer):
    t = proj.shape[0]
    r = nseq * c
    full = lambda shape: pl.BlockSpec(shape, lambda i: (0,) * len(shape))
    st_spec = pl.BlockSpec((nseq, H_C, DK_C, DV_C), lambda i: (i, 0, 0, 0))
    return pl.pallas_call(
        functools.partial(_hgrn_body, c=c, layer=layer),
        grid=(t // r,),
        in_specs=[pl.BlockSpec((r, OD_QD), lambda i: (i, 0)), full((DEPTH, 512)), full((1, 512)), st_spec],
        out_specs=[pl.BlockSpec((r, 512), lambda i: (i, 0)), st_spec],
        out_shape=[jax.ShapeDtypeStruct((t, 512), F32), jax.ShapeDtypeStruct(s0.shape, F32)],
        compiler_params=_cparams(("arbitrary",)),
        name="hgrn_mixer",
    )(proj, lb_logits, on_g.reshape(1, 512), s0)


def _head_norm(x, g):
    rows, width = x.shape
    lo = lax.broadcasted_iota(jnp.int32, (rows, 128), 1) < HD_D
    outs = []
    for t in range(width // 128):
        xt = x[:, t * 128:(t + 1) * 128]
        sq = xt * xt
        ss_lo = jnp.sum(jnp.where(lo, sq, 0.0), axis=-1, keepdims=True)
        ss_hi = jnp.sum(jnp.where(lo, 0.0, sq), axis=-1, keepdims=True)
        scale = jnp.where(lo, lax.rsqrt(ss_lo * (1.0 / HD_D) + EPS), lax.rsqrt(ss_hi * (1.0 / HD_D) + EPS))
        outs.append(xt * scale)
    return jnp.concatenate(outs, axis=1) * g


def _rope_partial(x, cos, sin):
    width = x.shape[1]
    half = ROT_DIM_D // 2
    lane = lax.broadcasted_iota(jnp.int32, x.shape, 1)
    rot = jnp.where((lane % HD_D) < half, pltpu.roll(x, width - half, 1), pltpu.roll(x, half, 1))
    return x * cos + rot * sin


def _swa_prep(q, k, cos, sin, qng_ref, kng_ref):
    qn = _rope_partial(_head_norm(q, qng_ref[...]), jnp.concatenate([cos] * (HQ_D // 2), axis=1),
                       jnp.concatenate([sin] * (HQ_D // 2), axis=1)).astype(BF16)
    return qn, _rope_partial(_head_norm(k, kng_ref[...]), cos, sin)


def _swa_place(x):
    lo = lax.broadcasted_iota(jnp.int32, x.shape, 1) < HD_D
    sw = pltpu.roll(x, HD_D, 1)
    return [[jnp.where(lo, x, 0.0).astype(BF16), jnp.where(lo, 0.0, sw).astype(BF16)],
            [jnp.where(lo, sw, 0.0).astype(BF16), jnp.where(lo, 0.0, x).astype(BF16)]]


def _swa_block(qn, k_prev, k_cur, v_prev, v_cur, sink_ref, allow_prev):
    rep = HQ_D // HKV_D
    lo = lax.broadcasted_iota(jnp.int32, (WINDOW, 128), 1) < HD_D
    qi = lax.broadcasted_iota(jnp.int32, (WINDOW, 2 * WINDOW), 0)
    ci = lax.broadcasted_iota(jnp.int32, (WINDOW, 2 * WINDOW), 1)
    cur = (ci >= WINDOW) & ((ci - WINDOW) <= qi)
    prev = (ci < WINDOW) & (ci > qi)
    mask = prev | cur if allow_prev is True else (prev & allow_prev) | cur
    keys = lambda g, half: jnp.concatenate([k_prev[g][half], k_cur[g][half]], axis=0)
    ones = [jnp.where(lo, 1.0, 0.0).astype(BF16), jnp.where(lo, 0.0, 1.0).astype(BF16)]
    vals = lambda g, half: jnp.concatenate(
        [jnp.concatenate([v_prev[g][half], v_cur[g][half]], axis=0), jnp.concatenate([ones[half]] * 2, axis=0)], axis=1)
    scores = [lax.dot_general(qn[:, (h // 2) * 128:(h // 2 + 1) * 128], keys(h // rep, h % 2),
                              (((1,), (1,)), ((), ())), preferred_element_type=F32) for h in range(HQ_D)]
    yield None
    pv, sink_term = [], []
    for h in range(HQ_D):
        s = jnp.where(mask, scores[h] * (HD_D ** -0.5), -jnp.inf)
        sink = sink_ref[h:h + 1, 0:1]
        m = jnp.maximum(jnp.max(s, axis=-1, keepdims=True), sink)
        pv.append(jnp.dot(jnp.exp(s - m).astype(BF16), vals(h // rep, h % 2), preferred_element_type=F32))
        sink_term.append(jnp.exp(sink - m))
    yield None
    tiles = []
    for t in range(HQ_D // 2):
        both = pv[2 * t] + pv[2 * t + 1]
        den = both[:, 128:] + jnp.where(lo, sink_term[2 * t], sink_term[2 * t + 1])
        tiles.append(both[:, :128] / den)
    yield jnp.concatenate(tiles, axis=1)


def _swa_sample_chain(q2, kt_cache, k_new, vt_cache, v_new, sinks2, half, l):
    rows = q2.shape[0]
    ns = rows // (2 * l)
    lo = lax.broadcasted_iota(jnp.int32, (1, 128), 1) < HD_D
    ones = jnp.where(lo, 1.0, 0.0) if half == 0 else jnp.where(lo, 0.0, 1.0)
    ext = lambda v: jnp.concatenate([v, jnp.broadcast_to(ones, v.shape).astype(BF16)], axis=1)
    nt = lambda a, b: lax.dot_general(a, b, (((1,), (1,)), ((), ())), preferred_element_type=F32)
    zeros = jnp.zeros((HD_D, kt_cache.shape[1]), BF16)
    band = lambda x: jnp.concatenate([x, zeros] if half == 0 else [zeros, x], axis=0)
    k_cache = band(kt_cache.astype(BF16))
    v_cache = jnp.concatenate([band(vt_cache.astype(BF16)), band(jnp.ones_like(zeros))], axis=0)
    s1 = jnp.dot(q2, k_cache, preferred_element_type=F32)
    s2 = nt(q2, k_new)
    yield None
    r1 = lax.broadcasted_iota(jnp.int32, s1.shape, 0)
    c1 = lax.broadcasted_iota(jnp.int32, s1.shape, 1)
    ok1 = ((r1 % (ns * l)) // l == c1 // WINDOW) & (c1 % WINDOW > r1 % l)
    r2 = lax.broadcasted_iota(jnp.int32, s2.shape, 0)
    c2 = lax.broadcasted_iota(jnp.int32, s2.shape, 1)
    ok2 = ((r2 % (ns * l)) // l == c2 // l) & (c2 % l <= r2 % l)
    s1 = jnp.where(ok1, s1 * (HD_D ** -0.5), -jnp.inf)
    s2 = jnp.where(ok2, s2 * (HD_D ** -0.5), -jnp.inf)
    sink = jnp.where(lax.broadcasted_iota(jnp.int32, (rows, 1), 0) < ns * l, sinks2[0], sinks2[1])
    m = jnp.maximum(jnp.maximum(jnp.max(s1, axis=-1, keepdims=True), jnp.max(s2, axis=-1, keepdims=True)), sink)
    res = (nt(jnp.exp(s1 - m).astype(BF16), v_cache)
           + jnp.dot(jnp.exp(s2 - m).astype(BF16), ext(v_new), preferred_element_type=F32))
    yield res, jnp.exp(sink - m)


def _swa_sample_body(q_ref, k_ref, v_ref, kc_ref, vc_ref, cos_ref, sin_ref, qng_ref, kng_ref, sink_ref,
                     o_ref, kout_ref, vout_ref, *, nseq, l, group):
    rep = HQ_D // HKV_D
    qn, kn = _swa_prep(q_ref[...], k_ref[...], cos_ref[...], sin_ref[...], qng_ref, kng_ref)
    v = v_ref[...]
    lo = lax.broadcasted_iota(jnp.int32, (group * l, 128), 1) < HD_D
    r_sel = lax.broadcasted_iota(jnp.int32, (group * l, group * WINDOW), 0)
    c_sel = lax.broadcasted_iota(jnp.int32, (group * l, group * WINDOW), 1)
    append = ((c_sel // WINDOW == r_sel // l) & (c_sel % WINDOW == WINDOW - l + r_sel % l)).astype(BF16)
    is_new = lax.broadcasted_iota(jnp.int32, (HKV_D * HD_D, group * WINDOW), 1) % WINDOW >= WINDOW - l

    def shifted(old_t, new_rows):
        moved = sum(lax.dot_general(piece, append, (((0,), (0,)), ((), ())), preferred_element_type=F32)
                    for piece in _split3(new_rows))
        return jnp.where(is_new, moved, pltpu.roll(old_t, group * WINDOW - l, 1))

    chains, where = [], []
    for gi in range(nseq // group):
        rs = slice(gi * group * l, (gi + 1) * group * l)
        seqs = range(gi * group, (gi + 1) * group)
        kt = [jnp.concatenate([kc_ref[s, g] for s in seqs], axis=1) for g in range(HKV_D)]
        vt = [jnp.concatenate([vc_ref[s, g] for s in seqs], axis=1) for g in range(HKV_D)]
        k_new, v_new = _swa_place(kn[rs, :]), _swa_place(v[rs, :])
        for g in range(HKV_D):
            tiles = [(g * rep) // 2, (g * rep) // 2 + 1]
            q2 = jnp.concatenate([qn[rs, t * 128:(t + 1) * 128] for t in tiles], axis=0)
            for half in range(2):
                sinks2 = [sink_ref[2 * t + half:2 * t + half + 1, 0:1] for t in tiles]
                chains.append(_swa_sample_chain(q2, kt[g], k_new[g][half], vt[g], v_new[g][half], sinks2, half, l))
                where.append((rs, tiles, half))
        k_all = shifted(jnp.concatenate(kt, axis=0), kn[rs, :])
        v_all = shifted(jnp.concatenate(vt, axis=0), v[rs, :])
        for i, s in enumerate(seqs):
            for g in range(HKV_D):
                kout_ref[s, g] = k_all[g * HD_D:(g + 1) * HD_D, i * WINDOW:(i + 1) * WINDOW]
                vout_ref[s, g] = v_all[g * HD_D:(g + 1) * HD_D, i * WINDOW:(i + 1) * WINDOW]
    results = _interleave(*chains)
    for i in range(0, len(chains), 2):
        (rs, tiles, _), (res0, st0), (res1, st1) = where[i], results[i], results[i + 1]
        both = res0 + res1
        for j, t in enumerate(tiles):
            js = slice(j * group * l, (j + 1) * group * l)
            den = both[js, 128:] + jnp.where(lo, st0[js, :], st1[js, :])
            o_ref[rs, t * 128:(t + 1) * 128] = both[js, :128] / den


def _swa_gains(qn_g, kn_g):
    return (jnp.tile(qn_g, HQ_D).reshape(1, HQ_D * HD_D), jnp.tile(kn_g, HKV_D).reshape(1, HKV_D * HD_D))


def _swa_sample(proj, pos, kcache, vcache, qn_g, kn_g, sinks, *, nseq, group):
    t = proj.shape[0]
    l = pos.shape[0]
    r = nseq * l
    cos, sin = _rope_tables(pos, ROT_DIM_D, ROPE_THETA, HD_D, 2, row_reps=nseq)
    qng, kng = _swa_gains(qn_g, kn_g)
    rowb = lambda w, j: pl.BlockSpec((r, w), lambda i, j=j: (i, j))
    full = lambda shape: pl.BlockSpec(shape, lambda i: (0,) * len(shape))
    cache = pl.BlockSpec((nseq, HKV_D, HD_D, WINDOW), lambda i: (i, 0, 0, 0))
    return pl.pallas_call(
        functools.partial(_swa_sample_body, nseq=nseq, l=l, group=group),
        grid=(t // r,),
        in_specs=[rowb(512, OD_QD // 512), rowb(128, OD_KD // 128), rowb(128, OD_VD // 128), cache, cache,
                  full((r, 128)), full((r, 128)), full((1, 512)), full((1, 128)), full((HQ_D, 128))],
        out_specs=[rowb(512, 0), cache, cache],
        out_shape=[jax.ShapeDtypeStruct((t, 512), F32), jax.ShapeDtypeStruct(kcache.shape, F32),
                   jax.ShapeDtypeStruct(vcache.shape, F32)],
        compiler_params=_cparams(("arbitrary",)),
        name="swa_sample",
    )(proj, proj, proj, kcache, vcache, cos, sin, qng, kng, jnp.broadcast_to(sinks[:, None], (HQ_D, 128)))


def _odd_layer_body(x_ref, ng_ref, win_ref, wout_ref, lbl_ref, ong_ref, cos_ref, sin_ref, qng_ref, kng_ref, sink_ref,
                    y_ref, sout_ref, kout_ref, vout_ref, st_ref, kprev_ref, vprev_ref, *, c, pb, layer):
    step = pl.program_id(1)

    @pl.when(step == 0)
    def _():
        st_ref[...] = jnp.zeros_like(st_ref)
        kprev_ref[...] = jnp.zeros_like(kprev_ref)
        vprev_ref[...] = jnp.zeros_like(vprev_ref)

    x = x_ref[...]
    h = _rms(x, ng_ref[...]).astype(BF16)
    lb = _hgrn_lower_bound(lbl_ref[...], layer)
    ngrp = x.shape[0] // pb
    project = lambda g: jnp.dot(h[g * pb:(g + 1) * pb, :], win_ref[...], preferred_element_type=F32)
    proj = project(0)
    k_last, v_last = kprev_ref[...], vprev_ref[...]
    k_prev, v_prev = _swa_place(k_last), _swa_place(v_last)
    for g in range(ngrp):
        nxt = project(g + 1) if g + 1 < ngrp else None
        grows = slice(g * pb, (g + 1) * pb)
        qn, kn = _swa_prep(proj[:, OD_QD:OD_KD], proj[:, OD_KD:OD_VD], cos_ref[grows, :], sin_ref[grows, :],
                           qng_ref, kng_ref)
        staged = []
        for u in range(pb // WINDOW):
            us = slice(u * WINDOW, (u + 1) * WINDOW)
            k_last, v_last = kn[us, :], proj[us, OD_VD:OD_END]
            k_cur, v_cur = _swa_place(k_last), _swa_place(v_last)
            staged.append(_hgrn_tile(proj[us, :], lb, ong_ref, st_ref, c))
            staged.append(_swa_block(qn[us, :], k_prev, k_cur, v_prev, v_cur, sink_ref,
                                     True if g + u > 0 else step > 0))
            k_prev, v_prev = k_cur, v_cur
        outs = _interleave(*staged)
        o = jnp.concatenate([jnp.concatenate(outs[0::2], axis=0), jnp.concatenate(outs[1::2], axis=0)], axis=1)
        y_ref[grows, :] = x[grows, :] + jnp.dot(o.astype(BF16), wout_ref[...], preferred_element_type=F32)
        proj = nxt
    kprev_ref[...] = k_last
    vprev_ref[...] = v_last
    kout_ref[0] = k_last
    vout_ref[0] = v_last

    @pl.when(step == pl.num_programs(1) - 1)
    def _():
        sout_ref[0] = st_ref[...]


def _odd_layer(x, pos, ng, w_in, w_out, lb_logits, on_g, qn_g, kn_g, sinks, *, n_seq, r, pb, layer):
    t, d = x.shape
    seq_len = pos.shape[0]
    c = math.gcd(seq_len, HGRN_CHUNK)
    assert WINDOW % c == 0 and pb % WINDOW == 0 and r % pb == 0 and seq_len % r == 0
    steps = seq_len // r
    cos, sin = _rope_tables(pos, ROT_DIM_D, ROPE_THETA, HD_D, 2)
    qng, kng = _swa_gains(qn_g, kn_g)
    rowb = pl.BlockSpec((r, d), lambda b, i: (b * steps + i, 0))
    tab = pl.BlockSpec((r, 128), lambda b, i: (i, 0))
    st_spec = pl.BlockSpec((1, H_C, DK_C, DV_C), lambda b, i: (b, 0, 0, 0))
    kv_out = pl.BlockSpec((1, WINDOW, 128), lambda b, i: (b, 0, 0))
    return pl.pallas_call(
        functools.partial(_odd_layer_body, c=c, pb=pb, layer=layer),
        grid=(n_seq, steps),
        in_specs=[rowb, _resident((1, d)), _resident(w_in.shape), _resident(w_out.shape), _resident((DEPTH, 512)),
                  _resident((1, 512)), tab, tab, _resident((1, 512)), _resident((1, 128)), _resident((HQ_D, 128))],
        out_specs=[rowb, st_spec, kv_out, kv_out],
        out_shape=[jax.ShapeDtypeStruct((t, d), F32), jax.ShapeDtypeStruct((n_seq, H_C, DK_C, DV_C), F32),
                   jax.ShapeDtypeStruct((n_seq, WINDOW, 128), F32), jax.ShapeDtypeStruct((n_seq, WINDOW, 128), F32)],
        scratch_shapes=[pltpu.VMEM((H_C, DK_C, DV_C), F32), pltpu.VMEM((WINDOW, 128), F32),
                        pltpu.VMEM((WINDOW, 128), F32)],
        compiler_params=_cparams(("arbitrary", "arbitrary")),
        name="odd_layer",
    )(x, ng.reshape(1, d), w_in, w_out, lb_logits, on_g.reshape(1, 512), cos, sin, qng, kng,
      jnp.broadcast_to(sinks[:, None], (HQ_D, 128)))


def _mem_kv_body(m_ref, w_ref, g_ref, k_ref, v_ref):
    kv = jnp.dot(m_ref[...].astype(BF16), w_ref[...], preferred_element_type=F32)
    hw = H_X * HD_X
    for h in range(H_X):
        hs = slice(h * HD_X, (h + 1) * HD_X)
        k_ref[:, hs] = _rms(kv[:, hs], g_ref[...])
    v_ref[...] = kv[:, hw:]


def _mem_kv(mem, w, g):
    t = mem.shape[0]
    hw = H_X * HD_X
    return pl.pallas_call(
        _mem_kv_body,
        out_shape=[jax.ShapeDtypeStruct((t, hw), F32), jax.ShapeDtypeStruct((t, hw), F32)],
        compiler_params=pltpu.CompilerParams(vmem_limit_bytes=VMEM_LIMIT_BYTES),
        name="mem_kv",
    )(mem, w, g.reshape(1, HD_X))


def _mem_attend_body(x_ref, g_ref, wq_ref, qng_ref, mk_ref, mv_ref, wo_ref, y_ref, att_ref, *, nseq):
    x = x_ref[...]
    q = jnp.dot(_rms(x, g_ref[...]).astype(BF16), wq_ref[...], preferred_element_type=F32)
    rows_per = x.shape[0] // nseq
    heads = [slice(h * HD_X, (h + 1) * HD_X) for h in range(H_X)]
    for s in range(nseq):
        rs = slice(s * rows_per, (s + 1) * rows_per)
        scores = [_dot_nt(_rms(q[rs, hs], qng_ref[...]), mk_ref[s, :, hs]) for hs in heads]
        for hs, sc in zip(heads, scores):
            sc = sc * (HD_X ** -0.5)
            p = jnp.exp(sc - jnp.max(sc, axis=-1, keepdims=True))
            att_ref[rs, hs] = _dot(p, mv_ref[s, :, hs]) / jnp.sum(p, axis=-1, keepdims=True)
    y_ref[...] = x + jnp.dot(att_ref[...].astype(BF16), wo_ref[...], preferred_element_type=F32)


def _mem_attend(x, g, wq, qn_g, mk, mv, wo, *, tm, nseq, tiles_per_mem):
    t, d = x.shape
    hw = H_X * HD_X
    full = lambda shape: pl.BlockSpec(shape, lambda i: (0,) * len(shape))
    mem_spec = pl.BlockSpec((nseq, N_MEM, hw), lambda i: (i // tiles_per_mem, 0, 0))
    return pl.pallas_call(
        functools.partial(_mem_attend_body, nseq=nseq),
        grid=(t // tm,),
        in_specs=[pl.BlockSpec((tm, d), lambda i: (i, 0)), full((1, d)), full((d, hw)), full((1, HD_X)),
                  mem_spec, mem_spec, full((hw, d))],
        out_specs=pl.BlockSpec((tm, d), lambda i: (i, 0)),
        out_shape=jax.ShapeDtypeStruct((t, d), F32),
        scratch_shapes=[pltpu.VMEM((tm, hw), F32)],
        compiler_params=_cparams(("arbitrary",)),
        name="mem_attend",
    )(x, g.reshape(1, d), wq, qn_g.reshape(1, HD_X), mk, mv, wo)


def _mem_attend_cached_body(x_ref, g_ref, wq_ref, qng_ref, mk_ref, mv_ref, wo_ref, y_ref, att_ref, *, nseq):
    x = x_ref[...]
    q = jnp.dot(_rms(x, g_ref[...]).astype(BF16), wq_ref[...], preferred_element_type=F32)
    l = x.shape[0] // nseq
    rows_all = H_X * l
    cols = N_MEM * H_X
    own = (lax.broadcasted_iota(jnp.int32, (rows_all, cols), 1) % H_X
           == lax.broadcasted_iota(jnp.int32, (rows_all, cols), 0) // l)
    scores = []
    for s in range(nseq):
        rs = slice(s * l, (s + 1) * l)
        qs = jnp.concatenate([_rms(q[rs, h * HD_X:(h + 1) * HD_X], qng_ref[...]) for h in range(H_X)], axis=0)
        scores.append(_dot_nt(qs, mk_ref[0, s]))
    for s in range(nseq):
        rs = slice(s * l, (s + 1) * l)
        sc = jnp.where(own, scores[s] * (HD_X ** -0.5), -jnp.inf)
        m = jnp.max(sc, axis=-1, keepdims=True)
        p = jnp.exp(sc - m)
        o = _dot(p, mv_ref[0, s]) / jnp.sum(p, axis=-1, keepdims=True)
        for h in range(H_X):
            att_ref[rs, h * HD_X:(h + 1) * HD_X] = o[h * l:(h + 1) * l, :]
    y_ref[...] = x + jnp.dot(att_ref[...].astype(BF16), wo_ref[...], preferred_element_type=F32)


def _mem_attend_cached(x, g, wq, qn_g, mk, mv, wo, *, layer, nseq):
    t, d = x.shape
    hw = H_X * HD_X
    tm = nseq * (t // mk.shape[1])
    full = lambda shape: pl.BlockSpec(shape, lambda i: (0,) * len(shape))
    mem_spec = pl.BlockSpec((1, nseq, N_MEM * H_X, HD_X), lambda i: (layer, i, 0, 0))
    return pl.pallas_call(
        functools.partial(_mem_attend_cached_body, nseq=nseq),
        grid=(t // tm,),
        in_specs=[pl.BlockSpec((tm, d), lambda i: (i, 0)), full((1, d)), full((d, hw)), full((1, HD_X)),
                  mem_spec, mem_spec, full((hw, d))],
        out_specs=pl.BlockSpec((tm, d), lambda i: (i, 0)),
        out_shape=jax.ShapeDtypeStruct((t, d), F32),
        scratch_shapes=[pltpu.VMEM((tm, hw), F32)],
        compiler_params=_cparams(("arbitrary",)),
        name="mem_attend_cached",
    )(x, g.reshape(1, d), wq, qn_g.reshape(1, HD_X), mk, mv, wo)


def _ffn_body(*refs, tm, ffc, tiles_per_seq, per_row_state):
    if per_row_state:
        x_ref, g_ref, wg_ref, wu_ref, cw_ref, cb_ref, wd_ref, c0_ref, y_ref, gt_ref, tail_ref = refs
    else:
        x_ref, g_ref, wg_ref, wu_ref, cw_ref, cb_ref, wd_ref, y_ref, gt_ref, tail_ref = refs
    ff = wg_ref.shape[1]
    x = x_ref[...]
    h = _rms(x, g_ref[...]).astype(BF16)
    hist = CONV_W - 1
    if per_row_state:
        ns = tm // 8
        r_in = lax.broadcasted_iota(jnp.int32, (tm, hist * ns), 0)
        c_in = lax.broadcasted_iota(jnp.int32, (tm, hist * ns), 1)
        same = (r_in // 8) == (c_in // hist)
        sel1 = (same & (r_in % 8 == 0) & (c_in % hist == hist - 1)).astype(BF16)
        sel2 = (same & (r_in % 8 < hist) & (c_in % hist == r_in % 8)).astype(BF16)
        r_out = lax.broadcasted_iota(jnp.int32, (hist * ns, tm), 0)
        c_out = lax.broadcasted_iota(jnp.int32, (hist * ns, tm), 1)
        sel_out = (((c_out // 8) == (r_out // hist)) & (c_out % 8 == 8 - hist + r_out % hist)).astype(BF16)

        def select(sel, val):
            return sum(jnp.dot(sel, piece, preferred_element_type=F32) for piece in _split3(val))
    else:
        @pl.when(pl.program_id(0) % tiles_per_seq == 0)
        def _():
            tail_ref[...] = jnp.zeros_like(tail_ref)

    row8 = lax.broadcasted_iota(jnp.int32, (8, ffc), 0)
    acc = None

    def down(acts):
        a = acts[0][0] if len(acts) == 1 else jnp.concatenate([p[0] for p in acts], axis=1)
        part = jnp.dot(a, wd_ref[acts[0][1].start:acts[-1][1].stop, :], preferred_element_type=F32)
        return part if acc is None else acc + part

    filling, ready = [], None
    for c0 in range(0, ff, ffc):
        cs = slice(c0, c0 + ffc)
        gate = jnp.dot(h, wg_ref[:, cs], preferred_element_type=F32)
        up = jnp.dot(h, wu_ref[:, cs], preferred_element_type=F32)
        if ready is not None:
            acc = down(ready)
            ready = None
        r1 = pltpu.roll(gate, 1, 0)
        r2 = pltpu.roll(gate, 2, 0)
        if per_row_state:
            pos = lax.broadcasted_iota(jnp.int32, gate.shape, 0) % 8
            c0c = c0_ref[:, cs]
            g1 = jnp.where(pos == 0, select(sel1, c0c), r1)
            g2 = jnp.where(pos < hist, select(sel2, c0c), r2)
            gt_ref[:, cs] = select(sel_out, gate)
        else:
            prev = tail_ref[:, cs]
            top1 = jnp.where(row8 == 0, prev[7:8, :], r1[0:8, :])
            top2 = jnp.where(row8 == 0, prev[6:7, :], jnp.where(row8 == 1, prev[7:8, :], r2[0:8, :]))
            g1 = jnp.concatenate([top1, r1[8:, :]], axis=0)
            g2 = jnp.concatenate([top2, r2[8:, :]], axis=0)
            tail_ref[:, cs] = gate[tm - 8:, :]
            gt_ref[:, cs] = gate[tm - 8:, :]
        conv = cb_ref[:, cs] + cw_ref[0:1, cs] * g2 + cw_ref[1:2, cs] * g1 + cw_ref[2:3, cs] * gate
        filling.append(((_gelu(conv) * up).astype(BF16), cs))
        if len(filling) == FFN_DOWN_GROUP:
            filling, ready = [], filling
    for acts in (ready, filling):
        if acts:
            acc = down(acts)
    y_ref[...] = x + acc


def _ffn(x, g, wg, wu, cw, cb, wd, *, tm, ffc, tiles_per_seq, c0=None):
    t, d = x.shape
    ff = wg.shape[1]
    per_row_state = c0 is not None
    tail_n = tm // 8 * (CONV_W - 1) if per_row_state else 8
    in_specs = [pl.BlockSpec((tm, d), lambda i: (i, 0)), _resident((1, d)), _resident((d, ff)), _resident((d, ff)),
                _resident((8, ff)), _resident((1, ff)), _resident((ff, d))]
    args = [x, g.reshape(1, d), wg, wu, jnp.pad(cw, ((0, 8 - CONV_W), (0, 0))), cb.reshape(1, ff), wd]
    if per_row_state:
        in_specs += [pl.BlockSpec((tail_n, ff), lambda i: (i, 0))]
        args += [c0]
    return pl.pallas_call(
        functools.partial(_ffn_body, tm=tm, ffc=ffc, tiles_per_seq=tiles_per_seq, per_row_state=per_row_state),
        grid=(t // tm,),
        in_specs=in_specs,
        out_specs=[pl.BlockSpec((tm, d), lambda i: (i, 0)), pl.BlockSpec((tail_n, ff), lambda i: (i, 0))],
        out_shape=[jax.ShapeDtypeStruct((t, d), F32), jax.ShapeDtypeStruct((t // tm * tail_n, ff), F32)],
        scratch_shapes=[pltpu.VMEM((8, ff), F32)],
        compiler_params=_cparams(("arbitrary",)),
        name="ffn",
    )(*args)


def _run_prompt(x, pos, n_seq, mem, w):
    seq_len = pos.shape[0]
    tm = 512
    hw = H_X * HD_X
    out = {k: [] for k in ("ret", "hgrn", "swa_k", "swa_v", "mem_k", "mem_v", "conv")}
    for l in range(DEPTH):
        j = l // 2
        if l % 2 == 0:
            x, s_new = _even_layer(x, pos, w["norm_mix_g"][l], w["ev_w_in"][j], w["ev_w_out"][j], w["ret_gn_g"][j],
                                   w["mlp_norm_g"][j], w["mlp_w_s"][j], w["mlp_b_s"][j], n_seq=n_seq, r=1024, pb=256)
            out["ret"].append(s_new)
        else:
            x, s_new, k_new, v_new = _odd_layer(x, pos, w["norm_mix_g"][l], w["od_w_in"][j], w["od_w_out"][j],
                                                w["hgrn_lb_logits"], w["hgrn_onorm_g"][j], w["swa_qnorm_g"][j],
                                                w["swa_knorm_g"][j], w["swa_sinks"][j], n_seq=n_seq, r=512, pb=256,
                                                layer=l)
            out["hgrn"].append(s_new)
            out["swa_k"].append(k_new.reshape(n_seq, WINDOW, HKV_D, HD_D))
            out["swa_v"].append(v_new.reshape(n_seq, WINDOW, HKV_D, HD_D))
        mk, mv = _mem_kv(mem, w["mem_w_kv"][l], w["mem_knorm_g"][l])
        out["mem_k"].append(mk.reshape(n_seq, N_MEM, H_X, HD_X))
        out["mem_v"].append(mv.reshape(n_seq, N_MEM, H_X, HD_X))
        x = _mem_attend(x, w["norm_mem_g"][l], w["mem_w_q"][l], w["mem_qnorm_g"][l], mk.reshape(n_seq, N_MEM, hw),
                        mv.reshape(n_seq, N_MEM, hw), w["mem_w_o"][l], tm=tm, nseq=1, tiles_per_mem=seq_len // tm)
        x, gt = _ffn(x, w["norm_ffn_g"][l], w["ffn_w_gate"][l], w["ffn_w_up"][l], w["ffn_conv_w"][l],
                     w["ffn_conv_b"][l], w["ffn_w_down"][l], tm=tm, ffc=256, tiles_per_seq=seq_len // tm)
        out["conv"].append(gt.reshape(n_seq, seq_len // tm, 8, D_FF)[:, -1, 8 - (CONV_W - 1):, :])
    return x, {name: jnp.stack(rows) for name, rows in out.items()}


def _run_sample(x, pos, n_seq, st, w):
    seq_len = pos.shape[0]
    assert seq_len == 8
    t = x.shape[0]
    out = {k: [] for k in ("ret", "chunk_v", "hgrn", "swa_k", "swa_v", "conv")}
    for l in range(DEPTH):
        j = l // 2
        if l % 2 == 0:
            proj = _norm_proj(x, w["norm_mix_g"][l], w["ev_w_in"][j], t)
            o, s_new, v_rows = _even_mixer(proj, pos, st["ret"][j], w["ret_gn_g"][j], w["mlp_norm_g"][j],
                                           w["mlp_w_s"][j], w["mlp_b_s"][j], nseg=16)
            out["chunk_v"].append(v_rows.reshape(n_seq, seq_len, G_B, DG_B))
            out["ret"].append(s_new)
            x = _out_proj([o], w["ev_w_out"][j], x, t)
        else:
            proj = _norm_proj(x, w["norm_mix_g"][l], w["od_w_in"][j], t)
            o_c, s_new = _hgrn_mixer(proj, w["hgrn_lb_logits"], w["hgrn_onorm_g"][j], st["hgrn"][j], c=seq_len,
                                     nseq=16, layer=l)
            o_d, k_new, v_new = _swa_sample(proj, pos, jnp.transpose(st["swa_k"][j], (0, 2, 3, 1)),
                                            jnp.transpose(st["swa_v"][j], (0, 2, 3, 1)),
                                            w["swa_qnorm_g"][j], w["swa_knorm_g"][j], w["swa_sinks"][j], nseq=16,
                                            group=8)
            out["hgrn"].append(s_new)
            out["swa_k"].append(jnp.transpose(k_new, (0, 3, 1, 2)))
            out["swa_v"].append(jnp.transpose(v_new, (0, 3, 1, 2)))
            x = _out_proj([o_c, o_d], w["od_w_out"][j], x, t)
        x = _mem_attend_cached(x, w["norm_mem_g"][l], w["mem_w_q"][l], w["mem_qnorm_g"][l],
                               st["mem_k"].reshape(DEPTH, n_seq, N_MEM * H_X, HD_X),
                               st["mem_v"].reshape(DEPTH, n_seq, N_MEM * H_X, HD_X), w["mem_w_o"][l], layer=l, nseq=8)
        c0 = st["conv"][l].reshape(n_seq * (CONV_W - 1), D_FF)
        x, gt = _ffn(x, w["norm_ffn_g"][l], w["ffn_w_gate"][l], w["ffn_w_up"][l], w["ffn_conv_w"][l],
                     w["ffn_conv_b"][l], w["ffn_w_down"][l], tm=512, ffc=256, tiles_per_seq=1, c0=c0)
        out["conv"].append(gt.reshape(n_seq, CONV_W - 1, D_FF))
    return x, {name: jnp.stack(rows) for name, rows in out.items()}


def kernel(x_prompt, x_sample, state_ret, state_hgrn, cache_swa_k, cache_swa_v, cache_mem_k, cache_mem_v,
           state_ffn_conv, mem_prompt, norm_mix_g, norm_mem_g, norm_ffn_g, ev_w_in, ev_w_out, ret_gn_g,
           mlp_norm_g, mlp_w_s, mlp_b_s, od_w_in, od_w_out, hgrn_lb_logits, hgrn_onorm_g, swa_qnorm_g,
           swa_knorm_g, swa_sinks, mem_w_q, mem_w_kv, mem_qnorm_g, mem_knorm_g, mem_w_o, ffn_w_gate,
           ffn_w_up, ffn_conv_w, ffn_conv_b, ffn_w_down):
    bf = lambda a: a.astype(BF16)
    w = dict(norm_mix_g=norm_mix_g, norm_mem_g=norm_mem_g, norm_ffn_g=norm_ffn_g, ev_w_in=bf(ev_w_in),
             ev_w_out=bf(ev_w_out), ret_gn_g=ret_gn_g, mlp_norm_g=mlp_norm_g, mlp_w_s=mlp_w_s, mlp_b_s=mlp_b_s,
             od_w_in=bf(od_w_in), od_w_out=bf(od_w_out), hgrn_lb_logits=hgrn_lb_logits, hgrn_onorm_g=hgrn_onorm_g,
             swa_qnorm_g=swa_qnorm_g, swa_knorm_g=swa_knorm_g, swa_sinks=swa_sinks, mem_w_q=bf(mem_w_q),
             mem_w_kv=bf(mem_w_kv), mem_qnorm_g=mem_qnorm_g, mem_knorm_g=mem_knorm_g, mem_w_o=bf(mem_w_o),
             ffn_w_gate=bf(ffn_w_gate), ffn_w_up=bf(ffn_w_up), ffn_conv_w=ffn_conv_w, ffn_conv_b=ffn_conv_b,
             ffn_w_down=bf(ffn_w_down))
    b, seq, d = x_prompt.shape
    db, dseq, _ = x_sample.shape
    pos_prompt = np.arange(seq)
    pos_sample = PAST_LEN + np.arange(dseq)
    y_p, ns_p = _run_prompt(x_prompt.reshape(b * seq, d), pos_prompt, b, mem_prompt.reshape(b * N_MEM, d), w)
    st = dict(ret=state_ret, hgrn=state_hgrn, swa_k=cache_swa_k, swa_v=cache_swa_v, mem_k=cache_mem_k,
              mem_v=cache_mem_v, conv=state_ffn_conv)
    y_s, ns_s = _run_sample(x_sample.reshape(db * dseq, d), pos_sample, db, st, w)
    return (y_p.reshape(b, seq, d), y_s.reshape(db, dseq, d), ns_p["ret"], ns_p["hgrn"], ns_p["swa_k"], ns_p["swa_v"],
            ns_p["mem_k"], ns_p["mem_v"], ns_p["conv"], ns_s["ret"], ns_s["chunk_v"], ns_s["hgrn"], ns_s["swa_k"],
            ns_s["swa_v"], ns_s["conv"])
```

```python
import functools
import math

import jax
import jax.numpy as jnp
import numpy as np
from jax import lax
from jax.experimental import pallas as pl
from jax.experimental.pallas import tpu as pltpu

F32 = jnp.float32
BF16 = jnp.bfloat16

D_MODEL = 1024
DEPTH = 2
PAST_LEN = 16384
H_A, DV_A, DK_A = 4, 128, 64
RET_CHUNK = 128
RET_THETA = 10000.0
G_B, DG_B = 4, 128
MLP_CHUNK = 128
H_C, DK_C, DV_C = 4, 128, 128
HGRN_CHUNK = 64
HD_D, HQ_D, HKV_D = 64, 8, 2
WINDOW = 128
ROPE_THETA = 500000.0
ROT_DIM_D = HD_D // 4
N_MEM, H_X, HD_X = 256, 4, 128
D_FF = 2816
CONV_W = 3
FFN_DOWN_GROUP = 4
EPS = 1e-6
SQRT_HALF = float(np.sqrt(0.5))
EV_Q, EV_K, EV_V, EV_G, EV_U, EV_VB, EV_END = 0, 256, 512, 1024, 1536, 2048, 2560
OD_Q, OD_F, OD_I, OD_G, OD_QD, OD_KD, OD_VD, OD_END = 0, 512, 1024, 1536, 2048, 2560, 2688, 2816

VMEM_LIMIT_BYTES = 56 * 1024 * 1024


def _cparams(sem):
    return pltpu.CompilerParams(dimension_semantics=sem, vmem_limit_bytes=VMEM_LIMIT_BYTES)


def _resident(shape):
    return pl.BlockSpec(shape, lambda *_: (0,) * len(shape), pipeline_mode=pl.Buffered(1))


class _Layer:
    def __init__(self, array, index):
        self.array, self.index, self.shape = array, index, array.shape[1:]

    def spec(self):
        index, zeros = self.index, (0,) * len(self.shape)
        return pl.BlockSpec((None,) + self.shape, lambda *_: (index,) + zeros, pipeline_mode=pl.Buffered(1))


def _rms(x, g):
    return x * lax.rsqrt(jnp.mean(x * x, axis=-1, keepdims=True) + EPS) * g


def _ln(x, g):
    mu = jnp.mean(x, axis=-1, keepdims=True)
    xc = x - mu
    return xc * lax.rsqrt(jnp.mean(xc * xc, axis=-1, keepdims=True) + EPS) * g


def _sigmoid(x):
    return 1.0 / (1.0 + jnp.exp(-x))


def _silu(x):
    return x * _sigmoid(x)


def _gelu(x):
    return 0.5 * x * (1.0 + lax.erf(x * SQRT_HALF))


def _dot(a, b):
    return jnp.dot(a.astype(BF16), b.astype(BF16), preferred_element_type=F32)


def _dot_nt(a, b):
    return lax.dot_general(a.astype(BF16), b.astype(BF16), (((1,), (1,)), ((), ())), preferred_element_type=F32)


def _dot_tn(a, b):
    return lax.dot_general(a.astype(BF16), b.astype(BF16), (((0,), (0,)), ((), ())), preferred_element_type=F32)


def _split3(x):
    hi = x.astype(BF16)
    r1 = x - hi.astype(F32)
    mid = r1.astype(BF16)
    lo = (r1 - mid.astype(F32)).astype(BF16)
    return hi, mid, lo


def _retention_consts(c, nseq=1):
    lg = np.log1p(-np.exp2(-5.0 - np.arange(H_A, dtype=np.float64)))
    idx = np.arange(c, dtype=np.float64)
    rel = idx[:, None] - idx[None, :]
    dmask = np.where(rel >= 0, np.exp(rel[None] * lg[:, None, None]), 0.0)
    q_dec = np.exp((idx + 1.0)[None, :] * lg[:, None])
    k_dec = np.exp((c - 1.0 - idx)[None, :] * lg[:, None])
    c_dec = np.exp(c * lg)
    qdec = np.tile(np.repeat(q_dec.T, DK_A, axis=1), (nseq, 1))
    kdec = np.tile(np.repeat(k_dec.T, DK_A, axis=1), (nseq, 1))
    dmask = np.stack([np.kron(np.eye(nseq), dmask[h]) for h in range(H_A)])
    cdec = np.broadcast_to(c_dec[:, None, None], (H_A, 8, DV_A))
    return tuple(jnp.asarray(a, F32) for a in (dmask, qdec, kdec, cdec))


def _rope_tables(pos, rot_dim, theta, head_dim, reps, row_reps=1):
    half = rot_dim // 2
    inv = theta ** (-np.arange(half, dtype=np.float64) / half)
    ang = pos.astype(np.float64)[:, None] * inv[None, :]
    cos = np.cos(ang)
    sin = np.sin(ang)
    l = pos.shape[0]
    pad = head_dim - rot_dim
    cos_h = np.concatenate([cos, cos, np.ones((l, pad))], axis=1)
    sin_h = np.concatenate([-sin, sin, np.zeros((l, pad))], axis=1)
    return (jnp.asarray(np.tile(cos_h, (row_reps, reps)), F32), jnp.asarray(np.tile(sin_h, (row_reps, reps)), F32))


def _even_consts(c, gn_g, mn_g, w_s, b_s, nseq=1):
    dmask, qdec, kdec, cdec = _retention_consts(c, nseq)
    bs_full = jnp.tile(jnp.repeat(b_s[:, :c].T, DG_B, axis=1), (nseq, 1))
    arrays = [dmask, qdec, kdec, cdec, gn_g.reshape(1, H_A * DV_A), mn_g.reshape(1, G_B * DG_B),
              jnp.tile(w_s[:, :c, :c], (1, nseq, nseq)), bs_full]
    return arrays, [a.shape for a in arrays]


def _retention_state_batched(qd, kd, v, s0_ref, cdec_ref, nseq):
    n = qd.shape[0]
    own = (lax.broadcasted_iota(jnp.int32, (n, nseq * 128), 1) // 128
           == lax.broadcasted_iota(jnp.int32, (n, nseq * 128), 0) // (n // nseq))
    place = lambda x: jnp.where(own, jnp.concatenate([x] * nseq, axis=1), 0.0)
    inter, s_new = [None] * H_A, [None] * H_A
    for p in range(H_A // 2):
        h0, h1 = 2 * p, 2 * p + 1
        lanes = slice(p * 128, (p + 1) * 128)
        s_a, s_b = s0_ref[:, h0], s0_ref[:, h1]
        z = jnp.zeros_like(s_a)
        slab = jnp.concatenate([jnp.concatenate([s_a, z], axis=2), jnp.concatenate([z, s_b], axis=2)], axis=1)
        res = _dot(place(qd[:, lanes]), slab.reshape(nseq * 2 * DK_A, 2 * DV_A))
        inter[h0], inter[h1] = res[:, :DV_A], res[:, DV_A:]
        upd = _dot_tn(place(kd[:, lanes]), v[:, 2 * p * DV_A:2 * (p + 1) * DV_A]).reshape(nseq, 2 * DK_A, 2 * DV_A)
        s_new[h0] = cdec_ref[h0, 0:1, :] * s_a + upd[:, :DK_A, :DV_A]
        s_new[h1] = cdec_ref[h1, 0:1, :] * s_b + upd[:, DK_A:, DV_A:]
    return inter, s_new


def _even_segment(pj, cos, sin, consts, s_old, nseq=1):
    dmask_ref, qdec_ref, kdec_ref, cdec_ref, gng_ref, mng_ref, ws_ref, bs_ref = consts
    c = pj.shape[0]
    width = H_A * DK_A
    lane = lax.broadcasted_iota(jnp.int32, (c, width), 1)
    first_half = (lane % DK_A) < (DK_A // 2)
    row = lax.broadcasted_iota(jnp.int32, (c, c), 0)
    col = lax.broadcasted_iota(jnp.int32, (c, c), 1)
    tril = (col <= row) & (row // (c // nseq) == col // (c // nseq))

    def rope(x):
        rot = jnp.where(first_half, pltpu.roll(x, width - DK_A // 2, 1), pltpu.roll(x, DK_A // 2, 1))
        return x * cos + rot * sin

    q = rope(pj[:, EV_Q:EV_K])
    k = rope(pj[:, EV_K:EV_V]) * (DK_A ** -0.5)
    v = pj[:, EV_V:EV_G]
    ga = pj[:, EV_G:EV_U]
    qd = q * qdec_ref[...]
    kd = k * kdec_ref[...]
    ks = [slice(h * DK_A, (h + 1) * DK_A) for h in range(H_A)]
    vs = [slice(h * DV_A, (h + 1) * DV_A) for h in range(H_A)]
    qk = [_dot_nt(q[:, ks[h]], k[:, ks[h]]) for h in range(H_A)]
    if nseq == 1:
        inter = [_dot(qd[:, ks[h]], s_old[h]) for h in range(H_A)]
        s_new = [cdec_ref[h, 0:1, :] * s_old[h] + _dot_tn(kd[:, ks[h]], v[:, vs[h]]) for h in range(H_A)]
    else:
        inter, s_new = _retention_state_batched(qd, kd, v, s_old, cdec_ref, nseq)
    vg = _gelu(pj[:, EV_VB:EV_END])
    vrows = [_ln(vg[:, gs], mng_ref[:, gs]) for gs in vs]
    mixed = [_dot(jnp.where(tril, ws_ref[g], 0.0), vrows[g]) for g in range(G_B)]
    yield s_new
    outs = []
    for h in range(H_A):
        o = _dot(qk[h] * dmask_ref[h], v[:, vs[h]]) + inter[h]
        outs.append(_silu(ga[:, vs[h]]) * _ln(o, gng_ref[:, vs[h]]))
    u = _gelu(pj[:, EV_U:EV_VB])
    for g, gs in enumerate(vs):
        outs.append(u[:, gs] * (mixed[g] + bs_ref[:, gs]))
    yield jnp.concatenate(outs, axis=1), s_new, jnp.concatenate(vrows, axis=1)


def _even_body(x_ref, ng_ref, win_ref, wout_ref, cos_ref, sin_ref, dmask_ref, qdec_ref, kdec_ref, cdec_ref, gng_ref,
               mng_ref, ws_ref, bs_ref, s0_ref, y_ref, sout_ref, vrows_ref, *, nseg):
    consts = (dmask_ref, qdec_ref, kdec_ref, cdec_ref, gng_ref, mng_ref, ws_ref, bs_ref)
    x = x_ref[...]
    pj = jnp.dot(_rms(x, ng_ref[...]).astype(BF16), win_ref[...], preferred_element_type=F32)
    (o, s_new, vrows), = _interleave(_even_segment(pj, cos_ref[...], sin_ref[...], consts, s0_ref, nseq=nseg))
    for h in range(H_A):
        sout_ref[:, h] = s_new[h]
    y_ref[...] = x + jnp.dot(o.astype(BF16), wout_ref[...], preferred_element_type=F32)
    vrows_ref[...] = vrows


def _even_mixer(x, pos, ng, w_in, w_out, s0, gn_g, mn_g, w_s, b_s, *, nseg):
    t, d = x.shape
    c = pos.shape[0]
    r = nseg * c
    consts, shapes = _even_consts(c, gn_g, mn_g, w_s, b_s, nseq=nseg)
    cos, sin = _rope_tables(pos, DK_A, RET_THETA, DK_A, H_A, row_reps=nseg)
    rowb = pl.BlockSpec((r, d), lambda i: (i, 0))
    st_spec = pl.BlockSpec((nseg, H_A, DK_A, DV_A), lambda i: (i, 0, 0, 0))
    return pl.pallas_call(
        functools.partial(_even_body, nseg=nseg),
        grid=(t // r,),
        in_specs=[rowb, _resident((1, d)), w_in.spec(), w_out.spec(), _resident((r, 256)), _resident((r, 256))]
        + [_resident(s) for s in shapes] + [st_spec],
        out_specs=[rowb, st_spec, pl.BlockSpec((r, 512), lambda i: (i, 0))],
        out_shape=[jax.ShapeDtypeStruct((t, d), F32), jax.ShapeDtypeStruct(s0.shape, F32),
                   jax.ShapeDtypeStruct((t, 512), F32)],
        compiler_params=_cparams(("arbitrary",)),
        name="even_mixer",
    )(x, ng.reshape(1, d), w_in.array, w_out.array, cos, sin, *consts, s0)


def _even_layer_body(x_ref, ng_ref, win_ref, wout_ref, cos_ref, sin_ref, dmask_ref, qdec_ref, kdec_ref, cdec_ref,
                     gng_ref, mng_ref, ws_ref, bs_ref, y_ref, sout_ref, st_ref, *, c, pb):
    step = pl.program_id(1)

    @pl.when(step == 0)
    def _():
        st_ref[...] = jnp.zeros_like(st_ref)

    consts = (dmask_ref, qdec_ref, kdec_ref, cdec_ref, gng_ref, mng_ref, ws_ref, bs_ref)
    x = x_ref[...]
    h = _rms(x, ng_ref[...]).astype(BF16)
    ngrp = x.shape[0] // pb
    project = lambda g: jnp.dot(h[g * pb:(g + 1) * pb, :], win_ref[...], preferred_element_type=F32)
    proj = project(0)
    state = [st_ref[hh] for hh in range(H_A)]
    pending = None

    def finish(rows, seg):
        o, _, _ = next(seg)
        y_ref[rows, :] = x[rows, :] + jnp.dot(o.astype(BF16), wout_ref[...], preferred_element_type=F32)

    for g in range(ngrp):
        nxt = project(g + 1) if g + 1 < ngrp else None
        for j in range(pb // c):
            rows = slice(g * pb + j * c, g * pb + (j + 1) * c)
            seg = _even_segment(proj[j * c:(j + 1) * c, :], cos_ref[rows, :], sin_ref[rows, :], consts, state)
            state = next(seg)
            if pending is not None:
                finish(*pending)
            pending = (rows, seg)
        proj = nxt
    finish(*pending)
    for hh in range(H_A):
        st_ref[hh] = state[hh]

    @pl.when(step == pl.num_programs(1) - 1)
    def _():
        sout_ref[0] = st_ref[...]


def _even_layer(x, pos, ng, w_in, w_out, gn_g, mn_g, w_s, b_s, *, n_seq, r, pb):
    t, d = x.shape
    seq_len = pos.shape[0]
    c = math.gcd(seq_len, RET_CHUNK)
    assert c == min(MLP_CHUNK, seq_len) and pb % c == 0 and r % pb == 0 and seq_len % r == 0
    steps = seq_len // r
    consts, shapes = _even_consts(c, gn_g, mn_g, w_s, b_s)
    cos, sin = _rope_tables(pos, DK_A, RET_THETA, DK_A, H_A)
    rowb = pl.BlockSpec((r, d), lambda b, i: (b * steps + i, 0))
    tab = pl.BlockSpec((r, 256), lambda b, i: (i, 0))
    st_spec = pl.BlockSpec((1, H_A, DK_A, DV_A), lambda b, i: (b, 0, 0, 0))
    return pl.pallas_call(
        functools.partial(_even_layer_body, c=c, pb=pb),
        grid=(n_seq, steps),
        in_specs=[rowb, _resident((1, d)), w_in.spec(), w_out.spec(), tab, tab]
        + [_resident(s) for s in shapes],
        out_specs=[rowb, st_spec],
        out_shape=[jax.ShapeDtypeStruct((t, d), F32), jax.ShapeDtypeStruct((n_seq, H_A, DK_A, DV_A), F32)],
        scratch_shapes=[pltpu.VMEM((H_A, DK_A, DV_A), F32)],
        compiler_params=_cparams(("arbitrary", "arbitrary")),
        name="even_layer",
    )(x, ng.reshape(1, d), w_in.array, w_out.array, cos, sin, *consts)


def _hgrn_lower_bound(lbl, layer):
    e = jnp.exp(lbl - jnp.max(lbl, axis=0, keepdims=True))
    sm = e / jnp.sum(e, axis=0, keepdims=True)
    acc = sm[0:1, :]
    first = acc
    for i in range(1, layer + 1):
        acc = acc + sm[i:i + 1, :]
    return acc - first


def _block_row(x, blk, j):
    n, lanes = x.shape
    if blk % 16 == 0:
        return jnp.concatenate(
            [jnp.broadcast_to(x[i * blk + j:i * blk + j + 1, :], (blk, lanes)) for i in range(n // blk)], axis=0)
    x3 = x.reshape(n // 8, 8, lanes)
    pick = lambda jj: jnp.broadcast_to(x3[:, jj:jj + 1, :], x3.shape).reshape(n, lanes)
    if blk == 8:
        return pick(j)
    assert blk == 4
    upper = (lax.broadcasted_iota(jnp.int32, (n, 1), 0) % 8) >= 4
    return jnp.where(upper, pick(4 + j), pick(j))


def _hgrn_gates(pj, lb):
    f = lb + (1.0 - lb) * _sigmoid(pj[:, OD_F:OD_I])
    return f, jnp.log(f), 1.0 - f, _silu(pj[:, OD_Q:OD_F]), pj[:, OD_I:OD_G], pj[:, OD_G:OD_QD]


def _chunk_cumsum(g, c):
    n = g.shape[0]
    row = lax.broadcasted_iota(jnp.int32, (n, n), 0)
    col = lax.broadcasted_iota(jnp.int32, (n, n), 1)
    tri = ((col <= row) & (row // c == col // c)).astype(BF16)
    return sum(jnp.dot(tri, piece, preferred_element_type=F32) for piece in _split3(g))


def _hgrn_intra(q, k, f, b, c):
    return _hgrn_assemble(_hgrn_level_products(q, k, f, b, c))


def _hgrn_level_products(q, k, f, b, c):
    n = q.shape[0]
    rloc = lax.broadcasted_iota(jnp.int32, (n, 1), 0)
    diag = jnp.sum(q * k, axis=-1, keepdims=True)
    prods = []
    blk = 2
    while blk <= c:
        second = (rloc % blk) >= blk // 2
        if blk == 2:
            e = f
            ke = k
        else:
            dq = b - _block_row(b, blk, blk // 2 - 1)
            e = jnp.exp(jnp.where(second, dq, -dq))
            ke = k * e
        prods.append((blk, _dot_nt(jnp.where(second, q * e, 0.0), jnp.where(second, 0.0, ke))))
        blk *= 2
    return diag, prods


def _hgrn_assemble(level_products):
    diag, prods = level_products
    n = diag.shape[0]
    row = lax.broadcasted_iota(jnp.int32, (n, n), 0)
    col = lax.broadcasted_iota(jnp.int32, (n, n), 1)
    a = jnp.where(row == col, diag, 0.0)
    for blk, p in prods:
        a = a + jnp.where((row // blk) == (col // blk), p, 0.0)
    return a


def _interleave(*gens):
    results = [None] * len(gens)
    live = list(enumerate(gens))
    while live:
        still = []
        for i, gen in live:
            try:
                results[i] = next(gen)
                still.append((i, gen))
            except StopIteration:
                pass
        live = still
    return results


def _hgrn_tile(pj, lb, ong_ref, st_ref, c):
    n = pj.shape[0]
    heads = [slice(h * DK_C, (h + 1) * DK_C) for h in range(H_C)]
    chunks = [slice(j * c, (j + 1) * c) for j in range(n // c)]
    f, g, k, q, v, gc = _hgrn_gates(pj, lb)
    b = _chunk_cumsum(g, c)
    qe = q * jnp.exp(b)
    kd = k * jnp.exp(_block_row(b, c, c - 1) - b)
    levels = [_hgrn_level_products(q[:, hs], k[:, hs], f[:, hs], b[:, hs], c) for hs in heads]
    kv = [[_dot_tn(kd[rs, hs], v[rs, hs]) for rs in chunks] for hs in heads]
    yield None
    o_intra = [_dot(_hgrn_assemble(levels[h]), v[:, hs]) for h, hs in enumerate(heads)]
    yield None
    outs = []
    for h, hs in enumerate(heads):
        s = st_ref[h]
        parts = []
        for j, rs in enumerate(chunks):
            parts.append(o_intra[h][rs, :] + _dot(qe[rs, hs], s))
            bl = b[rs.stop - 1:rs.stop, hs]
            decay = jnp.broadcast_to(jnp.exp(bl), (DK_C, DK_C)).T
            s = decay * s + kv[h][j]
        st_ref[h] = s
        outs.append(_rms(jnp.concatenate(parts, axis=0), ong_ref[:, hs]) * _silu(gc[:, hs]))
    yield jnp.concatenate(outs, axis=1)


def _hgrn_body(x_ref, ng_ref, win_ref, wout_ref, lbl_ref, ong_ref, s0_ref, y_ref, sout_ref, *, c, layer):
    x = x_ref[...]
    pj = jnp.dot(_rms(x, ng_ref[...]).astype(BF16), win_ref[:, OD_Q:OD_QD], preferred_element_type=F32)
    n = pj.shape[0]
    ns = n // c
    lb = _hgrn_lower_bound(lbl_ref[...], layer)
    f, g, k, q, v, gc = _hgrn_gates(pj, lb)
    b = _chunk_cumsum(g, c)
    blast = _block_row(b, c, c - 1)
    qe = q * jnp.exp(b)
    kd = k * jnp.exp(blast - b)
    own_cols = (lax.broadcasted_iota(jnp.int32, (n, ns * DK_C), 1) // DK_C
                == lax.broadcasted_iota(jnp.int32, (n, ns * DK_C), 0) // c)
    place = lambda x: jnp.where(own_cols, jnp.concatenate([x] * ns, axis=1), 0.0)
    seq_cols = (lax.broadcasted_iota(jnp.int32, (ns, ns * DK_C), 1) // DK_C
                == lax.broadcasted_iota(jnp.int32, (ns, ns * DK_C), 0))
    ones = jnp.ones((ns, DV_C), BF16)
    last_row = (lax.broadcasted_iota(jnp.int32, (ns, n), 1)
                == lax.broadcasted_iota(jnp.int32, (ns, n), 0) * c + (c - 1)).astype(BF16)
    b_last = sum(jnp.dot(last_row, piece, preferred_element_type=F32) for piece in _split3(b))
    outs = []
    for h in range(H_C):
        hs = slice(h * DK_C, (h + 1) * DK_C)
        vh = v[:, hs]
        s = s0_ref[:, h].reshape(ns * DK_C, DV_C)
        o = _dot(_hgrn_intra(q[:, hs], k[:, hs], f[:, hs], b[:, hs], c), vh) + _dot(place(qe[:, hs]), s)
        e_last = jnp.exp(b_last[:, hs])
        e_placed = jnp.where(seq_cols, jnp.concatenate([e_last] * ns, axis=1), 0.0)
        decay = sum(lax.dot_general(piece, ones, (((0,), (0,)), ((), ())), preferred_element_type=F32)
                    for piece in _split3(e_placed))
        s_new = decay * s + _dot_tn(place(kd[:, hs]), vh)
        sout_ref[:, h] = s_new.reshape(ns, DK_C, DV_C)
        outs.append(_rms(o, ong_ref[:, hs]) * _silu(gc[:, hs]))
    o_c = jnp.concatenate(outs, axis=1).astype(BF16)
    y_ref[...] = x + jnp.dot(o_c, wout_ref[0:H_C * DV_C, :], preferred_element_type=F32)


def _hgrn_mixer(x, ng, w_in, w_out, lb_logits, on_g, s0, *, c, nseq, layer):
    t, d = x.shape
    r = nseq * c
    rowb = pl.BlockSpec((r, d), lambda i: (i, 0))
    st_spec = pl.BlockSpec((nseq, H_C, DK_C, DV_C), lambda i: (i, 0, 0, 0))
    return pl.pallas_call(
        functools.partial(_hgrn_body, c=c, layer=layer),
        grid=(t // r,),
        in_specs=[rowb, _resident((1, d)), w_in.spec(), w_out.spec(), _resident((DEPTH, 512)), _resident((1, 512)),
                  st_spec],
        out_specs=[rowb, st_spec],
        out_shape=[jax.ShapeDtypeStruct((t, d), F32), jax.ShapeDtypeStruct(s0.shape, F32)],
        compiler_params=_cparams(("arbitrary",)),
        name="hgrn_mixer",
    )(x, ng.reshape(1, d), w_in.array, w_out.array, lb_logits, on_g.reshape(1, 512), s0)


def _head_norm(x, g):
    rows, width = x.shape
    lo = lax.broadcasted_iota(jnp.int32, (rows, 128), 1) < HD_D
    outs = []
    for t in range(width // 128):
        xt = x[:, t * 128:(t + 1) * 128]
        sq = xt * xt
        ss_lo = jnp.sum(jnp.where(lo, sq, 0.0), axis=-1, keepdims=True)
        ss_hi = jnp.sum(jnp.where(lo, 0.0, sq), axis=-1, keepdims=True)
        scale = jnp.where(lo, lax.rsqrt(ss_lo * (1.0 / HD_D) + EPS), lax.rsqrt(ss_hi * (1.0 / HD_D) + EPS))
        outs.append(xt * scale)
    return jnp.concatenate(outs, axis=1) * g


def _rope_partial(x, cos, sin):
    width = x.shape[1]
    half = ROT_DIM_D // 2
    lane = lax.broadcasted_iota(jnp.int32, x.shape, 1)
    rot = jnp.where((lane % HD_D) < half, pltpu.roll(x, width - half, 1), pltpu.roll(x, half, 1))
    return x * cos + rot * sin


def _swa_prep(q, k, cos, sin, qng_ref, kng_ref):
    qn = _rope_partial(_head_norm(q, qng_ref[...]), jnp.concatenate([cos] * (HQ_D // 2), axis=1),
                       jnp.concatenate([sin] * (HQ_D // 2), axis=1)).astype(BF16)
    return qn, _rope_partial(_head_norm(k, kng_ref[...]), cos, sin)


def _swa_place(x):
    lo = lax.broadcasted_iota(jnp.int32, x.shape, 1) < HD_D
    sw = pltpu.roll(x, HD_D, 1)
    return [[jnp.where(lo, x, 0.0).astype(BF16), jnp.where(lo, 0.0, sw).astype(BF16)],
            [jnp.where(lo, sw, 0.0).astype(BF16), jnp.where(lo, 0.0, x).astype(BF16)]]


def _swa_block(qn, k_prev, k_cur, v_prev, v_cur, sink_ref, allow_prev):
    rep = HQ_D // HKV_D
    lo = lax.broadcasted_iota(jnp.int32, (WINDOW, 128), 1) < HD_D
    qi = lax.broadcasted_iota(jnp.int32, (WINDOW, 2 * WINDOW), 0)
    ci = lax.broadcasted_iota(jnp.int32, (WINDOW, 2 * WINDOW), 1)
    cur = (ci >= WINDOW) & ((ci - WINDOW) <= qi)
    prev = (ci < WINDOW) & (ci > qi)
    mask = prev | cur if allow_prev is True else (prev & allow_prev) | cur
    keys = lambda g, half: jnp.concatenate([k_prev[g][half], k_cur[g][half]], axis=0)
    ones = [jnp.where(lo, 1.0, 0.0).astype(BF16), jnp.where(lo, 0.0, 1.0).astype(BF16)]
    vals = lambda g, half: jnp.concatenate(
        [jnp.concatenate([v_prev[g][half], v_cur[g][half]], axis=0), jnp.concatenate([ones[half]] * 2, axis=0)], axis=1)
    scores = [lax.dot_general(qn[:, (h // 2) * 128:(h // 2 + 1) * 128], keys(h // rep, h % 2),
                              (((1,), (1,)), ((), ())), preferred_element_type=F32) for h in range(HQ_D)]
    yield None
    pv, sink_term = [], []
    for h in range(HQ_D):
        s = jnp.where(mask, scores[h] * (HD_D ** -0.5), -jnp.inf)
        sink = sink_ref[h:h + 1, 0:1]
        m = jnp.maximum(jnp.max(s, axis=-1, keepdims=True), sink)
        pv.append(jnp.dot(jnp.exp(s - m).astype(BF16), vals(h // rep, h % 2), preferred_element_type=F32))
        sink_term.append(jnp.exp(sink - m))
    yield None
    tiles = []
    for t in range(HQ_D // 2):
        both = pv[2 * t] + pv[2 * t + 1]
        den = both[:, 128:] + jnp.where(lo, sink_term[2 * t], sink_term[2 * t + 1])
        tiles.append(both[:, :128] / den)
    yield jnp.concatenate(tiles, axis=1)


def _swa_sample_chain(q2, kt_cache, k_new, vt_cache, v_new, sinks2, half, l):
    rows = q2.shape[0]
    ns = rows // (2 * l)
    lo = lax.broadcasted_iota(jnp.int32, (1, 128), 1) < HD_D
    ones = jnp.where(lo, 1.0, 0.0) if half == 0 else jnp.where(lo, 0.0, 1.0)
    ext = lambda v: jnp.concatenate([v, jnp.broadcast_to(ones, v.shape).astype(BF16)], axis=1)
    nt = lambda a, b: lax.dot_general(a, b, (((1,), (1,)), ((), ())), preferred_element_type=F32)
    zeros = jnp.zeros((HD_D, kt_cache.shape[1]), BF16)
    band = lambda x: jnp.concatenate([x, zeros] if half == 0 else [zeros, x], axis=0)
    k_cache = band(kt_cache.astype(BF16))
    v_cache = jnp.concatenate([band(vt_cache.astype(BF16)), band(jnp.ones_like(zeros))], axis=0)
    s1 = jnp.dot(q2, k_cache, preferred_element_type=F32)
    s2 = nt(q2, k_new)
    yield None
    r1 = lax.broadcasted_iota(jnp.int32, s1.shape, 0)
    c1 = lax.broadcasted_iota(jnp.int32, s1.shape, 1)
    ok1 = ((r1 % (ns * l)) // l == c1 // WINDOW) & (c1 % WINDOW > r1 % l)
    r2 = lax.broadcasted_iota(jnp.int32, s2.shape, 0)
    c2 = lax.broadcasted_iota(jnp.int32, s2.shape, 1)
    ok2 = ((r2 % (ns * l)) // l == c2 // l) & (c2 % l <= r2 % l)
    s1 = jnp.where(ok1, s1 * (HD_D ** -0.5), -jnp.inf)
    s2 = jnp.where(ok2, s2 * (HD_D ** -0.5), -jnp.inf)
    sink = jnp.where(lax.broadcasted_iota(jnp.int32, (rows, 1), 0) < ns * l, sinks2[0], sinks2[1])
    m = jnp.maximum(jnp.maximum(jnp.max(s1, axis=-1, keepdims=True), jnp.max(s2, axis=-1, keepdims=True)), sink)
    res = (nt(jnp.exp(s1 - m).astype(BF16), v_cache)
           + jnp.dot(jnp.exp(s2 - m).astype(BF16), ext(v_new), preferred_element_type=F32))
    yield res, jnp.exp(sink - m)


def _swa_sample_body(x_ref, res_ref, ng_ref, win_ref, wout_ref, kc_ref, vc_ref, cos_ref, sin_ref, qng_ref, kng_ref,
                     sink_ref, y_ref, kout_ref, vout_ref, o_ref, *, nseq, l, group):
    rep = HQ_D // HKV_D
    pj = jnp.dot(_rms(x_ref[...], ng_ref[...]).astype(BF16), win_ref[:, OD_QD:OD_END], preferred_element_type=F32)
    qn, kn = _swa_prep(pj[:, 0:OD_KD - OD_QD], pj[:, OD_KD - OD_QD:OD_VD - OD_QD], cos_ref[...], sin_ref[...],
                       qng_ref, kng_ref)
    v = pj[:, OD_VD - OD_QD:]
    lo = lax.broadcasted_iota(jnp.int32, (group * l, 128), 1) < HD_D
    r_sel = lax.broadcasted_iota(jnp.int32, (group * l, group * WINDOW), 0)
    c_sel = lax.broadcasted_iota(jnp.int32, (group * l, group * WINDOW), 1)
    append = ((c_sel // WINDOW == r_sel // l) & (c_sel % WINDOW == WINDOW - l + r_sel % l)).astype(BF16)
    is_new = lax.broadcasted_iota(jnp.int32, (HKV_D * HD_D, group * WINDOW), 1) % WINDOW >= WINDOW - l

    def shifted(old_t, new_rows):
        moved = sum(lax.dot_general(piece, append, (((0,), (0,)), ((), ())), preferred_element_type=F32)
                    for piece in _split3(new_rows))
        return jnp.where(is_new, moved, pltpu.roll(old_t, group * WINDOW - l, 1))

    chains, where = [], []
    for gi in range(nseq // group):
        rs = slice(gi * group * l, (gi + 1) * group * l)
        seqs = range(gi * group, (gi + 1) * group)
        kt = [jnp.concatenate([kc_ref[s, g] for s in seqs], axis=1) for g in range(HKV_D)]
        vt = [jnp.concatenate([vc_ref[s, g] for s in seqs], axis=1) for g in range(HKV_D)]
        k_new, v_new = _swa_place(kn[rs, :]), _swa_place(v[rs, :])
        for g in range(HKV_D):
            tiles = [(g * rep) // 2, (g * rep) // 2 + 1]
            q2 = jnp.concatenate([qn[rs, t * 128:(t + 1) * 128] for t in tiles], axis=0)
            for half in range(2):
                sinks2 = [sink_ref[2 * t + half:2 * t + half + 1, 0:1] for t in tiles]
                chains.append(_swa_sample_chain(q2, kt[g], k_new[g][half], vt[g], v_new[g][half], sinks2, half, l))
                where.append((rs, tiles, half))
        k_all = shifted(jnp.concatenate(kt, axis=0), kn[rs, :])
        v_all = shifted(jnp.concatenate(vt, axis=0), v[rs, :])
        for i, s in enumerate(seqs):
            for g in range(HKV_D):
                kout_ref[s, g] = k_all[g * HD_D:(g + 1) * HD_D, i * WINDOW:(i + 1) * WINDOW]
                vout_ref[s, g] = v_all[g * HD_D:(g + 1) * HD_D, i * WINDOW:(i + 1) * WINDOW]
    results = _interleave(*chains)
    for i in range(0, len(chains), 2):
        (rs, tiles, _), (res0, st0), (res1, st1) = where[i], results[i], results[i + 1]
        both = res0 + res1
        for j, t in enumerate(tiles):
            js = slice(j * group * l, (j + 1) * group * l)
            den = both[js, 128:] + jnp.where(lo, st0[js, :], st1[js, :])
            o_ref[rs, t * 128:(t + 1) * 128] = both[js, :128] / den
    y_ref[...] = res_ref[...] + jnp.dot(o_ref[...].astype(BF16), wout_ref[H_C * DV_C:, :], preferred_element_type=F32)


def _swa_gains(qn_g, kn_g):
    return (jnp.tile(qn_g, HQ_D).reshape(1, HQ_D * HD_D), jnp.tile(kn_g, HKV_D).reshape(1, HKV_D * HD_D))


def _swa_sample(x, res, pos, ng, w_in, w_out, kcache, vcache, qn_g, kn_g, sinks, *, nseq, group):
    t, d = x.shape
    l = pos.shape[0]
    r = nseq * l
    cos, sin = _rope_tables(pos, ROT_DIM_D, ROPE_THETA, HD_D, 2, row_reps=nseq)
    qng, kng = _swa_gains(qn_g, kn_g)
    rowb = pl.BlockSpec((r, d), lambda i: (i, 0))
    cache = pl.BlockSpec((nseq, HKV_D, HD_D, WINDOW), lambda i: (i, 0, 0, 0))
    return pl.pallas_call(
        functools.partial(_swa_sample_body, nseq=nseq, l=l, group=group),
        grid=(t // r,),
        in_specs=[rowb, rowb, _resident((1, d)), w_in.spec(), w_out.spec(), cache, cache, _resident((r, 128)),
                  _resident((r, 128)), _resident((1, 512)), _resident((1, 128)), _resident((HQ_D, 128))],
        out_specs=[rowb, cache, cache],
        out_shape=[jax.ShapeDtypeStruct((t, d), F32), jax.ShapeDtypeStruct(kcache.shape, F32),
                   jax.ShapeDtypeStruct(vcache.shape, F32)],
        scratch_shapes=[pltpu.VMEM((r, HQ_D * HD_D), F32)],
        compiler_params=_cparams(("arbitrary",)),
        name="swa_sample",
    )(x, res, ng.reshape(1, d), w_in.array, w_out.array, kcache, vcache, cos, sin, qng, kng,
      jnp.broadcast_to(sinks[:, None], (HQ_D, 128)))


def _odd_layer_body(x_ref, ng_ref, win_ref, wout_ref, lbl_ref, ong_ref, cos_ref, sin_ref, qng_ref, kng_ref, sink_ref,
                    y_ref, sout_ref, kout_ref, vout_ref, st_ref, kprev_ref, vprev_ref, *, c, pb, layer):
    step = pl.program_id(1)

    @pl.when(step == 0)
    def _():
        st_ref[...] = jnp.zeros_like(st_ref)
        kprev_ref[...] = jnp.zeros_like(kprev_ref)
        vprev_ref[...] = jnp.zeros_like(vprev_ref)

    x = x_ref[...]
    h = _rms(x, ng_ref[...]).astype(BF16)
    lb = _hgrn_lower_bound(lbl_ref[...], layer)
    ngrp = x.shape[0] // pb
    project = lambda g: jnp.dot(h[g * pb:(g + 1) * pb, :], win_ref[...], preferred_element_type=F32)
    proj = project(0)
    k_last, v_last = kprev_ref[...], vprev_ref[...]
    k_prev, v_prev = _swa_place(k_last), _swa_place(v_last)
    for g in range(ngrp):
        nxt = project(g + 1) if g + 1 < ngrp else None
        grows = slice(g * pb, (g + 1) * pb)
        qn, kn = _swa_prep(proj[:, OD_QD:OD_KD], proj[:, OD_KD:OD_VD], cos_ref[grows, :], sin_ref[grows, :],
                           qng_ref, kng_ref)
        staged = []
        for u in range(pb // WINDOW):
            us = slice(u * WINDOW, (u + 1) * WINDOW)
            k_last, v_last = kn[us, :], proj[us, OD_VD:OD_END]
            k_cur, v_cur = _swa_place(k_last), _swa_place(v_last)
            staged.append(_hgrn_tile(proj[us, :], lb, ong_ref, st_ref, c))
            staged.append(_swa_block(qn[us, :], k_prev, k_cur, v_prev, v_cur, sink_ref,
                                     True if g + u > 0 else step > 0))
            k_prev, v_prev = k_cur, v_cur
        outs = _interleave(*staged)
        o = jnp.concatenate([jnp.concatenate(outs[0::2], axis=0), jnp.concatenate(outs[1::2], axis=0)], axis=1)
        y_ref[grows, :] = x[grows, :] + jnp.dot(o.astype(BF16), wout_ref[...], preferred_element_type=F32)
        proj = nxt
    kprev_ref[...] = k_last
    vprev_ref[...] = v_last
    kout_ref[0] = k_last
    vout_ref[0] = v_last

    @pl.when(step == pl.num_programs(1) - 1)
    def _():
        sout_ref[0] = st_ref[...]


def _odd_layer(x, pos, ng, w_in, w_out, lb_logits, on_g, qn_g, kn_g, sinks, *, n_seq, r, pb, layer):
    t, d = x.shape
    seq_len = pos.shape[0]
    c = math.gcd(seq_len, HGRN_CHUNK)
    assert WINDOW % c == 0 and pb % WINDOW == 0 and r % pb == 0 and seq_len % r == 0
    steps = seq_len // r
    cos, sin = _rope_tables(pos, ROT_DIM_D, ROPE_THETA, HD_D, 2)
    qng, kng = _swa_gains(qn_g, kn_g)
    rowb = pl.BlockSpec((r, d), lambda b, i: (b * steps + i, 0))
    tab = pl.BlockSpec((r, 128), lambda b, i: (i, 0))
    st_spec = pl.BlockSpec((1, H_C, DK_C, DV_C), lambda b, i: (b, 0, 0, 0))
    kv_out = pl.BlockSpec((1, WINDOW, 128), lambda b, i: (b, 0, 0))
    return pl.pallas_call(
        functools.partial(_odd_layer_body, c=c, pb=pb, layer=layer),
        grid=(n_seq, steps),
        in_specs=[rowb, _resident((1, d)), w_in.spec(), w_out.spec(), _resident((DEPTH, 512)),
                  _resident((1, 512)), tab, tab, _resident((1, 512)), _resident((1, 128)), _resident((HQ_D, 128))],
        out_specs=[rowb, st_spec, kv_out, kv_out],
        out_shape=[jax.ShapeDtypeStruct((t, d), F32), jax.ShapeDtypeStruct((n_seq, H_C, DK_C, DV_C), F32),
                   jax.ShapeDtypeStruct((n_seq, WINDOW, 128), F32), jax.ShapeDtypeStruct((n_seq, WINDOW, 128), F32)],
        scratch_shapes=[pltpu.VMEM((H_C, DK_C, DV_C), F32), pltpu.VMEM((WINDOW, 128), F32),
                        pltpu.VMEM((WINDOW, 128), F32)],
        compiler_params=_cparams(("arbitrary", "arbitrary")),
        name="odd_layer",
    )(x, ng.reshape(1, d), w_in.array, w_out.array, lb_logits, on_g.reshape(1, 512), cos, sin, qng, kng,
      jnp.broadcast_to(sinks[:, None], (HQ_D, 128)))


def _mem_kv_body(m_ref, w_ref, g_ref, k_ref, v_ref):
    kv = jnp.dot(m_ref[...].astype(BF16), w_ref[...], preferred_element_type=F32)
    hw = H_X * HD_X
    for h in range(H_X):
        hs = slice(h * HD_X, (h + 1) * HD_X)
        k_ref[:, hs] = _rms(kv[:, hs], g_ref[...])
    v_ref[...] = kv[:, hw:]


def _mem_kv(mem, w, g):
    t, d = mem.shape
    hw = H_X * HD_X
    whole = lambda shape: pl.BlockSpec(shape, lambda i: (0,) * len(shape))
    return pl.pallas_call(
        _mem_kv_body,
        grid=(1,),
        in_specs=[whole((t, d)), w.spec(), whole((1, HD_X))],
        out_specs=[whole((t, hw)), whole((t, hw))],
        out_shape=[jax.ShapeDtypeStruct((t, hw), F32), jax.ShapeDtypeStruct((t, hw), F32)],
        compiler_params=_cparams(("arbitrary",)),
        name="mem_kv",
    )(mem, w.array, g.reshape(1, HD_X))


def _mem_attend_rows(x, g_ref, wq_ref, qng_ref, mk_ref, mv_ref, wo_ref):
    heads = [slice(h * HD_X, (h + 1) * HD_X) for h in range(H_X)]
    q = jnp.dot(_rms(x, g_ref[...]).astype(BF16), wq_ref[...], preferred_element_type=F32)
    yield None
    scores = [_dot_nt(_rms(q[:, hs], qng_ref[...]), mk_ref[0, :, hs]) for hs in heads]
    yield None
    outs = []
    for hs, sc in zip(heads, scores):
        sc = sc * (HD_X ** -0.5)
        p = jnp.exp(sc - jnp.max(sc, axis=-1, keepdims=True))
        outs.append(_dot(p, mv_ref[0, :, hs]) / jnp.sum(p, axis=-1, keepdims=True))
    yield None
    yield x + jnp.dot(jnp.concatenate(outs, axis=1).astype(BF16), wo_ref[...], preferred_element_type=F32)


def _mem_attend_body(x_ref, g_ref, wq_ref, qng_ref, mk_ref, mv_ref, wo_ref, y_ref, *, nsplit):
    rows = x_ref.shape[0] // nsplit
    parts = _interleave(*[_mem_attend_rows(x_ref[i * rows:(i + 1) * rows, :], g_ref, wq_ref, qng_ref, mk_ref, mv_ref,
                                           wo_ref) for i in range(nsplit)])
    for i, y in enumerate(parts):
        y_ref[i * rows:(i + 1) * rows, :] = y


def _mem_attend(x, g, wq, qn_g, mk, mv, wo, *, tm, nsplit, tiles_per_mem):
    t, d = x.shape
    hw = H_X * HD_X
    mem_spec = pl.BlockSpec((1, N_MEM, hw), lambda i: (i // tiles_per_mem, 0, 0))
    return pl.pallas_call(
        functools.partial(_mem_attend_body, nsplit=nsplit),
        grid=(t // tm,),
        in_specs=[pl.BlockSpec((tm, d), lambda i: (i, 0)), _resident((1, d)), wq.spec(),
                  _resident((1, HD_X)), mem_spec, mem_spec, wo.spec()],
        out_specs=pl.BlockSpec((tm, d), lambda i: (i, 0)),
        out_shape=jax.ShapeDtypeStruct((t, d), F32),
        compiler_params=_cparams(("arbitrary",)),
        name="mem_attend",
    )(x, g.reshape(1, d), wq.array, qn_g.reshape(1, HD_X), mk, mv, wo.array)


def _mem_attend_cached_body(x_ref, g_ref, wq_ref, qng_ref, mk_ref, mv_ref, wo_ref, y_ref, att_ref, *, nseq):
    x = x_ref[...]
    q = jnp.dot(_rms(x, g_ref[...]).astype(BF16), wq_ref[...], preferred_element_type=F32)
    l = x.shape[0] // nseq
    rows_all = H_X * l
    cols = N_MEM * H_X
    own = (lax.broadcasted_iota(jnp.int32, (rows_all, cols), 1) % H_X
           == lax.broadcasted_iota(jnp.int32, (rows_all, cols), 0) // l)
    scores = []
    for s in range(nseq):
        rs = slice(s * l, (s + 1) * l)
        qs = jnp.concatenate([_rms(q[rs, h * HD_X:(h + 1) * HD_X], qng_ref[...]) for h in range(H_X)], axis=0)
        scores.append(_dot_nt(qs, mk_ref[0, s]))
    for s in range(nseq):
        rs = slice(s * l, (s + 1) * l)
        sc = jnp.where(own, scores[s] * (HD_X ** -0.5), -jnp.inf)
        m = jnp.max(sc, axis=-1, keepdims=True)
        p = jnp.exp(sc - m)
        o = _dot(p, mv_ref[0, s]) / jnp.sum(p, axis=-1, keepdims=True)
        for h in range(H_X):
            att_ref[rs, h * HD_X:(h + 1) * HD_X] = o[h * l:(h + 1) * l, :]
    y_ref[...] = x + jnp.dot(att_ref[...].astype(BF16), wo_ref[...], preferred_element_type=F32)


def _mem_attend_cached(x, g, wq, qn_g, mk, mv, wo, *, layer, nseq):
    t, d = x.shape
    hw = H_X * HD_X
    tm = nseq * (t // mk.shape[1])
    full = lambda shape: pl.BlockSpec(shape, lambda i: (0,) * len(shape))
    mem_spec = pl.BlockSpec((1, nseq, N_MEM * H_X, HD_X), lambda i: (layer, i, 0, 0))
    return pl.pallas_call(
        functools.partial(_mem_attend_cached_body, nseq=nseq),
        grid=(t // tm,),
        in_specs=[pl.BlockSpec((tm, d), lambda i: (i, 0)), full((1, d)), wq.spec(), full((1, HD_X)),
                  mem_spec, mem_spec, wo.spec()],
        out_specs=pl.BlockSpec((tm, d), lambda i: (i, 0)),
        out_shape=jax.ShapeDtypeStruct((t, d), F32),
        scratch_shapes=[pltpu.VMEM((tm, hw), F32)],
        compiler_params=_cparams(("arbitrary",)),
        name="mem_attend_cached",
    )(x, g.reshape(1, d), wq.array, qn_g.reshape(1, HD_X), mk, mv, wo.array)


def _ffn_body(*refs, tm, ffc, tiles_per_seq, per_row_state):
    if per_row_state:
        x_ref, g_ref, wg_ref, wu_ref, cw_ref, cb_ref, wd_ref, c0_ref, y_ref, gt_ref, tail_ref = refs
    else:
        x_ref, g_ref, wg_ref, wu_ref, cw_ref, cb_ref, wd_ref, y_ref, gt_ref, tail_ref = refs
    ff = wg_ref.shape[1]
    x = x_ref[...]
    h = _rms(x, g_ref[...]).astype(BF16)
    hist = CONV_W - 1
    if per_row_state:
        ns = tm // 8
        r_in = lax.broadcasted_iota(jnp.int32, (tm, hist * ns), 0)
        c_in = lax.broadcasted_iota(jnp.int32, (tm, hist * ns), 1)
        same = (r_in // 8) == (c_in // hist)
        sel1 = (same & (r_in % 8 == 0) & (c_in % hist == hist - 1)).astype(BF16)
        sel2 = (same & (r_in % 8 < hist) & (c_in % hist == r_in % 8)).astype(BF16)
        r_out = lax.broadcasted_iota(jnp.int32, (hist * ns, tm), 0)
        c_out = lax.broadcasted_iota(jnp.int32, (hist * ns, tm), 1)
        sel_out = (((c_out // 8) == (r_out // hist)) & (c_out % 8 == 8 - hist + r_out % hist)).astype(BF16)

        def select(sel, val):
            return sum(jnp.dot(sel, piece, preferred_element_type=F32) for piece in _split3(val))
    else:
        @pl.when(pl.program_id(0) % tiles_per_seq == 0)
        def _():
            tail_ref[...] = jnp.zeros_like(tail_ref)

    row8 = lax.broadcasted_iota(jnp.int32, (8, ffc), 0)
    acc = None

    def down(acts):
        a = acts[0][0] if len(acts) == 1 else jnp.concatenate([p[0] for p in acts], axis=1)
        part = jnp.dot(a, wd_ref[acts[0][1].start:acts[-1][1].stop, :], preferred_element_type=F32)
        return part if acc is None else acc + part

    filling, ready = [], None
    for c0 in range(0, ff, ffc):
        cs = slice(c0, c0 + ffc)
        gate = jnp.dot(h, wg_ref[:, cs], preferred_element_type=F32)
        up = jnp.dot(h, wu_ref[:, cs], preferred_element_type=F32)
        if ready is not None:
            acc = down(ready)
            ready = None
        r1 = pltpu.roll(gate, 1, 0)
        r2 = pltpu.roll(gate, 2, 0)
        if per_row_state:
            pos = lax.broadcasted_iota(jnp.int32, gate.shape, 0) % 8
            c0c = c0_ref[:, cs]
            g1 = jnp.where(pos == 0, select(sel1, c0c), r1)
            g2 = jnp.where(pos < hist, select(sel2, c0c), r2)
            gt_ref[:, cs] = select(sel_out, gate)
        else:
            prev = tail_ref[:, cs]
            top1 = jnp.where(row8 == 0, prev[7:8, :], r1[0:8, :])
            top2 = jnp.where(row8 == 0, prev[6:7, :], jnp.where(row8 == 1, prev[7:8, :], r2[0:8, :]))
            g1 = jnp.concatenate([top1, r1[8:, :]], axis=0)
            g2 = jnp.concatenate([top2, r2[8:, :]], axis=0)
            tail_ref[:, cs] = gate[tm - 8:, :]
            gt_ref[:, cs] = gate[tm - 8:, :]
        conv = cb_ref[:, cs] + cw_ref[0:1, cs] * g2 + cw_ref[1:2, cs] * g1 + cw_ref[2:3, cs] * gate
        filling.append(((_gelu(conv) * up).astype(BF16), cs))
        if len(filling) == FFN_DOWN_GROUP:
            filling, ready = [], filling
    for acts in (ready, filling):
        if acts:
            acc = down(acts)
    y_ref[...] = x + acc


def _ffn(x, g, wg, wu, cw, cb, wd, *, tm, ffc, tiles_per_seq, c0=None):
    t, d = x.shape
    ff = wg.shape[1]
    per_row_state = c0 is not None
    tail_n = tm // 8 * (CONV_W - 1) if per_row_state else 8
    in_specs = [pl.BlockSpec((tm, d), lambda i: (i, 0)), _resident((1, d)), wg.spec(), wu.spec(),
                _resident((8, ff)), _resident((1, ff)), wd.spec()]
    args = [x, g.reshape(1, d), wg.array, wu.array, jnp.pad(cw, ((0, 8 - CONV_W), (0, 0))), cb.reshape(1, ff),
            wd.array]
    if per_row_state:
        in_specs += [pl.BlockSpec((tail_n, ff), lambda i: (i, 0))]
        args += [c0]
    return pl.pallas_call(
        functools.partial(_ffn_body, tm=tm, ffc=ffc, tiles_per_seq=tiles_per_seq, per_row_state=per_row_state),
        grid=(t // tm,),
        in_specs=in_specs,
        out_specs=[pl.BlockSpec((tm, d), lambda i: (i, 0)), pl.BlockSpec((tail_n, ff), lambda i: (i, 0))],
        out_shape=[jax.ShapeDtypeStruct((t, d), F32), jax.ShapeDtypeStruct((t // tm * tail_n, ff), F32)],
        scratch_shapes=[pltpu.VMEM((8, ff), F32)],
        compiler_params=_cparams(("arbitrary",)),
        name="ffn",
    )(*args)


def _run_prompt(x, pos, n_seq, mem, w):
    seq_len = pos.shape[0]
    tm = 512
    hw = H_X * HD_X
    out = {k: [] for k in ("ret", "hgrn", "swa_k", "swa_v", "mem_k", "mem_v", "conv")}
    for l in range(DEPTH):
        j = l // 2
        if l % 2 == 0:
            x, s_new = _even_layer(x, pos, w["norm_mix_g"][l], w["ev_w_in"][j], w["ev_w_out"][j], w["ret_gn_g"][j],
                                   w["mlp_norm_g"][j], w["mlp_w_s"][j], w["mlp_b_s"][j], n_seq=n_seq, r=1024, pb=256)
            out["ret"].append(s_new)
        else:
            x, s_new, k_new, v_new = _odd_layer(x, pos, w["norm_mix_g"][l], w["od_w_in"][j], w["od_w_out"][j],
                                                w["hgrn_lb_logits"], w["hgrn_onorm_g"][j], w["swa_qnorm_g"][j],
                                                w["swa_knorm_g"][j], w["swa_sinks"][j], n_seq=n_seq, r=512, pb=256,
                                                layer=l)
            out["hgrn"].append(s_new)
            out["swa_k"].append(k_new.reshape(n_seq, WINDOW, HKV_D, HD_D))
            out["swa_v"].append(v_new.reshape(n_seq, WINDOW, HKV_D, HD_D))
        mk, mv = _mem_kv(mem, w["mem_w_kv"][l], w["mem_knorm_g"][l])
        out["mem_k"].append(mk.reshape(n_seq, N_MEM, H_X, HD_X))
        out["mem_v"].append(mv.reshape(n_seq, N_MEM, H_X, HD_X))
        x = _mem_attend(x, w["norm_mem_g"][l], w["mem_w_q"][l], w["mem_qnorm_g"][l], mk.reshape(n_seq, N_MEM, hw),
                        mv.reshape(n_seq, N_MEM, hw), w["mem_w_o"][l], tm=tm, nsplit=2, tiles_per_mem=seq_len // tm)
        x, gt = _ffn(x, w["norm_ffn_g"][l], w["ffn_w_gate"][l], w["ffn_w_up"][l], w["ffn_conv_w"][l],
                     w["ffn_conv_b"][l], w["ffn_w_down"][l], tm=tm, ffc=256, tiles_per_seq=seq_len // tm)
        out["conv"].append(gt.reshape(n_seq, seq_len // tm, 8, D_FF)[:, -1, 8 - (CONV_W - 1):, :])
    return x, {name: jnp.stack(rows) for name, rows in out.items()}


def _run_sample(x, pos, n_seq, st, w):
    seq_len = pos.shape[0]
    assert seq_len == 8
    t = x.shape[0]
    out = {k: [] for k in ("ret", "chunk_v", "hgrn", "swa_k", "swa_v", "conv")}
    for l in range(DEPTH):
        j = l // 2
        if l % 2 == 0:
            x, s_new, v_rows = _even_mixer(x, pos, w["norm_mix_g"][l], w["ev_w_in"][j], w["ev_w_out"][j],
                                           st["ret"][j], w["ret_gn_g"][j], w["mlp_norm_g"][j], w["mlp_w_s"][j],
                                           w["mlp_b_s"][j], nseg=16)
            out["chunk_v"].append(v_rows.reshape(n_seq, seq_len, G_B, DG_B))
            out["ret"].append(s_new)
        else:
            x_mid, s_new = _hgrn_mixer(x, w["norm_mix_g"][l], w["od_w_in"][j], w["od_w_out"][j], w["hgrn_lb_logits"],
                                       w["hgrn_onorm_g"][j], st["hgrn"][j], c=seq_len, nseq=16, layer=l)
            x, k_new, v_new = _swa_sample(x, x_mid, pos, w["norm_mix_g"][l], w["od_w_in"][j], w["od_w_out"][j],
                                          jnp.transpose(st["swa_k"][j], (0, 2, 3, 1)),
                                          jnp.transpose(st["swa_v"][j], (0, 2, 3, 1)), w["swa_qnorm_g"][j],
                                          w["swa_knorm_g"][j], w["swa_sinks"][j], nseq=16, group=8)
            out["hgrn"].append(s_new)
            out["swa_k"].append(jnp.transpose(k_new, (0, 3, 1, 2)))
            out["swa_v"].append(jnp.transpose(v_new, (0, 3, 1, 2)))
        x = _mem_attend_cached(x, w["norm_mem_g"][l], w["mem_w_q"][l], w["mem_qnorm_g"][l],
                               st["mem_k"].reshape(DEPTH, n_seq, N_MEM * H_X, HD_X),
                               st["mem_v"].reshape(DEPTH, n_seq, N_MEM * H_X, HD_X), w["mem_w_o"][l], layer=l, nseq=8)
        c0 = st["conv"][l].reshape(n_seq * (CONV_W - 1), D_FF)
        x, gt = _ffn(x, w["norm_ffn_g"][l], w["ffn_w_gate"][l], w["ffn_w_up"][l], w["ffn_conv_w"][l],
                     w["ffn_conv_b"][l], w["ffn_w_down"][l], tm=512, ffc=256, tiles_per_seq=1, c0=c0)
        out["conv"].append(gt.reshape(n_seq, CONV_W - 1, D_FF))
    return x, {name: jnp.stack(rows) for name, rows in out.items()}


def kernel(x_prompt, x_sample, state_ret, state_hgrn, cache_swa_k, cache_swa_v, cache_mem_k, cache_mem_v,
           state_ffn_conv, mem_prompt, norm_mix_g, norm_mem_g, norm_ffn_g, ev_w_in, ev_w_out, ret_gn_g,
           mlp_norm_g, mlp_w_s, mlp_b_s, od_w_in, od_w_out, hgrn_lb_logits, hgrn_onorm_g, swa_qnorm_g,
           swa_knorm_g, swa_sinks, mem_w_q, mem_w_kv, mem_qnorm_g, mem_knorm_g, mem_w_o, ffn_w_gate,
           ffn_w_up, ffn_conv_w, ffn_conv_b, ffn_w_down):
    def bf(a):
        stacked = a.astype(BF16)
        return [_Layer(stacked, i) for i in range(a.shape[0])]

    w = dict(norm_mix_g=norm_mix_g, norm_mem_g=norm_mem_g, norm_ffn_g=norm_ffn_g, ev_w_in=bf(ev_w_in),
             ev_w_out=bf(ev_w_out), ret_gn_g=ret_gn_g, mlp_norm_g=mlp_norm_g, mlp_w_s=mlp_w_s, mlp_b_s=mlp_b_s,
             od_w_in=bf(od_w_in), od_w_out=bf(od_w_out), hgrn_lb_logits=hgrn_lb_logits, hgrn_onorm_g=hgrn_onorm_g,
             swa_qnorm_g=swa_qnorm_g, swa_knorm_g=swa_knorm_g, swa_sinks=swa_sinks, mem_w_q=bf(mem_w_q),
             mem_w_kv=bf(mem_w_kv), mem_qnorm_g=mem_qnorm_g, mem_knorm_g=mem_knorm_g, mem_w_o=bf(mem_w_o),
             ffn_w_gate=bf(ffn_w_gate), ffn_w_up=bf(ffn_w_up), ffn_conv_w=ffn_conv_w, ffn_conv_b=ffn_conv_b,
             ffn_w_down=bf(ffn_w_down))
    b, seq, d = x_prompt.shape
    db, dseq, _ = x_sample.shape
    pos_prompt = np.arange(seq)
    pos_sample = PAST_LEN + np.arange(dseq)
    y_p, ns_p = _run_prompt(x_prompt.reshape(b * seq, d), pos_prompt, b, mem_prompt.reshape(b * N_MEM, d), w)
    st = dict(ret=state_ret, hgrn=state_hgrn, swa_k=cache_swa_k, swa_v=cache_swa_v, mem_k=cache_mem_k,
              mem_v=cache_mem_v, conv=state_ffn_conv)
    y_s, ns_s = _run_sample(x_sample.reshape(db * dseq, d), pos_sample, db, st, w)
    return (y_p.reshape(b, seq, d), y_s.reshape(db, dseq, d), ns_p["ret"], ns_p["hgrn"], ns_p["swa_k"], ns_p["swa_v"],
            ns_p["mem_k"], ns_p["mem_v"], ns_p["conv"], ns_s["ret"], ns_s["chunk_v"], ns_s["hgrn"], ns_s["swa_k"],
            ns_s["swa_v"], ns_s["conv"])
```

```python
import functools
import math

import jax
import jax.numpy as jnp
import numpy as np
from jax import lax
from jax.experimental import pallas as pl
from jax.experimental.pallas import tpu as pltpu

F32 = jnp.float32
BF16 = jnp.bfloat16

D_MODEL = 1024
DEPTH = 2
PAST_LEN = 16384
H_A, DV_A, DK_A = 4, 128, 64
RET_CHUNK = 128
RET_THETA = 10000.0
G_B, DG_B = 4, 128
MLP_CHUNK = 128
H_C, DK_C, DV_C = 4, 128, 128
HGRN_CHUNK = 64
HD_D, HQ_D, HKV_D = 64, 8, 2
WINDOW = 128
ROPE_THETA = 500000.0
ROT_DIM_D = HD_D // 4
N_MEM, H_X, HD_X = 256, 4, 128
D_FF = 2816
CONV_W = 3
FFN_DOWN_GROUP = 6
EPS = 1e-6
SQRT_HALF = float(np.sqrt(0.5))
EV_Q, EV_K, EV_V, EV_G, EV_U, EV_VB, EV_END = 0, 256, 512, 1024, 1536, 2048, 2560
OD_Q, OD_F, OD_I, OD_G, OD_QD, OD_KD, OD_VD, OD_END = 0, 512, 1024, 1536, 2048, 2560, 2688, 2816

VMEM_LIMIT_BYTES = 56 * 1024 * 1024


def _cparams(sem):
    return pltpu.CompilerParams(dimension_semantics=sem, vmem_limit_bytes=VMEM_LIMIT_BYTES)


def _resident(shape):
    return pl.BlockSpec(shape, lambda *_: (0,) * len(shape), pipeline_mode=pl.Buffered(1))


class _Layer:
    def __init__(self, array, index):
        self.array, self.index, self.shape = array, index, array.shape[1:]

    def spec(self):
        index, zeros = self.index, (0,) * len(self.shape)
        return pl.BlockSpec((None,) + self.shape, lambda *_: (index,) + zeros, pipeline_mode=pl.Buffered(1))


def _rms(x, g):
    return x * lax.rsqrt(jnp.mean(x * x, axis=-1, keepdims=True) + EPS) * g


def _ln(x, g):
    mu = jnp.mean(x, axis=-1, keepdims=True)
    xc = x - mu
    return xc * lax.rsqrt(jnp.mean(xc * xc, axis=-1, keepdims=True) + EPS) * g


def _sigmoid(x):
    return 1.0 / (1.0 + jnp.exp(-x))


def _silu(x):
    return x * _sigmoid(x)


def _gelu(x):
    return 0.5 * x * (1.0 + lax.erf(x * SQRT_HALF))


def _dot(a, b):
    return jnp.dot(a.astype(BF16), b.astype(BF16), preferred_element_type=F32)


def _dot_nt(a, b):
    return lax.dot_general(a.astype(BF16), b.astype(BF16), (((1,), (1,)), ((), ())), preferred_element_type=F32)


def _dot_tn(a, b):
    return lax.dot_general(a.astype(BF16), b.astype(BF16), (((0,), (0,)), ((), ())), preferred_element_type=F32)


def _split3(x):
    hi = x.astype(BF16)
    r1 = x - hi.astype(F32)
    mid = r1.astype(BF16)
    lo = (r1 - mid.astype(F32)).astype(BF16)
    return hi, mid, lo


def _retention_consts(c, nseq=1):
    lg = np.log1p(-np.exp2(-5.0 - np.arange(H_A, dtype=np.float64)))
    idx = np.arange(c, dtype=np.float64)
    rel = idx[:, None] - idx[None, :]
    dmask = np.where(rel >= 0, np.exp(rel[None] * lg[:, None, None]), 0.0)
    q_dec = np.exp((idx + 1.0)[None, :] * lg[:, None])
    k_dec = np.exp((c - 1.0 - idx)[None, :] * lg[:, None])
    c_dec = np.exp(c * lg)
    qdec = np.tile(np.repeat(q_dec.T, DK_A, axis=1), (nseq, 1))
    kdec = np.tile(np.repeat(k_dec.T, DK_A, axis=1), (nseq, 1))
    dmask = np.stack([np.kron(np.eye(nseq), dmask[h]) for h in range(H_A)])
    cdec = np.broadcast_to(c_dec[:, None, None], (H_A, 8, DV_A))
    return tuple(jnp.asarray(a, F32) for a in (dmask, qdec, kdec, cdec))


def _rope_tables(pos, rot_dim, theta, head_dim, reps, row_reps=1):
    half = rot_dim // 2
    inv = theta ** (-np.arange(half, dtype=np.float64) / half)
    ang = pos.astype(np.float64)[:, None] * inv[None, :]
    cos = np.cos(ang)
    sin = np.sin(ang)
    l = pos.shape[0]
    pad = head_dim - rot_dim
    cos_h = np.concatenate([cos, cos, np.ones((l, pad))], axis=1)
    sin_h = np.concatenate([-sin, sin, np.zeros((l, pad))], axis=1)
    return (jnp.asarray(np.tile(cos_h, (row_reps, reps)), F32), jnp.asarray(np.tile(sin_h, (row_reps, reps)), F32))


def _even_consts(c, gn_g, mn_g, w_s, b_s, nseq=1):
    dmask, qdec, kdec, cdec = _retention_consts(c, nseq)
    bs_full = jnp.tile(jnp.repeat(b_s[:, :c].T, DG_B, axis=1), (nseq, 1))
    arrays = [dmask, qdec, kdec, cdec, gn_g.reshape(1, H_A * DV_A), mn_g.reshape(1, G_B * DG_B),
              jnp.tile(w_s[:, :c, :c], (1, nseq, nseq)), bs_full]
    return arrays, [a.shape for a in arrays]


def _retention_state_batched(qd, kd, v, s0_ref, cdec_ref, nseq):
    n = qd.shape[0]
    own = (lax.broadcasted_iota(jnp.int32, (n, nseq * 128), 1) // 128
           == lax.broadcasted_iota(jnp.int32, (n, nseq * 128), 0) // (n // nseq))
    place = lambda x: jnp.where(own, jnp.concatenate([x] * nseq, axis=1), 0.0)
    inter, s_new = [None] * H_A, [None] * H_A
    for p in range(H_A // 2):
        h0, h1 = 2 * p, 2 * p + 1
        lanes = slice(p * 128, (p + 1) * 128)
        s_a, s_b = s0_ref[:, h0], s0_ref[:, h1]
        z = jnp.zeros_like(s_a)
        slab = jnp.concatenate([jnp.concatenate([s_a, z], axis=2), jnp.concatenate([z, s_b], axis=2)], axis=1)
        res = _dot(place(qd[:, lanes]), slab.reshape(nseq * 2 * DK_A, 2 * DV_A))
        inter[h0], inter[h1] = res[:, :DV_A], res[:, DV_A:]
        upd = _dot_tn(place(kd[:, lanes]), v[:, 2 * p * DV_A:2 * (p + 1) * DV_A]).reshape(nseq, 2 * DK_A, 2 * DV_A)
        s_new[h0] = cdec_ref[h0, 0:1, :] * s_a + upd[:, :DK_A, :DV_A]
        s_new[h1] = cdec_ref[h1, 0:1, :] * s_b + upd[:, DK_A:, DV_A:]
    return inter, s_new


def _even_segment(pj, cos, sin, consts, s_old, nseq=1):
    dmask_ref, qdec_ref, kdec_ref, cdec_ref, gng_ref, mng_ref, ws_ref, bs_ref = consts
    c = pj.shape[0]
    width = H_A * DK_A
    lane = lax.broadcasted_iota(jnp.int32, (c, width), 1)
    first_half = (lane % DK_A) < (DK_A // 2)
    row = lax.broadcasted_iota(jnp.int32, (c, c), 0)
    col = lax.broadcasted_iota(jnp.int32, (c, c), 1)
    tril = (col <= row) & (row // (c // nseq) == col // (c // nseq))

    def rope(x):
        rot = jnp.where(first_half, pltpu.roll(x, width - DK_A // 2, 1), pltpu.roll(x, DK_A // 2, 1))
        return x * cos + rot * sin

    q = rope(pj[:, EV_Q:EV_K])
    k = rope(pj[:, EV_K:EV_V]) * (DK_A ** -0.5)
    v = pj[:, EV_V:EV_G]
    ga = pj[:, EV_G:EV_U]
    qd = q * qdec_ref[...]
    kd = k * kdec_ref[...]
    ks = [slice(h * DK_A, (h + 1) * DK_A) for h in range(H_A)]
    vs = [slice(h * DV_A, (h + 1) * DV_A) for h in range(H_A)]
    qk = [_dot_nt(q[:, ks[h]], k[:, ks[h]]) for h in range(H_A)]
    if nseq == 1:
        inter = [_dot(qd[:, ks[h]], s_old[h]) for h in range(H_A)]
        s_new = [cdec_ref[h, 0:1, :] * s_old[h] + _dot_tn(kd[:, ks[h]], v[:, vs[h]]) for h in range(H_A)]
    else:
        inter, s_new = _retention_state_batched(qd, kd, v, s_old, cdec_ref, nseq)
    vg = _gelu(pj[:, EV_VB:EV_END])
    vrows = [_ln(vg[:, gs], mng_ref[:, gs]) for gs in vs]
    mixed = [_dot(jnp.where(tril, ws_ref[g], 0.0), vrows[g]) for g in range(G_B)]
    yield s_new
    outs = []
    for h in range(H_A):
        o = _dot(qk[h] * dmask_ref[h], v[:, vs[h]]) + inter[h]
        outs.append(_silu(ga[:, vs[h]]) * _ln(o, gng_ref[:, vs[h]]))
    u = _gelu(pj[:, EV_U:EV_VB])
    for g, gs in enumerate(vs):
        outs.append(u[:, gs] * (mixed[g] + bs_ref[:, gs]))
    yield jnp.concatenate(outs, axis=1), s_new, jnp.concatenate(vrows, axis=1)


def _even_body(x_ref, ng_ref, win_ref, wout_ref, cos_ref, sin_ref, dmask_ref, qdec_ref, kdec_ref, cdec_ref, gng_ref,
               mng_ref, ws_ref, bs_ref, s0_ref, y_ref, sout_ref, vrows_ref, *, nseg):
    consts = (dmask_ref, qdec_ref, kdec_ref, cdec_ref, gng_ref, mng_ref, ws_ref, bs_ref)
    x = x_ref[...]
    pj = jnp.dot(_rms(x, ng_ref[...]).astype(BF16), win_ref[...], preferred_element_type=F32)
    (o, s_new, vrows), = _interleave(_even_segment(pj, cos_ref[...], sin_ref[...], consts, s0_ref, nseq=nseg))
    for h in range(H_A):
        sout_ref[:, h] = s_new[h]
    y_ref[...] = x + jnp.dot(o.astype(BF16), wout_ref[...], preferred_element_type=F32)
    vrows_ref[...] = vrows


def _even_mixer(x, pos, ng, w_in, w_out, s0, gn_g, mn_g, w_s, b_s, *, nseg):
    t, d = x.shape
    c = pos.shape[0]
    r = nseg * c
    consts, shapes = _even_consts(c, gn_g, mn_g, w_s, b_s, nseq=nseg)
    cos, sin = _rope_tables(pos, DK_A, RET_THETA, DK_A, H_A, row_reps=nseg)
    rowb = pl.BlockSpec((r, d), lambda i: (i, 0))
    st_spec = pl.BlockSpec((nseg, H_A, DK_A, DV_A), lambda i: (i, 0, 0, 0))
    return pl.pallas_call(
        functools.partial(_even_body, nseg=nseg),
        grid=(t // r,),
        in_specs=[rowb, _resident((1, d)), w_in.spec(), w_out.spec(), _resident((r, 256)), _resident((r, 256))]
        + [_resident(s) for s in shapes] + [st_spec],
        out_specs=[rowb, st_spec, pl.BlockSpec((r, 512), lambda i: (i, 0))],
        out_shape=[jax.ShapeDtypeStruct((t, d), F32), jax.ShapeDtypeStruct(s0.shape, F32),
                   jax.ShapeDtypeStruct((t, 512), F32)],
        compiler_params=_cparams(("arbitrary",)),
        name="even_mixer",
    )(x, ng.reshape(1, d), w_in.array, w_out.array, cos, sin, *consts, s0)


def _even_layer_body(x_ref, ng_ref, win_ref, wout_ref, cos_ref, sin_ref, dmask_ref, qdec_ref, kdec_ref, cdec_ref,
                     gng_ref, mng_ref, ws_ref, bs_ref, y_ref, sout_ref, st_ref, *, c, pb):
    step = pl.program_id(1)

    @pl.when(step == 0)
    def _():
        st_ref[...] = jnp.zeros_like(st_ref)

    consts = (dmask_ref, qdec_ref, kdec_ref, cdec_ref, gng_ref, mng_ref, ws_ref, bs_ref)
    x = x_ref[...]
    h = _rms(x, ng_ref[...]).astype(BF16)
    ngrp = x.shape[0] // pb
    project = lambda g: jnp.dot(h[g * pb:(g + 1) * pb, :], win_ref[...], preferred_element_type=F32)
    proj = project(0)
    state = [st_ref[hh] for hh in range(H_A)]
    pending = None

    def finish(rows, seg):
        o, _, _ = next(seg)
        y_ref[rows, :] = x[rows, :] + jnp.dot(o.astype(BF16), wout_ref[...], preferred_element_type=F32)

    for g in range(ngrp):
        nxt = project(g + 1) if g + 1 < ngrp else None
        for j in range(pb // c):
            rows = slice(g * pb + j * c, g * pb + (j + 1) * c)
            seg = _even_segment(proj[j * c:(j + 1) * c, :], cos_ref[rows, :], sin_ref[rows, :], consts, state)
            state = next(seg)
            if pending is not None:
                finish(*pending)
            pending = (rows, seg)
        proj = nxt
    finish(*pending)
    for hh in range(H_A):
        st_ref[hh] = state[hh]

    @pl.when(step == pl.num_programs(1) - 1)
    def _():
        sout_ref[0] = st_ref[...]


def _even_layer(x, pos, ng, w_in, w_out, gn_g, mn_g, w_s, b_s, *, n_seq, r, pb):
    t, d = x.shape
    seq_len = pos.shape[0]
    c = math.gcd(seq_len, RET_CHUNK)
    assert c == min(MLP_CHUNK, seq_len) and pb % c == 0 and r % pb == 0 and seq_len % r == 0
    steps = seq_len // r
    consts, shapes = _even_consts(c, gn_g, mn_g, w_s, b_s)
    cos, sin = _rope_tables(pos, DK_A, RET_THETA, DK_A, H_A)
    rowb = pl.BlockSpec((r, d), lambda b, i: (b * steps + i, 0))
    tab = pl.BlockSpec((r, 256), lambda b, i: (i, 0))
    st_spec = pl.BlockSpec((1, H_A, DK_A, DV_A), lambda b, i: (b, 0, 0, 0))
    return pl.pallas_call(
        functools.partial(_even_layer_body, c=c, pb=pb),
        grid=(n_seq, steps),
        in_specs=[rowb, _resident((1, d)), w_in.spec(), w_out.spec(), tab, tab]
        + [_resident(s) for s in shapes],
        out_specs=[rowb, st_spec],
        out_shape=[jax.ShapeDtypeStruct((t, d), F32), jax.ShapeDtypeStruct((n_seq, H_A, DK_A, DV_A), F32)],
        scratch_shapes=[pltpu.VMEM((H_A, DK_A, DV_A), F32)],
        compiler_params=_cparams(("arbitrary", "arbitrary")),
        name="even_layer",
    )(x, ng.reshape(1, d), w_in.array, w_out.array, cos, sin, *consts)


def _hgrn_lower_bound(lbl, layer):
    e = jnp.exp(lbl - jnp.max(lbl, axis=0, keepdims=True))
    sm = e / jnp.sum(e, axis=0, keepdims=True)
    acc = sm[0:1, :]
    first = acc
    for i in range(1, layer + 1):
        acc = acc + sm[i:i + 1, :]
    return acc - first


def _block_row(x, blk, j):
    n, lanes = x.shape
    if blk % 16 == 0:
        return jnp.concatenate(
            [jnp.broadcast_to(x[i * blk + j:i * blk + j + 1, :], (blk, lanes)) for i in range(n // blk)], axis=0)
    x3 = x.reshape(n // 8, 8, lanes)
    pick = lambda jj: jnp.broadcast_to(x3[:, jj:jj + 1, :], x3.shape).reshape(n, lanes)
    if blk == 8:
        return pick(j)
    assert blk == 4
    upper = (lax.broadcasted_iota(jnp.int32, (n, 1), 0) % 8) >= 4
    return jnp.where(upper, pick(4 + j), pick(j))


def _hgrn_gates(pj, lb):
    f = lb + (1.0 - lb) * _sigmoid(pj[:, OD_F:OD_I])
    return f, jnp.log(f), 1.0 - f, _silu(pj[:, OD_Q:OD_F]), pj[:, OD_I:OD_G], pj[:, OD_G:OD_QD]


def _chunk_cumsum(g, c):
    n = g.shape[0]
    row = lax.broadcasted_iota(jnp.int32, (n, n), 0)
    col = lax.broadcasted_iota(jnp.int32, (n, n), 1)
    tri = ((col <= row) & (row // c == col // c)).astype(BF16)
    return sum(jnp.dot(tri, piece, preferred_element_type=F32) for piece in _split3(g))


def _hgrn_intra(q, k, f, b, c):
    return _hgrn_assemble(_hgrn_level_products(q, k, f, b, c))


def _hgrn_level_products(q, k, f, b, c):
    n = q.shape[0]
    rloc = lax.broadcasted_iota(jnp.int32, (n, 1), 0)
    diag = jnp.sum(q * k, axis=-1, keepdims=True)
    prods = []
    blk = 2
    while blk <= c:
        second = (rloc % blk) >= blk // 2
        if blk == 2:
            e = f
            ke = k
        else:
            dq = b - _block_row(b, blk, blk // 2 - 1)
            e = jnp.exp(jnp.where(second, dq, -dq))
            ke = k * e
        prods.append((blk, _dot_nt(jnp.where(second, q * e, 0.0), jnp.where(second, 0.0, ke))))
        blk *= 2
    return diag, prods


def _hgrn_assemble(level_products):
    diag, prods = level_products
    n = diag.shape[0]
    row = lax.broadcasted_iota(jnp.int32, (n, n), 0)
    col = lax.broadcasted_iota(jnp.int32, (n, n), 1)
    a = jnp.where(row == col, diag, 0.0)
    for blk, p in prods:
        a = a + jnp.where((row // blk) == (col // blk), p, 0.0)
    return a


def _interleave(*gens):
    results = [None] * len(gens)
    live = list(enumerate(gens))
    while live:
        still = []
        for i, gen in live:
            try:
                results[i] = next(gen)
                still.append((i, gen))
            except StopIteration:
                pass
        live = still
    return results


def _hgrn_tile(pj, lb, ong_ref, st_ref, c):
    n = pj.shape[0]
    heads = [slice(h * DK_C, (h + 1) * DK_C) for h in range(H_C)]
    chunks = [slice(j * c, (j + 1) * c) for j in range(n // c)]
    f, g, k, q, v, gc = _hgrn_gates(pj, lb)
    b = _chunk_cumsum(g, c)
    qe = q * jnp.exp(b)
    kd = k * jnp.exp(_block_row(b, c, c - 1) - b)
    levels = [_hgrn_level_products(q[:, hs], k[:, hs], f[:, hs], b[:, hs], c) for hs in heads]
    kv = [[_dot_tn(kd[rs, hs], v[rs, hs]) for rs in chunks] for hs in heads]
    yield None
    o_intra = [_dot(_hgrn_assemble(levels[h]), v[:, hs]) for h, hs in enumerate(heads)]
    yield None
    outs = []
    for h, hs in enumerate(heads):
        s = st_ref[h]
        parts = []
        for j, rs in enumerate(chunks):
            parts.append(o_intra[h][rs, :] + _dot(qe[rs, hs], s))
            bl = b[rs.stop - 1:rs.stop, hs]
            decay = jnp.broadcast_to(jnp.exp(bl), (DK_C, DK_C)).T
            s = decay * s + kv[h][j]
        st_ref[h] = s
        outs.append(_rms(jnp.concatenate(parts, axis=0), ong_ref[:, hs]) * _silu(gc[:, hs]))
    yield jnp.concatenate(outs, axis=1)


def _hgrn_body(x_ref, ng_ref, win_ref, wout_ref, lbl_ref, ong_ref, s0_ref, y_ref, sout_ref, *, c, layer):
    x = x_ref[...]
    pj = jnp.dot(_rms(x, ng_ref[...]).astype(BF16), win_ref[:, OD_Q:OD_QD], preferred_element_type=F32)
    n = pj.shape[0]
    ns = n // c
    lb = _hgrn_lower_bound(lbl_ref[...], layer)
    f, g, k, q, v, gc = _hgrn_gates(pj, lb)
    b = _chunk_cumsum(g, c)
    blast = _block_row(b, c, c - 1)
    qe = q * jnp.exp(b)
    kd = k * jnp.exp(blast - b)
    own_cols = (lax.broadcasted_iota(jnp.int32, (n, ns * DK_C), 1) // DK_C
                == lax.broadcasted_iota(jnp.int32, (n, ns * DK_C), 0) // c)
    place = lambda x: jnp.where(own_cols, jnp.concatenate([x] * ns, axis=1), 0.0)
    seq_cols = (lax.broadcasted_iota(jnp.int32, (ns, ns * DK_C), 1) // DK_C
                == lax.broadcasted_iota(jnp.int32, (ns, ns * DK_C), 0))
    ones = jnp.ones((ns, DV_C), BF16)
    last_row = (lax.broadcasted_iota(jnp.int32, (ns, n), 1)
                == lax.broadcasted_iota(jnp.int32, (ns, n), 0) * c + (c - 1)).astype(BF16)
    b_last = sum(jnp.dot(last_row, piece, preferred_element_type=F32) for piece in _split3(b))
    outs = []
    for h in range(H_C):
        hs = slice(h * DK_C, (h + 1) * DK_C)
        vh = v[:, hs]
        s = s0_ref[:, h].reshape(ns * DK_C, DV_C)
        o = _dot(_hgrn_intra(q[:, hs], k[:, hs], f[:, hs], b[:, hs], c), vh) + _dot(place(qe[:, hs]), s)
        e_last = jnp.exp(b_last[:, hs])
        e_placed = jnp.where(seq_cols, jnp.concatenate([e_last] * ns, axis=1), 0.0)
        decay = sum(lax.dot_general(piece, ones, (((0,), (0,)), ((), ())), preferred_element_type=F32)
                    for piece in _split3(e_placed))
        s_new = decay * s + _dot_tn(place(kd[:, hs]), vh)
        sout_ref[:, h] = s_new.reshape(ns, DK_C, DV_C)
        outs.append(_rms(o, ong_ref[:, hs]) * _silu(gc[:, hs]))
    o_c = jnp.concatenate(outs, axis=1).astype(BF16)
    y_ref[...] = x + jnp.dot(o_c, wout_ref[0:H_C * DV_C, :], preferred_element_type=F32)


def _hgrn_mixer(x, ng, w_in, w_out, lb_logits, on_g, s0, *, c, nseq, layer):
    t, d = x.shape
    r = nseq * c
    rowb = pl.BlockSpec((r, d), lambda i: (i, 0))
    st_spec = pl.BlockSpec((nseq, H_C, DK_C, DV_C), lambda i: (i, 0, 0, 0))
    return pl.pallas_call(
        functools.partial(_hgrn_body, c=c, layer=layer),
        grid=(t // r,),
        in_specs=[rowb, _resident((1, d)), w_in.spec(), w_out.spec(), _resident((DEPTH, 512)), _resident((1, 512)),
                  st_spec],
        out_specs=[rowb, st_spec],
        out_shape=[jax.ShapeDtypeStruct((t, d), F32), jax.ShapeDtypeStruct(s0.shape, F32)],
        compiler_params=_cparams(("arbitrary",)),
        name="hgrn_mixer",
    )(x, ng.reshape(1, d), w_in.array, w_out.array, lb_logits, on_g.reshape(1, 512), s0)


def _head_norm(x, g):
    rows, width = x.shape
    lo = lax.broadcasted_iota(jnp.int32, (rows, 128), 1) < HD_D
    outs = []
    for t in range(width // 128):
        xt = x[:, t * 128:(t + 1) * 128]
        sq = xt * xt
        ss_lo = jnp.sum(jnp.where(lo, sq, 0.0), axis=-1, keepdims=True)
        ss_hi = jnp.sum(jnp.where(lo, 0.0, sq), axis=-1, keepdims=True)
        scale = jnp.where(lo, lax.rsqrt(ss_lo * (1.0 / HD_D) + EPS), lax.rsqrt(ss_hi * (1.0 / HD_D) + EPS))
        outs.append(xt * scale)
    return jnp.concatenate(outs, axis=1) * g


def _rope_partial(x, cos, sin):
    width = x.shape[1]
    half = ROT_DIM_D // 2
    lane = lax.broadcasted_iota(jnp.int32, x.shape, 1)
    rot = jnp.where((lane % HD_D) < half, pltpu.roll(x, width - half, 1), pltpu.roll(x, half, 1))
    return x * cos + rot * sin


def _swa_prep(q, k, cos, sin, qng_ref, kng_ref):
    qn = _rope_partial(_head_norm(q, qng_ref[...]), jnp.concatenate([cos] * (HQ_D // 2), axis=1),
                       jnp.concatenate([sin] * (HQ_D // 2), axis=1)).astype(BF16)
    return qn, _rope_partial(_head_norm(k, kng_ref[...]), cos, sin)


def _swa_place(x):
    lo = lax.broadcasted_iota(jnp.int32, x.shape, 1) < HD_D
    sw = pltpu.roll(x, HD_D, 1)
    return [[jnp.where(lo, x, 0.0).astype(BF16), jnp.where(lo, 0.0, sw).astype(BF16)],
            [jnp.where(lo, sw, 0.0).astype(BF16), jnp.where(lo, 0.0, x).astype(BF16)]]


def _swa_block(qn, k_prev, k_cur, v_prev, v_cur, sink_ref, allow_prev):
    rep = HQ_D // HKV_D
    lo = lax.broadcasted_iota(jnp.int32, (WINDOW, 128), 1) < HD_D
    qi = lax.broadcasted_iota(jnp.int32, (WINDOW, 2 * WINDOW), 0)
    ci = lax.broadcasted_iota(jnp.int32, (WINDOW, 2 * WINDOW), 1)
    cur = (ci >= WINDOW) & ((ci - WINDOW) <= qi)
    prev = (ci < WINDOW) & (ci > qi)
    mask = prev | cur if allow_prev is True else (prev & allow_prev) | cur
    keys = lambda g, half: jnp.concatenate([k_prev[g][half], k_cur[g][half]], axis=0)
    ones = [jnp.where(lo, 1.0, 0.0).astype(BF16), jnp.where(lo, 0.0, 1.0).astype(BF16)]
    vals = lambda g, half: jnp.concatenate(
        [jnp.concatenate([v_prev[g][half], v_cur[g][half]], axis=0), jnp.concatenate([ones[half]] * 2, axis=0)], axis=1)
    scores = [lax.dot_general(qn[:, (h // 2) * 128:(h // 2 + 1) * 128], keys(h // rep, h % 2),
                              (((1,), (1,)), ((), ())), preferred_element_type=F32) for h in range(HQ_D)]
    yield None
    pv, sink_term = [], []
    for h in range(HQ_D):
        s = jnp.where(mask, scores[h] * (HD_D ** -0.5), -jnp.inf)
        sink = sink_ref[h:h + 1, 0:1]
        m = jnp.maximum(jnp.max(s, axis=-1, keepdims=True), sink)
        pv.append(jnp.dot(jnp.exp(s - m).astype(BF16), vals(h // rep, h % 2), preferred_element_type=F32))
        sink_term.append(jnp.exp(sink - m))
    yield None
    tiles = []
    for t in range(HQ_D // 2):
        both = pv[2 * t] + pv[2 * t + 1]
        den = both[:, 128:] + jnp.where(lo, sink_term[2 * t], sink_term[2 * t + 1])
        tiles.append(both[:, :128] / den)
    yield jnp.concatenate(tiles, axis=1)


def _swa_sample_chain(q2, kt_cache, k_new, vt_cache, v_new, sinks2, half, l):
    rows = q2.shape[0]
    ns = rows // (2 * l)
    lo = lax.broadcasted_iota(jnp.int32, (1, 128), 1) < HD_D
    ones = jnp.where(lo, 1.0, 0.0) if half == 0 else jnp.where(lo, 0.0, 1.0)
    ext = lambda v: jnp.concatenate([v, jnp.broadcast_to(ones, v.shape).astype(BF16)], axis=1)
    nt = lambda a, b: lax.dot_general(a, b, (((1,), (1,)), ((), ())), preferred_element_type=F32)
    zeros = jnp.zeros((HD_D, kt_cache.shape[1]), BF16)
    band = lambda x: jnp.concatenate([x, zeros] if half == 0 else [zeros, x], axis=0)
    k_cache = band(kt_cache.astype(BF16))
    v_cache = jnp.concatenate([band(vt_cache.astype(BF16)), band(jnp.ones_like(zeros))], axis=0)
    s1 = jnp.dot(q2, k_cache, preferred_element_type=F32)
    s2 = nt(q2, k_new)
    yield None
    r1 = lax.broadcasted_iota(jnp.int32, s1.shape, 0)
    c1 = lax.broadcasted_iota(jnp.int32, s1.shape, 1)
    ok1 = ((r1 % (ns * l)) // l == c1 // WINDOW) & (c1 % WINDOW > r1 % l)
    r2 = lax.broadcasted_iota(jnp.int32, s2.shape, 0)
    c2 = lax.broadcasted_iota(jnp.int32, s2.shape, 1)
    ok2 = ((r2 % (ns * l)) // l == c2 // l) & (c2 % l <= r2 % l)
    s1 = jnp.where(ok1, s1 * (HD_D ** -0.5), -jnp.inf)
    s2 = jnp.where(ok2, s2 * (HD_D ** -0.5), -jnp.inf)
    sink = jnp.where(lax.broadcasted_iota(jnp.int32, (rows, 1), 0) < ns * l, sinks2[0], sinks2[1])
    m = jnp.maximum(jnp.maximum(jnp.max(s1, axis=-1, keepdims=True), jnp.max(s2, axis=-1, keepdims=True)), sink)
    res = (nt(jnp.exp(s1 - m).astype(BF16), v_cache)
           + jnp.dot(jnp.exp(s2 - m).astype(BF16), ext(v_new), preferred_element_type=F32))
    yield res, jnp.exp(sink - m)


def _swa_sample_body(x_ref, res_ref, ng_ref, win_ref, wout_ref, kc_ref, vc_ref, cos_ref, sin_ref, qng_ref, kng_ref,
                     sink_ref, y_ref, kout_ref, vout_ref, o_ref, *, nseq, l, group):
    rep = HQ_D // HKV_D
    pj = jnp.dot(_rms(x_ref[...], ng_ref[...]).astype(BF16), win_ref[:, OD_QD:OD_END], preferred_element_type=F32)
    qn, kn = _swa_prep(pj[:, 0:OD_KD - OD_QD], pj[:, OD_KD - OD_QD:OD_VD - OD_QD], cos_ref[...], sin_ref[...],
                       qng_ref, kng_ref)
    v = pj[:, OD_VD - OD_QD:]
    lo = lax.broadcasted_iota(jnp.int32, (group * l, 128), 1) < HD_D
    r_sel = lax.broadcasted_iota(jnp.int32, (group * l, group * WINDOW), 0)
    c_sel = lax.broadcasted_iota(jnp.int32, (group * l, group * WINDOW), 1)
    append = ((c_sel // WINDOW == r_sel // l) & (c_sel % WINDOW == WINDOW - l + r_sel % l)).astype(BF16)
    is_new = lax.broadcasted_iota(jnp.int32, (HKV_D * HD_D, group * WINDOW), 1) % WINDOW >= WINDOW - l

    def shifted(old_t, new_rows):
        moved = sum(lax.dot_general(piece, append, (((0,), (0,)), ((), ())), preferred_element_type=F32)
                    for piece in _split3(new_rows))
        return jnp.where(is_new, moved, pltpu.roll(old_t, group * WINDOW - l, 1))

    chains, where = [], []
    for gi in range(nseq // group):
        rs = slice(gi * group * l, (gi + 1) * group * l)
        seqs = range(gi * group, (gi + 1) * group)
        kt = [jnp.concatenate([kc_ref[s, g] for s in seqs], axis=1) for g in range(HKV_D)]
        vt = [jnp.concatenate([vc_ref[s, g] for s in seqs], axis=1) for g in range(HKV_D)]
        k_new, v_new = _swa_place(kn[rs, :]), _swa_place(v[rs, :])
        for g in range(HKV_D):
            tiles = [(g * rep) // 2, (g * rep) // 2 + 1]
            q2 = jnp.concatenate([qn[rs, t * 128:(t + 1) * 128] for t in tiles], axis=0)
            for half in range(2):
                sinks2 = [sink_ref[2 * t + half:2 * t + half + 1, 0:1] for t in tiles]
                chains.append(_swa_sample_chain(q2, kt[g], k_new[g][half], vt[g], v_new[g][half], sinks2, half, l))
                where.append((rs, tiles, half))
        k_all = shifted(jnp.concatenate(kt, axis=0), kn[rs, :])
        v_all = shifted(jnp.concatenate(vt, axis=0), v[rs, :])
        for i, s in enumerate(seqs):
            for g in range(HKV_D):
                kout_ref[s, g] = k_all[g * HD_D:(g + 1) * HD_D, i * WINDOW:(i + 1) * WINDOW]
                vout_ref[s, g] = v_all[g * HD_D:(g + 1) * HD_D, i * WINDOW:(i + 1) * WINDOW]
    results = _interleave(*chains)
    for i in range(0, len(chains), 2):
        (rs, tiles, _), (res0, st0), (res1, st1) = where[i], results[i], results[i + 1]
        both = res0 + res1
        for j, t in enumerate(tiles):
            js = slice(j * group * l, (j + 1) * group * l)
            den = both[js, 128:] + jnp.where(lo, st0[js, :], st1[js, :])
            o_ref[rs, t * 128:(t + 1) * 128] = both[js, :128] / den
    y_ref[...] = res_ref[...] + jnp.dot(o_ref[...].astype(BF16), wout_ref[H_C * DV_C:, :], preferred_element_type=F32)


def _swa_gains(qn_g, kn_g):
    return (jnp.tile(qn_g, HQ_D).reshape(1, HQ_D * HD_D), jnp.tile(kn_g, HKV_D).reshape(1, HKV_D * HD_D))


def _swa_sample(x, res, pos, ng, w_in, w_out, kcache, vcache, qn_g, kn_g, sinks, *, nseq, group):
    t, d = x.shape
    l = pos.shape[0]
    r = nseq * l
    cos, sin = _rope_tables(pos, ROT_DIM_D, ROPE_THETA, HD_D, 2, row_reps=nseq)
    qng, kng = _swa_gains(qn_g, kn_g)
    rowb = pl.BlockSpec((r, d), lambda i: (i, 0))
    cache = pl.BlockSpec((nseq, HKV_D, HD_D, WINDOW), lambda i: (i, 0, 0, 0))
    return pl.pallas_call(
        functools.partial(_swa_sample_body, nseq=nseq, l=l, group=group),
        grid=(t // r,),
        in_specs=[rowb, rowb, _resident((1, d)), w_in.spec(), w_out.spec(), cache, cache, _resident((r, 128)),
                  _resident((r, 128)), _resident((1, 512)), _resident((1, 128)), _resident((HQ_D, 128))],
        out_specs=[rowb, cache, cache],
        out_shape=[jax.ShapeDtypeStruct((t, d), F32), jax.ShapeDtypeStruct(kcache.shape, F32),
                   jax.ShapeDtypeStruct(vcache.shape, F32)],
        scratch_shapes=[pltpu.VMEM((r, HQ_D * HD_D), F32)],
        compiler_params=_cparams(("arbitrary",)),
        name="swa_sample",
    )(x, res, ng.reshape(1, d), w_in.array, w_out.array, kcache, vcache, cos, sin, qng, kng,
      jnp.broadcast_to(sinks[:, None], (HQ_D, 128)))


def _odd_layer_body(x_ref, ng_ref, win_ref, wout_ref, lbl_ref, ong_ref, cos_ref, sin_ref, qng_ref, kng_ref, sink_ref,
                    y_ref, sout_ref, kout_ref, vout_ref, st_ref, kprev_ref, vprev_ref, *, c, pb, layer):
    step = pl.program_id(1)

    @pl.when(step == 0)
    def _():
        st_ref[...] = jnp.zeros_like(st_ref)
        kprev_ref[...] = jnp.zeros_like(kprev_ref)
        vprev_ref[...] = jnp.zeros_like(vprev_ref)

    x = x_ref[...]
    h = _rms(x, ng_ref[...]).astype(BF16)
    lb = _hgrn_lower_bound(lbl_ref[...], layer)
    ngrp = x.shape[0] // pb
    project = lambda g: jnp.dot(h[g * pb:(g + 1) * pb, :], win_ref[...], preferred_element_type=F32)
    proj = project(0)
    k_last, v_last = kprev_ref[...], vprev_ref[...]
    k_prev, v_prev = _swa_place(k_last), _swa_place(v_last)
    for g in range(ngrp):
        nxt = project(g + 1) if g + 1 < ngrp else None
        grows = slice(g * pb, (g + 1) * pb)
        qn, kn = _swa_prep(proj[:, OD_QD:OD_KD], proj[:, OD_KD:OD_VD], cos_ref[grows, :], sin_ref[grows, :],
                           qng_ref, kng_ref)
        staged = []
        for u in range(pb // WINDOW):
            us = slice(u * WINDOW, (u + 1) * WINDOW)
            k_last, v_last = kn[us, :], proj[us, OD_VD:OD_END]
            k_cur, v_cur = _swa_place(k_last), _swa_place(v_last)
            staged.append(_hgrn_tile(proj[us, :], lb, ong_ref, st_ref, c))
            staged.append(_swa_block(qn[us, :], k_prev, k_cur, v_prev, v_cur, sink_ref,
                                     True if g + u > 0 else step > 0))
            k_prev, v_prev = k_cur, v_cur
        outs = _interleave(*staged)
        o = jnp.concatenate([jnp.concatenate(outs[0::2], axis=0), jnp.concatenate(outs[1::2], axis=0)], axis=1)
        y_ref[grows, :] = x[grows, :] + jnp.dot(o.astype(BF16), wout_ref[...], preferred_element_type=F32)
        proj = nxt
    kprev_ref[...] = k_last
    vprev_ref[...] = v_last
    kout_ref[0] = k_last
    vout_ref[0] = v_last

    @pl.when(step == pl.num_programs(1) - 1)
    def _():
        sout_ref[0] = st_ref[...]


def _odd_layer(x, pos, ng, w_in, w_out, lb_logits, on_g, qn_g, kn_g, sinks, *, n_seq, r, pb, layer):
    t, d = x.shape
    seq_len = pos.shape[0]
    c = math.gcd(seq_len, HGRN_CHUNK)
    assert WINDOW % c == 0 and pb % WINDOW == 0 and r % pb == 0 and seq_len % r == 0
    steps = seq_len // r
    cos, sin = _rope_tables(pos, ROT_DIM_D, ROPE_THETA, HD_D, 2)
    qng, kng = _swa_gains(qn_g, kn_g)
    rowb = pl.BlockSpec((r, d), lambda b, i: (b * steps + i, 0))
    tab = pl.BlockSpec((r, 128), lambda b, i: (i, 0))
    st_spec = pl.BlockSpec((1, H_C, DK_C, DV_C), lambda b, i: (b, 0, 0, 0))
    kv_out = pl.BlockSpec((1, WINDOW, 128), lambda b, i: (b, 0, 0))
    return pl.pallas_call(
        functools.partial(_odd_layer_body, c=c, pb=pb, layer=layer),
        grid=(n_seq, steps),
        in_specs=[rowb, _resident((1, d)), w_in.spec(), w_out.spec(), _resident((DEPTH, 512)),
                  _resident((1, 512)), tab, tab, _resident((1, 512)), _resident((1, 128)), _resident((HQ_D, 128))],
        out_specs=[rowb, st_spec, kv_out, kv_out],
        out_shape=[jax.ShapeDtypeStruct((t, d), F32), jax.ShapeDtypeStruct((n_seq, H_C, DK_C, DV_C), F32),
                   jax.ShapeDtypeStruct((n_seq, WINDOW, 128), F32), jax.ShapeDtypeStruct((n_seq, WINDOW, 128), F32)],
        scratch_shapes=[pltpu.VMEM((H_C, DK_C, DV_C), F32), pltpu.VMEM((WINDOW, 128), F32),
                        pltpu.VMEM((WINDOW, 128), F32)],
        compiler_params=_cparams(("arbitrary", "arbitrary")),
        name="odd_layer",
    )(x, ng.reshape(1, d), w_in.array, w_out.array, lb_logits, on_g.reshape(1, 512), cos, sin, qng, kng,
      jnp.broadcast_to(sinks[:, None], (HQ_D, 128)))


def _mem_kv_body(m_ref, w_ref, g_ref, k_ref, v_ref):
    kv = jnp.dot(m_ref[...].astype(BF16), w_ref[...], preferred_element_type=F32)
    hw = H_X * HD_X
    for h in range(H_X):
        hs = slice(h * HD_X, (h + 1) * HD_X)
        k_ref[:, hs] = _rms(kv[:, hs], g_ref[...])
    v_ref[...] = kv[:, hw:]


def _mem_kv(mem, w, g):
    t, d = mem.shape
    hw = H_X * HD_X
    whole = lambda shape: pl.BlockSpec(shape, lambda i: (0,) * len(shape))
    return pl.pallas_call(
        _mem_kv_body,
        grid=(1,),
        in_specs=[whole((t, d)), w.spec(), whole((1, HD_X))],
        out_specs=[whole((t, hw)), whole((t, hw))],
        out_shape=[jax.ShapeDtypeStruct((t, hw), F32), jax.ShapeDtypeStruct((t, hw), F32)],
        compiler_params=_cparams(("arbitrary",)),
        name="mem_kv",
    )(mem, w.array, g.reshape(1, HD_X))


def _mem_attend_rows(x, g_ref, wq_ref, qng_ref, mk_ref, mv_ref, wo_ref):
    heads = [slice(h * HD_X, (h + 1) * HD_X) for h in range(H_X)]
    q = jnp.dot(_rms(x, g_ref[...]).astype(BF16), wq_ref[...], preferred_element_type=F32)
    yield None
    scores = [_dot_nt(_rms(q[:, hs], qng_ref[...]), mk_ref[0, :, hs]) for hs in heads]
    yield None
    outs = []
    for hs, sc in zip(heads, scores):
        sc = sc * (HD_X ** -0.5)
        p = jnp.exp(sc - jnp.max(sc, axis=-1, keepdims=True))
        outs.append(_dot(p, mv_ref[0, :, hs]) / jnp.sum(p, axis=-1, keepdims=True))
    yield None
    yield x + jnp.dot(jnp.concatenate(outs, axis=1).astype(BF16), wo_ref[...], preferred_element_type=F32)


def _mem_attend_body(x_ref, g_ref, wq_ref, qng_ref, mk_ref, mv_ref, wo_ref, y_ref, *, nsplit):
    rows = x_ref.shape[0] // nsplit
    parts = _interleave(*[_mem_attend_rows(x_ref[i * rows:(i + 1) * rows, :], g_ref, wq_ref, qng_ref, mk_ref, mv_ref,
                                           wo_ref) for i in range(nsplit)])
    for i, y in enumerate(parts):
        y_ref[i * rows:(i + 1) * rows, :] = y


def _mem_attend(x, g, wq, qn_g, mk, mv, wo, *, tm, nsplit, tiles_per_mem):
    t, d = x.shape
    hw = H_X * HD_X
    mem_spec = pl.BlockSpec((1, N_MEM, hw), lambda i: (i // tiles_per_mem, 0, 0))
    return pl.pallas_call(
        functools.partial(_mem_attend_body, nsplit=nsplit),
        grid=(t // tm,),
        in_specs=[pl.BlockSpec((tm, d), lambda i: (i, 0)), _resident((1, d)), wq.spec(),
                  _resident((1, HD_X)), mem_spec, mem_spec, wo.spec()],
        out_specs=pl.BlockSpec((tm, d), lambda i: (i, 0)),
        out_shape=jax.ShapeDtypeStruct((t, d), F32),
        compiler_params=_cparams(("arbitrary",)),
        name="mem_attend",
    )(x, g.reshape(1, d), wq.array, qn_g.reshape(1, HD_X), mk, mv, wo.array)


def _mem_attend_cached_body(x_ref, g_ref, wq_ref, qng_ref, mk_ref, mv_ref, wo_ref, y_ref, att_ref, *, nseq):
    x = x_ref[...]
    q = jnp.dot(_rms(x, g_ref[...]).astype(BF16), wq_ref[...], preferred_element_type=F32)
    l = x.shape[0] // nseq
    rows_all = H_X * l
    cols = N_MEM * H_X
    own = (lax.broadcasted_iota(jnp.int32, (rows_all, cols), 1) % H_X
           == lax.broadcasted_iota(jnp.int32, (rows_all, cols), 0) // l)
    scores = []
    for s in range(nseq):
        rs = slice(s * l, (s + 1) * l)
        qs = jnp.concatenate([_rms(q[rs, h * HD_X:(h + 1) * HD_X], qng_ref[...]) for h in range(H_X)], axis=0)
        scores.append(_dot_nt(qs, mk_ref[0, s]))
    for s in range(nseq):
        rs = slice(s * l, (s + 1) * l)
        sc = jnp.where(own, scores[s] * (HD_X ** -0.5), -jnp.inf)
        m = jnp.max(sc, axis=-1, keepdims=True)
        p = jnp.exp(sc - m)
        o = _dot(p, mv_ref[0, s]) / jnp.sum(p, axis=-1, keepdims=True)
        for h in range(H_X):
            att_ref[rs, h * HD_X:(h + 1) * HD_X] = o[h * l:(h + 1) * l, :]
    y_ref[...] = x + jnp.dot(att_ref[...].astype(BF16), wo_ref[...], preferred_element_type=F32)


def _mem_attend_cached(x, g, wq, qn_g, mk, mv, wo, *, layer, nseq):
    t, d = x.shape
    hw = H_X * HD_X
    tm = nseq * (t // mk.shape[1])
    full = lambda shape: pl.BlockSpec(shape, lambda i: (0,) * len(shape))
    mem_spec = pl.BlockSpec((1, nseq, N_MEM * H_X, HD_X), lambda i: (layer, i, 0, 0))
    return pl.pallas_call(
        functools.partial(_mem_attend_cached_body, nseq=nseq),
        grid=(t // tm,),
        in_specs=[pl.BlockSpec((tm, d), lambda i: (i, 0)), full((1, d)), wq.spec(), full((1, HD_X)),
                  mem_spec, mem_spec, wo.spec()],
        out_specs=pl.BlockSpec((tm, d), lambda i: (i, 0)),
        out_shape=jax.ShapeDtypeStruct((t, d), F32),
        scratch_shapes=[pltpu.VMEM((tm, hw), F32)],
        compiler_params=_cparams(("arbitrary",)),
        name="mem_attend_cached",
    )(x, g.reshape(1, d), wq.array, qn_g.reshape(1, HD_X), mk, mv, wo.array)


def _ffn_body(*refs, tm, ffc, tiles_per_seq, per_row_state):
    if per_row_state:
        x_ref, g_ref, wg_ref, wu_ref, cw_ref, cb_ref, wd_ref, c0_ref, y_ref, gt_ref, tail_ref = refs
    else:
        x_ref, g_ref, wg_ref, wu_ref, cw_ref, cb_ref, wd_ref, y_ref, gt_ref, tail_ref = refs
    ff = wg_ref.shape[1]
    x = x_ref[...]
    h = _rms(x, g_ref[...]).astype(BF16)
    hist = CONV_W - 1
    if per_row_state:
        ns = tm // 8
        r_in = lax.broadcasted_iota(jnp.int32, (tm, hist * ns), 0)
        c_in = lax.broadcasted_iota(jnp.int32, (tm, hist * ns), 1)
        same = (r_in // 8) == (c_in // hist)
        sel1 = (same & (r_in % 8 == 0) & (c_in % hist == hist - 1)).astype(BF16)
        sel2 = (same & (r_in % 8 < hist) & (c_in % hist == r_in % 8)).astype(BF16)
        r_out = lax.broadcasted_iota(jnp.int32, (hist * ns, tm), 0)
        c_out = lax.broadcasted_iota(jnp.int32, (hist * ns, tm), 1)
        sel_out = (((c_out // 8) == (r_out // hist)) & (c_out % 8 == 8 - hist + r_out % hist)).astype(BF16)

        def select(sel, val):
            return sum(jnp.dot(sel, piece, preferred_element_type=F32) for piece in _split3(val))
    else:
        @pl.when(pl.program_id(0) % tiles_per_seq == 0)
        def _():
            tail_ref[...] = jnp.zeros_like(tail_ref)

    row8 = lax.broadcasted_iota(jnp.int32, (8, ffc), 0)
    acc = None

    def down(acts):
        a = acts[0][0] if len(acts) == 1 else jnp.concatenate([p[0] for p in acts], axis=1)
        part = jnp.dot(a, wd_ref[acts[0][1].start:acts[-1][1].stop, :], preferred_element_type=F32)
        return part if acc is None else acc + part

    filling, ready = [], None
    for c0 in range(0, ff, ffc):
        cs = slice(c0, c0 + ffc)
        gate = jnp.dot(h, wg_ref[:, cs], preferred_element_type=F32)
        up = jnp.dot(h, wu_ref[:, cs], preferred_element_type=F32)
        if ready is not None:
            acc = down(ready)
            ready = None
        r1 = pltpu.roll(gate, 1, 0)
        r2 = pltpu.roll(gate, 2, 0)
        if per_row_state:
            pos = lax.broadcasted_iota(jnp.int32, gate.shape, 0) % 8
            c0c = c0_ref[:, cs]
            g1 = jnp.where(pos == 0, select(sel1, c0c), r1)
            g2 = jnp.where(pos < hist, select(sel2, c0c), r2)
            gt_ref[:, cs] = select(sel_out, gate)
        else:
            prev = tail_ref[:, cs]
            top1 = jnp.where(row8 == 0, prev[7:8, :], r1[0:8, :])
            top2 = jnp.where(row8 == 0, prev[6:7, :], jnp.where(row8 == 1, prev[7:8, :], r2[0:8, :]))
            g1 = jnp.concatenate([top1, r1[8:, :]], axis=0)
            g2 = jnp.concatenate([top2, r2[8:, :]], axis=0)
            tail_ref[:, cs] = gate[tm - 8:, :]
            gt_ref[:, cs] = gate[tm - 8:, :]
        conv = cb_ref[:, cs] + cw_ref[0:1, cs] * g2 + cw_ref[1:2, cs] * g1 + cw_ref[2:3, cs] * gate
        filling.append(((_gelu(conv) * up).astype(BF16), cs))
        if len(filling) == FFN_DOWN_GROUP:
            filling, ready = [], filling
    for acts in (ready, filling):
        if acts:
            acc = down(acts)
    y_ref[...] = x + acc


def _ffn(x, g, wg, wu, cw, cb, wd, *, tm, ffc, tiles_per_seq, c0=None):
    t, d = x.shape
    ff = wg.shape[1]
    per_row_state = c0 is not None
    tail_n = tm // 8 * (CONV_W - 1) if per_row_state else 8
    in_specs = [pl.BlockSpec((tm, d), lambda i: (i, 0)), _resident((1, d)), wg.spec(), wu.spec(),
                _resident((8, ff)), _resident((1, ff)), wd.spec()]
    args = [x, g.reshape(1, d), wg.array, wu.array, jnp.pad(cw, ((0, 8 - CONV_W), (0, 0))), cb.reshape(1, ff),
            wd.array]
    if per_row_state:
        in_specs += [pl.BlockSpec((tail_n, ff), lambda i: (i, 0))]
        args += [c0]
    return pl.pallas_call(
        functools.partial(_ffn_body, tm=tm, ffc=ffc, tiles_per_seq=tiles_per_seq, per_row_state=per_row_state),
        grid=(t // tm,),
        in_specs=in_specs,
        out_specs=[pl.BlockSpec((tm, d), lambda i: (i, 0)), pl.BlockSpec((tail_n, ff), lambda i: (i, 0))],
        out_shape=[jax.ShapeDtypeStruct((t, d), F32), jax.ShapeDtypeStruct((t // tm * tail_n, ff), F32)],
        scratch_shapes=[pltpu.VMEM((8, ff), F32)],
        compiler_params=_cparams(("arbitrary",)),
        name="ffn",
    )(*args)


def _run_prompt(x, pos, n_seq, mem, w):
    seq_len = pos.shape[0]
    tm = 512
    hw = H_X * HD_X
    out = {k: [] for k in ("ret", "hgrn", "swa_k", "swa_v", "mem_k", "mem_v", "conv")}
    for l in range(DEPTH):
        j = l // 2
        if l % 2 == 0:
            x, s_new = _even_layer(x, pos, w["norm_mix_g"][l], w["ev_w_in"][j], w["ev_w_out"][j], w["ret_gn_g"][j],
                                   w["mlp_norm_g"][j], w["mlp_w_s"][j], w["mlp_b_s"][j], n_seq=n_seq, r=1024, pb=256)
            out["ret"].append(s_new)
        else:
            x, s_new, k_new, v_new = _odd_layer(x, pos, w["norm_mix_g"][l], w["od_w_in"][j], w["od_w_out"][j],
                                                w["hgrn_lb_logits"], w["hgrn_onorm_g"][j], w["swa_qnorm_g"][j],
                                                w["swa_knorm_g"][j], w["swa_sinks"][j], n_seq=n_seq, r=1024, pb=256,
                                                layer=l)
            out["hgrn"].append(s_new)
            out["swa_k"].append(k_new.reshape(n_seq, WINDOW, HKV_D, HD_D))
            out["swa_v"].append(v_new.reshape(n_seq, WINDOW, HKV_D, HD_D))
        mk, mv = _mem_kv(mem, w["mem_w_kv"][l], w["mem_knorm_g"][l])
        out["mem_k"].append(mk.reshape(n_seq, N_MEM, H_X, HD_X))
        out["mem_v"].append(mv.reshape(n_seq, N_MEM, H_X, HD_X))
        x = _mem_attend(x, w["norm_mem_g"][l], w["mem_w_q"][l], w["mem_qnorm_g"][l], mk.reshape(n_seq, N_MEM, hw),
                        mv.reshape(n_seq, N_MEM, hw), w["mem_w_o"][l], tm=2 * tm, nsplit=4,
                        tiles_per_mem=seq_len // (2 * tm))
        x, gt = _ffn(x, w["norm_ffn_g"][l], w["ffn_w_gate"][l], w["ffn_w_up"][l], w["ffn_conv_w"][l],
                     w["ffn_conv_b"][l], w["ffn_w_down"][l], tm=tm, ffc=256, tiles_per_seq=seq_len // tm)
        out["conv"].append(gt.reshape(n_seq, seq_len // tm, 8, D_FF)[:, -1, 8 - (CONV_W - 1):, :])
    return x, {name: jnp.stack(rows) for name, rows in out.items()}


def _run_sample(x, pos, n_seq, st, w):
    seq_len = pos.shape[0]
    assert seq_len == 8
    t = x.shape[0]
    out = {k: [] for k in ("ret", "chunk_v", "hgrn", "swa_k", "swa_v", "conv")}
    for l in range(DEPTH):
        j = l // 2
        if l % 2 == 0:
            x, s_new, v_rows = _even_mixer(x, pos, w["norm_mix_g"][l], w["ev_w_in"][j], w["ev_w_out"][j],
                                           st["ret"][j], w["ret_gn_g"][j], w["mlp_norm_g"][j], w["mlp_w_s"][j],
                                           w["mlp_b_s"][j], nseg=16)
            out["chunk_v"].append(v_rows.reshape(n_seq, seq_len, G_B, DG_B))
            out["ret"].append(s_new)
        else:
            x_mid, s_new = _hgrn_mixer(x, w["norm_mix_g"][l], w["od_w_in"][j], w["od_w_out"][j], w["hgrn_lb_logits"],
                                       w["hgrn_onorm_g"][j], st["hgrn"][j], c=seq_len, nseq=16, layer=l)
            x, k_new, v_new = _swa_sample(x, x_mid, pos, w["norm_mix_g"][l], w["od_w_in"][j], w["od_w_out"][j],
                                          jnp.transpose(st["swa_k"][j], (0, 2, 3, 1)),
                                          jnp.transpose(st["swa_v"][j], (0, 2, 3, 1)), w["swa_qnorm_g"][j],
                                          w["swa_knorm_g"][j], w["swa_sinks"][j], nseq=16, group=8)
            out["hgrn"].append(s_new)
            out["swa_k"].append(jnp.transpose(k_new, (0, 3, 1, 2)))
            out["swa_v"].append(jnp.transpose(v_new, (0, 3, 1, 2)))
        x = _mem_attend_cached(x, w["norm_mem_g"][l], w["mem_w_q"][l], w["mem_qnorm_g"][l],
                               st["mem_k"].reshape(DEPTH, n_seq, N_MEM * H_X, HD_X),
                               st["mem_v"].reshape(DEPTH, n_seq, N_MEM * H_X, HD_X), w["mem_w_o"][l], layer=l, nseq=8)
        c0 = st["conv"][l].reshape(n_seq * (CONV_W - 1), D_FF)
        x, gt = _ffn(x, w["norm_ffn_g"][l], w["ffn_w_gate"][l], w["ffn_w_up"][l], w["ffn_conv_w"][l],
                     w["ffn_conv_b"][l], w["ffn_w_down"][l], tm=512, ffc=256, tiles_per_seq=1, c0=c0)
        out["conv"].append(gt.reshape(n_seq, CONV_W - 1, D_FF))
    return x, {name: jnp.stack(rows) for name, rows in out.items()}


def kernel(x_prompt, x_sample, state_ret, state_hgrn, cache_swa_k, cache_swa_v, cache_mem_k, cache_mem_v,
           state_ffn_conv, mem_prompt, norm_mix_g, norm_mem_g, norm_ffn_g, ev_w_in, ev_w_out, ret_gn_g,
           mlp_norm_g, mlp_w_s, mlp_b_s, od_w_in, od_w_out, hgrn_lb_logits, hgrn_onorm_g, swa_qnorm_g,
           swa_knorm_g, swa_sinks, mem_w_q, mem_w_kv, mem_qnorm_g, mem_knorm_g, mem_w_o, ffn_w_gate,
           ffn_w_up, ffn_conv_w, ffn_conv_b, ffn_w_down):
    def bf(a):
        stacked = a.astype(BF16)
        return [_Layer(stacked, i) for i in range(a.shape[0])]

    w = dict(norm_mix_g=norm_mix_g, norm_mem_g=norm_mem_g, norm_ffn_g=norm_ffn_g, ev_w_in=bf(ev_w_in),
             ev_w_out=bf(ev_w_out), ret_gn_g=ret_gn_g, mlp_norm_g=mlp_norm_g, mlp_w_s=mlp_w_s, mlp_b_s=mlp_b_s,
             od_w_in=bf(od_w_in), od_w_out=bf(od_w_out), hgrn_lb_logits=hgrn_lb_logits, hgrn_onorm_g=hgrn_onorm_g,
             swa_qnorm_g=swa_qnorm_g, swa_knorm_g=swa_knorm_g, swa_sinks=swa_sinks, mem_w_q=bf(mem_w_q),
             mem_w_kv=bf(mem_w_kv), mem_qnorm_g=mem_qnorm_g, mem_knorm_g=mem_knorm_g, mem_w_o=bf(mem_w_o),
             ffn_w_gate=bf(ffn_w_gate), ffn_w_up=bf(ffn_w_up), ffn_conv_w=ffn_conv_w, ffn_conv_b=ffn_conv_b,
             ffn_w_down=bf(ffn_w_down))
    b, seq, d = x_prompt.shape
    db, dseq, _ = x_sample.shape
    pos_prompt = np.arange(seq)
    pos_sample = PAST_LEN + np.arange(dseq)
    y_p, ns_p = _run_prompt(x_prompt.reshape(b * seq, d), pos_prompt, b, mem_prompt.reshape(b * N_MEM, d), w)
    st = dict(ret=state_ret, hgrn=state_hgrn, swa_k=cache_swa_k, swa_v=cache_swa_v, mem_k=cache_mem_k,
              mem_v=cache_mem_v, conv=state_ffn_conv)
    y_s, ns_s = _run_sample(x_sample.reshape(db * dseq, d), pos_sample, db, st, w)
    return (y_p.reshape(b, seq, d), y_s.reshape(db, dseq, d), ns_p["ret"], ns_p["hgrn"], ns_p["swa_k"], ns_p["swa_v"],
            ns_p["mem_k"], ns_p["mem_v"], ns_p["conv"], ns_s["ret"], ns_s["chunk_v"], ns_s["hgrn"], ns_s["swa_k"],
            ns_s["swa_v"], ns_s["conv"])
```

```python
import functools
import math

import jax
import jax.numpy as jnp
import numpy as np
from jax import lax
from jax.experimental import pallas as pl
from jax.experimental.pallas import tpu as pltpu

F32 = jnp.float32
BF16 = jnp.bfloat16

D_MODEL = 1024
DEPTH = 2
PAST_LEN = 16384
H_A, DV_A, DK_A = 4, 128, 64
RET_CHUNK = 128
RET_THETA = 10000.0
G_B, DG_B = 4, 128
MLP_CHUNK = 128
H_C, DK_C, DV_C = 4, 128, 128
HGRN_CHUNK = 64
HD_D, HQ_D, HKV_D = 64, 8, 2
WINDOW = 128
ROPE_THETA = 500000.0
ROT_DIM_D = HD_D // 4
N_MEM, H_X, HD_X = 256, 4, 128
D_FF = 2816
CONV_W = 3
FFN_DOWN_GROUP = 6
EPS = 1e-6
SQRT_HALF = float(np.sqrt(0.5))
EV_Q, EV_K, EV_V, EV_G, EV_U, EV_VB, EV_END = 0, 256, 512, 1024, 1536, 2048, 2560
OD_Q, OD_F, OD_I, OD_G, OD_QD, OD_KD, OD_VD, OD_END = 0, 512, 1024, 1536, 2048, 2560, 2688, 2816

VMEM_LIMIT_BYTES = 56 * 1024 * 1024


def _cparams(sem):
    return pltpu.CompilerParams(dimension_semantics=sem, vmem_limit_bytes=VMEM_LIMIT_BYTES)


def _resident(shape):
    return pl.BlockSpec(shape, lambda *_: (0,) * len(shape), pipeline_mode=pl.Buffered(1))


class _Layer:
    def __init__(self, array, index):
        self.array, self.index, self.shape = array, index, array.shape[1:]

    def spec(self):
        index, zeros = self.index, (0,) * len(self.shape)
        return pl.BlockSpec((None,) + self.shape, lambda *_: (index,) + zeros, pipeline_mode=pl.Buffered(1))


def _rms(x, g):
    return x * lax.rsqrt(jnp.mean(x * x, axis=-1, keepdims=True) + EPS) * g


def _ln(x, g):
    mu = jnp.mean(x, axis=-1, keepdims=True)
    xc = x - mu
    return xc * lax.rsqrt(jnp.mean(xc * xc, axis=-1, keepdims=True) + EPS) * g


def _sigmoid(x):
    return 1.0 / (1.0 + jnp.exp(-x))


def _silu(x):
    return x * _sigmoid(x)


def _gelu(x):
    return 0.5 * x * (1.0 + lax.erf(x * SQRT_HALF))


def _dot(a, b):
    return jnp.dot(a.astype(BF16), b.astype(BF16), preferred_element_type=F32)


def _dot_nt(a, b):
    return lax.dot_general(a.astype(BF16), b.astype(BF16), (((1,), (1,)), ((), ())), preferred_element_type=F32)


def _dot_tn(a, b):
    return lax.dot_general(a.astype(BF16), b.astype(BF16), (((0,), (0,)), ((), ())), preferred_element_type=F32)


def _split3(x):
    hi = x.astype(BF16)
    r1 = x - hi.astype(F32)
    mid = r1.astype(BF16)
    lo = (r1 - mid.astype(F32)).astype(BF16)
    return hi, mid, lo


def _retention_consts(c, nseq=1):
    lg = np.log1p(-np.exp2(-5.0 - np.arange(H_A, dtype=np.float64)))
    idx = np.arange(c, dtype=np.float64)
    rel = idx[:, None] - idx[None, :]
    dmask = np.where(rel >= 0, np.exp(rel[None] * lg[:, None, None]), 0.0)
    q_dec = np.exp((idx + 1.0)[None, :] * lg[:, None])
    k_dec = np.exp((c - 1.0 - idx)[None, :] * lg[:, None])
    c_dec = np.exp(c * lg)
    qdec = np.tile(np.repeat(q_dec.T, DK_A, axis=1), (nseq, 1))
    kdec = np.tile(np.repeat(k_dec.T, DK_A, axis=1), (nseq, 1))
    dmask = np.stack([np.kron(np.eye(nseq), dmask[h]) for h in range(H_A)])
    cdec = np.broadcast_to(c_dec[:, None, None], (H_A, 8, DV_A))
    return tuple(jnp.asarray(a, F32) for a in (dmask, qdec, kdec, cdec))


def _rope_tables(pos, rot_dim, theta, head_dim, reps, row_reps=1):
    half = rot_dim // 2
    inv = theta ** (-np.arange(half, dtype=np.float64) / half)
    ang = pos.astype(np.float64)[:, None] * inv[None, :]
    cos = np.cos(ang)
    sin = np.sin(ang)
    l = pos.shape[0]
    pad = head_dim - rot_dim
    cos_h = np.concatenate([cos, cos, np.ones((l, pad))], axis=1)
    sin_h = np.concatenate([-sin, sin, np.zeros((l, pad))], axis=1)
    return (jnp.asarray(np.tile(cos_h, (row_reps, reps)), F32), jnp.asarray(np.tile(sin_h, (row_reps, reps)), F32))


def _even_consts(c, gn_g, mn_g, w_s, b_s, nseq=1):
    dmask, qdec, kdec, cdec = _retention_consts(c, nseq)
    bs_full = jnp.tile(jnp.repeat(b_s[:, :c].T, DG_B, axis=1), (nseq, 1))
    arrays = [dmask, qdec, kdec, cdec, gn_g.reshape(1, H_A * DV_A), mn_g.reshape(1, G_B * DG_B),
              jnp.tile(w_s[:, :c, :c], (1, nseq, nseq)), bs_full]
    return arrays, [a.shape for a in arrays]


def _retention_state_batched(qd, kd, v, s0_ref, cdec_ref, nseq):
    n = qd.shape[0]
    own = (lax.broadcasted_iota(jnp.int32, (n, nseq * 128), 1) // 128
           == lax.broadcasted_iota(jnp.int32, (n, nseq * 128), 0) // (n // nseq))
    place = lambda x: jnp.where(own, jnp.concatenate([x] * nseq, axis=1), 0.0)
    inter, s_new = [None] * H_A, [None] * H_A
    for p in range(H_A // 2):
        h0, h1 = 2 * p, 2 * p + 1
        lanes = slice(p * 128, (p + 1) * 128)
        s_a, s_b = s0_ref[:, h0], s0_ref[:, h1]
        z = jnp.zeros_like(s_a)
        slab = jnp.concatenate([jnp.concatenate([s_a, z], axis=2), jnp.concatenate([z, s_b], axis=2)], axis=1)
        res = _dot(place(qd[:, lanes]), slab.reshape(nseq * 2 * DK_A, 2 * DV_A))
        inter[h0], inter[h1] = res[:, :DV_A], res[:, DV_A:]
        upd = _dot_tn(place(kd[:, lanes]), v[:, 2 * p * DV_A:2 * (p + 1) * DV_A]).reshape(nseq, 2 * DK_A, 2 * DV_A)
        s_new[h0] = cdec_ref[h0, 0:1, :] * s_a + upd[:, :DK_A, :DV_A]
        s_new[h1] = cdec_ref[h1, 0:1, :] * s_b + upd[:, DK_A:, DV_A:]
    return inter, s_new


def _even_segment(pj, cos, sin, consts, s_old, nseq=1):
    dmask_ref, qdec_ref, kdec_ref, cdec_ref, gng_ref, mng_ref, ws_ref, bs_ref = consts
    c = pj.shape[0]
    width = H_A * DK_A
    lane = lax.broadcasted_iota(jnp.int32, (c, width), 1)
    first_half = (lane % DK_A) < (DK_A // 2)
    row = lax.broadcasted_iota(jnp.int32, (c, c), 0)
    col = lax.broadcasted_iota(jnp.int32, (c, c), 1)
    tril = (col <= row) & (row // (c // nseq) == col // (c // nseq))

    def rope(x):
        rot = jnp.where(first_half, pltpu.roll(x, width - DK_A // 2, 1), pltpu.roll(x, DK_A // 2, 1))
        return x * cos + rot * sin

    q = rope(pj[:, EV_Q:EV_K])
    k = rope(pj[:, EV_K:EV_V]) * (DK_A ** -0.5)
    v = pj[:, EV_V:EV_G]
    ga = pj[:, EV_G:EV_U]
    qd = q * qdec_ref[...]
    kd = k * kdec_ref[...]
    ks = [slice(h * DK_A, (h + 1) * DK_A) for h in range(H_A)]
    vs = [slice(h * DV_A, (h + 1) * DV_A) for h in range(H_A)]
    qk = [_dot_nt(q[:, ks[h]], k[:, ks[h]]) for h in range(H_A)]
    if nseq == 1:
        inter = [_dot(qd[:, ks[h]], s_old[h]) for h in range(H_A)]
        s_new = [cdec_ref[h, 0:1, :] * s_old[h] + _dot_tn(kd[:, ks[h]], v[:, vs[h]]) for h in range(H_A)]
    else:
        inter, s_new = _retention_state_batched(qd, kd, v, s_old, cdec_ref, nseq)
    vg = _gelu(pj[:, EV_VB:EV_END])
    vrows = [_ln(vg[:, gs], mng_ref[:, gs]) for gs in vs]
    mixed = [_dot(jnp.where(tril, ws_ref[g], 0.0), vrows[g]) for g in range(G_B)]
    yield s_new
    outs = []
    for h in range(H_A):
        o = _dot(qk[h] * dmask_ref[h], v[:, vs[h]]) + inter[h]
        outs.append(_silu(ga[:, vs[h]]) * _ln(o, gng_ref[:, vs[h]]))
    u = _gelu(pj[:, EV_U:EV_VB])
    for g, gs in enumerate(vs):
        outs.append(u[:, gs] * (mixed[g] + bs_ref[:, gs]))
    yield jnp.concatenate(outs, axis=1), s_new, jnp.concatenate(vrows, axis=1)


def _even_body(x_ref, ng_ref, win_ref, wout_ref, cos_ref, sin_ref, dmask_ref, qdec_ref, kdec_ref, cdec_ref, gng_ref,
               mng_ref, ws_ref, bs_ref, s0_ref, y_ref, sout_ref, vrows_ref, *, nseg):
    consts = (dmask_ref, qdec_ref, kdec_ref, cdec_ref, gng_ref, mng_ref, ws_ref, bs_ref)
    x = x_ref[...]
    pj = jnp.dot(_rms(x, ng_ref[...]).astype(BF16), win_ref[...], preferred_element_type=F32)
    (o, s_new, vrows), = _interleave(_even_segment(pj, cos_ref[...], sin_ref[...], consts, s0_ref, nseq=nseg))
    for h in range(H_A):
        sout_ref[:, h] = s_new[h]
    y_ref[...] = x + jnp.dot(o.astype(BF16), wout_ref[...], preferred_element_type=F32)
    vrows_ref[...] = vrows


def _even_mixer(x, pos, ng, w_in, w_out, s0, gn_g, mn_g, w_s, b_s, *, nseg):
    t, d = x.shape
    c = pos.shape[0]
    r = nseg * c
    consts, shapes = _even_consts(c, gn_g, mn_g, w_s, b_s, nseq=nseg)
    cos, sin = _rope_tables(pos, DK_A, RET_THETA, DK_A, H_A, row_reps=nseg)
    rowb = pl.BlockSpec((r, d), lambda i: (i, 0))
    st_spec = pl.BlockSpec((nseg, H_A, DK_A, DV_A), lambda i: (i, 0, 0, 0))
    return pl.pallas_call(
        functools.partial(_even_body, nseg=nseg),
        grid=(t // r,),
        in_specs=[rowb, _resident((1, d)), w_in.spec(), w_out.spec(), _resident((r, 256)), _resident((r, 256))]
        + [_resident(s) for s in shapes] + [st_spec],
        out_specs=[rowb, st_spec, pl.BlockSpec((r, 512), lambda i: (i, 0))],
        out_shape=[jax.ShapeDtypeStruct((t, d), F32), jax.ShapeDtypeStruct(s0.shape, F32),
                   jax.ShapeDtypeStruct((t, 512), F32)],
        compiler_params=_cparams(("arbitrary",)),
        name="even_mixer",
    )(x, ng.reshape(1, d), w_in.array, w_out.array, cos, sin, *consts, s0)


def _even_layer_body(x_ref, ng_ref, win_ref, wout_ref, cos_ref, sin_ref, dmask_ref, qdec_ref, kdec_ref, cdec_ref,
                     gng_ref, mng_ref, ws_ref, bs_ref, y_ref, sout_ref, st_ref, *, c, pb):
    step = pl.program_id(1)

    @pl.when(step == 0)
    def _():
        st_ref[...] = jnp.zeros_like(st_ref)

    consts = (dmask_ref, qdec_ref, kdec_ref, cdec_ref, gng_ref, mng_ref, ws_ref, bs_ref)
    x = x_ref[...]
    h = _rms(x, ng_ref[...]).astype(BF16)
    ngrp = x.shape[0] // pb
    project = lambda g: jnp.dot(h[g * pb:(g + 1) * pb, :], win_ref[...], preferred_element_type=F32)
    proj = project(0)
    state = [st_ref[hh] for hh in range(H_A)]
    pending = None

    def finish(rows, seg):
        o, _, _ = next(seg)
        y_ref[rows, :] = x[rows, :] + jnp.dot(o.astype(BF16), wout_ref[...], preferred_element_type=F32)

    for g in range(ngrp):
        nxt = project(g + 1) if g + 1 < ngrp else None
        for j in range(pb // c):
            rows = slice(g * pb + j * c, g * pb + (j + 1) * c)
            seg = _even_segment(proj[j * c:(j + 1) * c, :], cos_ref[rows, :], sin_ref[rows, :], consts, state)
            state = next(seg)
            if pending is not None:
                finish(*pending)
            pending = (rows, seg)
        proj = nxt
    finish(*pending)
    for hh in range(H_A):
        st_ref[hh] = state[hh]

    @pl.when(step == pl.num_programs(1) - 1)
    def _():
        sout_ref[0] = st_ref[...]


def _even_layer(x, pos, ng, w_in, w_out, gn_g, mn_g, w_s, b_s, *, n_seq, r, pb):
    t, d = x.shape
    seq_len = pos.shape[0]
    c = math.gcd(seq_len, RET_CHUNK)
    assert c == min(MLP_CHUNK, seq_len) and pb % c == 0 and r % pb == 0 and seq_len % r == 0
    steps = seq_len // r
    consts, shapes = _even_consts(c, gn_g, mn_g, w_s, b_s)
    cos, sin = _rope_tables(pos, DK_A, RET_THETA, DK_A, H_A)
    rowb = pl.BlockSpec((r, d), lambda b, i: (b * steps + i, 0))
    tab = pl.BlockSpec((r, 256), lambda b, i: (i, 0))
    st_spec = pl.BlockSpec((1, H_A, DK_A, DV_A), lambda b, i: (b, 0, 0, 0))
    return pl.pallas_call(
        functools.partial(_even_layer_body, c=c, pb=pb),
        grid=(n_seq, steps),
        in_specs=[rowb, _resident((1, d)), w_in.spec(), w_out.spec(), tab, tab]
        + [_resident(s) for s in shapes],
        out_specs=[rowb, st_spec],
        out_shape=[jax.ShapeDtypeStruct((t, d), F32), jax.ShapeDtypeStruct((n_seq, H_A, DK_A, DV_A), F32)],
        scratch_shapes=[pltpu.VMEM((H_A, DK_A, DV_A), F32)],
        compiler_params=_cparams(("arbitrary", "arbitrary")),
        name="even_layer",
    )(x, ng.reshape(1, d), w_in.array, w_out.array, cos, sin, *consts)


def _hgrn_lower_bound(lbl, layer):
    e = jnp.exp(lbl - jnp.max(lbl, axis=0, keepdims=True))
    sm = e / jnp.sum(e, axis=0, keepdims=True)
    acc = sm[0:1, :]
    first = acc
    for i in range(1, layer + 1):
        acc = acc + sm[i:i + 1, :]
    return acc - first


def _block_row(x, blk, j):
    n, lanes = x.shape
    if blk % 16 == 0:
        return jnp.concatenate(
            [jnp.broadcast_to(x[i * blk + j:i * blk + j + 1, :], (blk, lanes)) for i in range(n // blk)], axis=0)
    x3 = x.reshape(n // 8, 8, lanes)
    pick = lambda jj: jnp.broadcast_to(x3[:, jj:jj + 1, :], x3.shape).reshape(n, lanes)
    if blk == 8:
        return pick(j)
    assert blk == 4
    upper = (lax.broadcasted_iota(jnp.int32, (n, 1), 0) % 8) >= 4
    return jnp.where(upper, pick(4 + j), pick(j))


def _hgrn_gates(pj, lb):
    f = lb + (1.0 - lb) * _sigmoid(pj[:, OD_F:OD_I])
    return f, jnp.log(f), 1.0 - f, _silu(pj[:, OD_Q:OD_F]), pj[:, OD_I:OD_G], pj[:, OD_G:OD_QD]


def _chunk_cumsum(g, c):
    n = g.shape[0]
    row = lax.broadcasted_iota(jnp.int32, (n, n), 0)
    col = lax.broadcasted_iota(jnp.int32, (n, n), 1)
    tri = ((col <= row) & (row // c == col // c)).astype(BF16)
    return sum(jnp.dot(tri, piece, preferred_element_type=F32) for piece in _split3(g))


def _hgrn_intra(q, k, f, b, c):
    return _hgrn_assemble(_hgrn_level_products(q, k, f, b, c))


def _hgrn_level_products(q, k, f, b, c):
    n = q.shape[0]
    rloc = lax.broadcasted_iota(jnp.int32, (n, 1), 0)
    diag = jnp.sum(q * k, axis=-1, keepdims=True)
    prods = []
    blk = 2
    while blk <= c:
        second = (rloc % blk) >= blk // 2
        if blk == 2:
            e = f
            ke = k
        else:
            dq = b - _block_row(b, blk, blk // 2 - 1)
            e = jnp.exp(jnp.where(second, dq, -dq))
            ke = k * e
        prods.append((blk, _dot_nt(jnp.where(second, q * e, 0.0), jnp.where(second, 0.0, ke))))
        blk *= 2
    return diag, prods


def _hgrn_assemble(level_products):
    diag, prods = level_products
    n = diag.shape[0]
    row = lax.broadcasted_iota(jnp.int32, (n, n), 0)
    col = lax.broadcasted_iota(jnp.int32, (n, n), 1)
    a = jnp.where(row == col, diag, 0.0)
    for blk, p in prods:
        a = a + jnp.where((row // blk) == (col // blk), p, 0.0)
    return a


def _interleave(*gens):
    results = [None] * len(gens)
    live = list(enumerate(gens))
    while live:
        still = []
        for i, gen in live:
            try:
                results[i] = next(gen)
                still.append((i, gen))
            except StopIteration:
                pass
        live = still
    return results


def _hgrn_tile(pj, lb, ong_ref, st_ref, c):
    n = pj.shape[0]
    heads = [slice(h * DK_C, (h + 1) * DK_C) for h in range(H_C)]
    chunks = [slice(j * c, (j + 1) * c) for j in range(n // c)]
    f, g, k, q, v, gc = _hgrn_gates(pj, lb)
    b = _chunk_cumsum(g, c)
    qe = q * jnp.exp(b)
    kd = k * jnp.exp(_block_row(b, c, c - 1) - b)
    levels = [_hgrn_level_products(q[:, hs], k[:, hs], f[:, hs], b[:, hs], c) for hs in heads]
    kv = [[_dot_tn(kd[rs, hs], v[rs, hs]) for rs in chunks] for hs in heads]
    yield None
    o_intra = [_dot(_hgrn_assemble(levels[h]), v[:, hs]) for h, hs in enumerate(heads)]
    yield None
    outs = []
    for h, hs in enumerate(heads):
        s = st_ref[h]
        parts = []
        for j, rs in enumerate(chunks):
            parts.append(o_intra[h][rs, :] + _dot(qe[rs, hs], s))
            bl = b[rs.stop - 1:rs.stop, hs]
            decay = jnp.broadcast_to(jnp.exp(bl), (DK_C, DK_C)).T
            s = decay * s + kv[h][j]
        st_ref[h] = s
        outs.append(_rms(jnp.concatenate(parts, axis=0), ong_ref[:, hs]) * _silu(gc[:, hs]))
    yield jnp.concatenate(outs, axis=1)


def _hgrn_body(x_ref, ng_ref, win_ref, wout_ref, lbl_ref, ong_ref, s0_ref, y_ref, sout_ref, *, c, layer):
    x = x_ref[...]
    pj = jnp.dot(_rms(x, ng_ref[...]).astype(BF16), win_ref[:, OD_Q:OD_QD], preferred_element_type=F32)
    n = pj.shape[0]
    ns = n // c
    lb = _hgrn_lower_bound(lbl_ref[...], layer)
    f, g, k, q, v, gc = _hgrn_gates(pj, lb)
    b = _chunk_cumsum(g, c)
    blast = _block_row(b, c, c - 1)
    qe = q * jnp.exp(b)
    kd = k * jnp.exp(blast - b)
    own_cols = (lax.broadcasted_iota(jnp.int32, (n, ns * DK_C), 1) // DK_C
                == lax.broadcasted_iota(jnp.int32, (n, ns * DK_C), 0) // c)
    place = lambda x: jnp.where(own_cols, jnp.concatenate([x] * ns, axis=1), 0.0)
    seq_cols = (lax.broadcasted_iota(jnp.int32, (ns, ns * DK_C), 1) // DK_C
                == lax.broadcasted_iota(jnp.int32, (ns, ns * DK_C), 0))
    ones = jnp.ones((ns, DV_C), BF16)
    last_row = (lax.broadcasted_iota(jnp.int32, (ns, n), 1)
                == lax.broadcasted_iota(jnp.int32, (ns, n), 0) * c + (c - 1)).astype(BF16)
    b_last = sum(jnp.dot(last_row, piece, preferred_element_type=F32) for piece in _split3(b))
    outs = []
    for h in range(H_C):
        hs = slice(h * DK_C, (h + 1) * DK_C)
        vh = v[:, hs]
        s = s0_ref[:, h].reshape(ns * DK_C, DV_C)
        o = _dot(_hgrn_intra(q[:, hs], k[:, hs], f[:, hs], b[:, hs], c), vh) + _dot(place(qe[:, hs]), s)
        e_last = jnp.exp(b_last[:, hs])
        e_placed = jnp.where(seq_cols, jnp.concatenate([e_last] * ns, axis=1), 0.0)
        decay = sum(lax.dot_general(piece, ones, (((0,), (0,)), ((), ())), preferred_element_type=F32)
                    for piece in _split3(e_placed))
        s_new = decay * s + _dot_tn(place(kd[:, hs]), vh)
        sout_ref[:, h] = s_new.reshape(ns, DK_C, DV_C)
        outs.append(_rms(o, ong_ref[:, hs]) * _silu(gc[:, hs]))
    o_c = jnp.concatenate(outs, axis=1).astype(BF16)
    y_ref[...] = x + jnp.dot(o_c, wout_ref[0:H_C * DV_C, :], preferred_element_type=F32)


def _hgrn_mixer(x, ng, w_in, w_out, lb_logits, on_g, s0, *, c, nseq, layer):
    t, d = x.shape
    r = nseq * c
    rowb = pl.BlockSpec((r, d), lambda i: (i, 0))
    st_spec = pl.BlockSpec((nseq, H_C, DK_C, DV_C), lambda i: (i, 0, 0, 0))
    return pl.pallas_call(
        functools.partial(_hgrn_body, c=c, layer=layer),
        grid=(t // r,),
        in_specs=[rowb, _resident((1, d)), w_in.spec(), w_out.spec(), _resident((DEPTH, 512)), _resident((1, 512)),
                  st_spec],
        out_specs=[rowb, st_spec],
        out_shape=[jax.ShapeDtypeStruct((t, d), F32), jax.ShapeDtypeStruct(s0.shape, F32)],
        compiler_params=_cparams(("arbitrary",)),
        name="hgrn_mixer",
    )(x, ng.reshape(1, d), w_in.array, w_out.array, lb_logits, on_g.reshape(1, 512), s0)


def _head_norm(x, g):
    rows, width = x.shape
    lo = lax.broadcasted_iota(jnp.int32, (rows, 128), 1) < HD_D
    outs = []
    for t in range(width // 128):
        xt = x[:, t * 128:(t + 1) * 128]
        sq = xt * xt
        ss_lo = jnp.sum(jnp.where(lo, sq, 0.0), axis=-1, keepdims=True)
        ss_hi = jnp.sum(jnp.where(lo, 0.0, sq), axis=-1, keepdims=True)
        scale = jnp.where(lo, lax.rsqrt(ss_lo * (1.0 / HD_D) + EPS), lax.rsqrt(ss_hi * (1.0 / HD_D) + EPS))
        outs.append(xt * scale)
    return jnp.concatenate(outs, axis=1) * g


def _rope_partial(x, cos, sin):
    width = x.shape[1]
    half = ROT_DIM_D // 2
    lane = lax.broadcasted_iota(jnp.int32, x.shape, 1)
    rot = jnp.where((lane % HD_D) < half, pltpu.roll(x, width - half, 1), pltpu.roll(x, half, 1))
    return x * cos + rot * sin


def _swa_prep(q, k, cos, sin, qng_ref, kng_ref):
    qn = _rope_partial(_head_norm(q, qng_ref[...]), jnp.concatenate([cos] * (HQ_D // 2), axis=1),
                       jnp.concatenate([sin] * (HQ_D // 2), axis=1)).astype(BF16)
    return qn, _rope_partial(_head_norm(k, kng_ref[...]), cos, sin)


def _swa_place(x):
    lo = lax.broadcasted_iota(jnp.int32, x.shape, 1) < HD_D
    sw = pltpu.roll(x, HD_D, 1)
    return [[jnp.where(lo, x, 0.0).astype(BF16), jnp.where(lo, 0.0, sw).astype(BF16)],
            [jnp.where(lo, sw, 0.0).astype(BF16), jnp.where(lo, 0.0, x).astype(BF16)]]


def _swa_block(qn, k_prev, k_cur, v_prev, v_cur, sink_ref, allow_prev):
    rep = HQ_D // HKV_D
    lo = lax.broadcasted_iota(jnp.int32, (WINDOW, 128), 1) < HD_D
    qi = lax.broadcasted_iota(jnp.int32, (WINDOW, 2 * WINDOW), 0)
    ci = lax.broadcasted_iota(jnp.int32, (WINDOW, 2 * WINDOW), 1)
    cur = (ci >= WINDOW) & ((ci - WINDOW) <= qi)
    prev = (ci < WINDOW) & (ci > qi)
    mask = prev | cur if allow_prev is True else (prev & allow_prev) | cur
    keys = lambda g, half: jnp.concatenate([k_prev[g][half], k_cur[g][half]], axis=0)
    ones = [jnp.where(lo, 1.0, 0.0).astype(BF16), jnp.where(lo, 0.0, 1.0).astype(BF16)]
    vals = lambda g, half: jnp.concatenate(
        [jnp.concatenate([v_prev[g][half], v_cur[g][half]], axis=0), jnp.concatenate([ones[half]] * 2, axis=0)], axis=1)
    scores = [lax.dot_general(qn[:, (h // 2) * 128:(h // 2 + 1) * 128], keys(h // rep, h % 2),
                              (((1,), (1,)), ((), ())), preferred_element_type=F32) for h in range(HQ_D)]
    yield None
    pv, sink_term = [], []
    for h in range(HQ_D):
        s = jnp.where(mask, scores[h] * (HD_D ** -0.5), -jnp.inf)
        sink = sink_ref[h:h + 1, 0:1]
        m = jnp.maximum(jnp.max(s, axis=-1, keepdims=True), sink)
        pv.append(jnp.dot(jnp.exp(s - m).astype(BF16), vals(h // rep, h % 2), preferred_element_type=F32))
        sink_term.append(jnp.exp(sink - m))
    yield None
    tiles = []
    for t in range(HQ_D // 2):
        both = pv[2 * t] + pv[2 * t + 1]
        den = both[:, 128:] + jnp.where(lo, sink_term[2 * t], sink_term[2 * t + 1])
        tiles.append(both[:, :128] / den)
    yield jnp.concatenate(tiles, axis=1)


def _swa_sample_chain(q2, kt_cache, k_new, vt_cache, v_new, sinks2, half, l):
    rows = q2.shape[0]
    ns = rows // (2 * l)
    lo = lax.broadcasted_iota(jnp.int32, (1, 128), 1) < HD_D
    ones = jnp.where(lo, 1.0, 0.0) if half == 0 else jnp.where(lo, 0.0, 1.0)
    ext = lambda v: jnp.concatenate([v, jnp.broadcast_to(ones, v.shape).astype(BF16)], axis=1)
    nt = lambda a, b: lax.dot_general(a, b, (((1,), (1,)), ((), ())), preferred_element_type=F32)
    zeros = jnp.zeros((HD_D, kt_cache.shape[1]), BF16)
    band = lambda x: jnp.concatenate([x, zeros] if half == 0 else [zeros, x], axis=0)
    k_cache = band(kt_cache.astype(BF16))
    v_cache = jnp.concatenate([band(vt_cache.astype(BF16)), band(jnp.ones_like(zeros))], axis=0)
    s1 = jnp.dot(q2, k_cache, preferred_element_type=F32)
    s2 = nt(q2, k_new)
    yield None
    r1 = lax.broadcasted_iota(jnp.int32, s1.shape, 0)
    c1 = lax.broadcasted_iota(jnp.int32, s1.shape, 1)
    ok1 = ((r1 % (ns * l)) // l == c1 // WINDOW) & (c1 % WINDOW > r1 % l)
    r2 = lax.broadcasted_iota(jnp.int32, s2.shape, 0)
    c2 = lax.broadcasted_iota(jnp.int32, s2.shape, 1)
    ok2 = ((r2 % (ns * l)) // l == c2 // l) & (c2 % l <= r2 % l)
    s1 = jnp.where(ok1, s1 * (HD_D ** -0.5), -jnp.inf)
    s2 = jnp.where(ok2, s2 * (HD_D ** -0.5), -jnp.inf)
    sink = jnp.where(lax.broadcasted_iota(jnp.int32, (rows, 1), 0) < ns * l, sinks2[0], sinks2[1])
    m = jnp.maximum(jnp.maximum(jnp.max(s1, axis=-1, keepdims=True), jnp.max(s2, axis=-1, keepdims=True)), sink)
    res = (nt(jnp.exp(s1 - m).astype(BF16), v_cache)
           + jnp.dot(jnp.exp(s2 - m).astype(BF16), ext(v_new), preferred_element_type=F32))
    yield res, jnp.exp(sink - m)


def _swa_sample_body(x_ref, res_ref, ng_ref, win_ref, wout_ref, kc_ref, vc_ref, cos_ref, sin_ref, qng_ref, kng_ref,
                     sink_ref, y_ref, kout_ref, vout_ref, o_ref, *, nseq, l, group):
    rep = HQ_D // HKV_D
    pj = jnp.dot(_rms(x_ref[...], ng_ref[...]).astype(BF16), win_ref[:, OD_QD:OD_END], preferred_element_type=F32)
    qn, kn = _swa_prep(pj[:, 0:OD_KD - OD_QD], pj[:, OD_KD - OD_QD:OD_VD - OD_QD], cos_ref[...], sin_ref[...],
                       qng_ref, kng_ref)
    v = pj[:, OD_VD - OD_QD:]
    lo = lax.broadcasted_iota(jnp.int32, (group * l, 128), 1) < HD_D
    r_sel = lax.broadcasted_iota(jnp.int32, (group * l, group * WINDOW), 0)
    c_sel = lax.broadcasted_iota(jnp.int32, (group * l, group * WINDOW), 1)
    append = ((c_sel // WINDOW == r_sel // l) & (c_sel % WINDOW == WINDOW - l + r_sel % l)).astype(BF16)
    is_new = lax.broadcasted_iota(jnp.int32, (HKV_D * HD_D, group * WINDOW), 1) % WINDOW >= WINDOW - l

    def shifted(old_t, new_rows):
        moved = sum(lax.dot_general(piece, append, (((0,), (0,)), ((), ())), preferred_element_type=F32)
                    for piece in _split3(new_rows))
        return jnp.where(is_new, moved, pltpu.roll(old_t, group * WINDOW - l, 1))

    chains, where = [], []
    for gi in range(nseq // group):
        rs = slice(gi * group * l, (gi + 1) * group * l)
        seqs = range(gi * group, (gi + 1) * group)
        kt = [jnp.concatenate([kc_ref[s, g] for s in seqs], axis=1) for g in range(HKV_D)]
        vt = [jnp.concatenate([vc_ref[s, g] for s in seqs], axis=1) for g in range(HKV_D)]
        k_new, v_new = _swa_place(kn[rs, :]), _swa_place(v[rs, :])
        for g in range(HKV_D):
            tiles = [(g * rep) // 2, (g * rep) // 2 + 1]
            q2 = jnp.concatenate([qn[rs, t * 128:(t + 1) * 128] for t in tiles], axis=0)
            for half in range(2):
                sinks2 = [sink_ref[2 * t + half:2 * t + half + 1, 0:1] for t in tiles]
                chains.append(_swa_sample_chain(q2, kt[g], k_new[g][half], vt[g], v_new[g][half], sinks2, half, l))
                where.append((rs, tiles, half))
        k_all = shifted(jnp.concatenate(kt, axis=0), kn[rs, :])
        v_all = shifted(jnp.concatenate(vt, axis=0), v[rs, :])
        for i, s in enumerate(seqs):
            for g in range(HKV_D):
                kout_ref[s, g] = k_all[g * HD_D:(g + 1) * HD_D, i * WINDOW:(i + 1) * WINDOW]
                vout_ref[s, g] = v_all[g * HD_D:(g + 1) * HD_D, i * WINDOW:(i + 1) * WINDOW]
    results = _interleave(*chains)
    for i in range(0, len(chains), 2):
        (rs, tiles, _), (res0, st0), (res1, st1) = where[i], results[i], results[i + 1]
        both = res0 + res1
        for j, t in enumerate(tiles):
            js = slice(j * group * l, (j + 1) * group * l)
            den = both[js, 128:] + jnp.where(lo, st0[js, :], st1[js, :])
            o_ref[rs, t * 128:(t + 1) * 128] = both[js, :128] / den
    y_ref[...] = res_ref[...] + jnp.dot(o_ref[...].astype(BF16), wout_ref[H_C * DV_C:, :], preferred_element_type=F32)


def _swa_gains(qn_g, kn_g):
    return (jnp.tile(qn_g, HQ_D).reshape(1, HQ_D * HD_D), jnp.tile(kn_g, HKV_D).reshape(1, HKV_D * HD_D))


def _swa_sample(x, res, pos, ng, w_in, w_out, kcache, vcache, qn_g, kn_g, sinks, *, nseq, group):
    t, d = x.shape
    l = pos.shape[0]
    r = nseq * l
    cos, sin = _rope_tables(pos, ROT_DIM_D, ROPE_THETA, HD_D, 2, row_reps=nseq)
    qng, kng = _swa_gains(qn_g, kn_g)
    rowb = pl.BlockSpec((r, d), lambda i: (i, 0))
    cache = pl.BlockSpec((nseq, HKV_D, HD_D, WINDOW), lambda i: (i, 0, 0, 0))
    return pl.pallas_call(
        functools.partial(_swa_sample_body, nseq=nseq, l=l, group=group),
        grid=(t // r,),
        in_specs=[rowb, rowb, _resident((1, d)), w_in.spec(), w_out.spec(), cache, cache, _resident((r, 128)),
                  _resident((r, 128)), _resident((1, 512)), _resident((1, 128)), _resident((HQ_D, 128))],
        out_specs=[rowb, cache, cache],
        out_shape=[jax.ShapeDtypeStruct((t, d), F32), jax.ShapeDtypeStruct(kcache.shape, F32),
                   jax.ShapeDtypeStruct(vcache.shape, F32)],
        scratch_shapes=[pltpu.VMEM((r, HQ_D * HD_D), F32)],
        compiler_params=_cparams(("arbitrary",)),
        name="swa_sample",
    )(x, res, ng.reshape(1, d), w_in.array, w_out.array, kcache, vcache, cos, sin, qng, kng,
      jnp.broadcast_to(sinks[:, None], (HQ_D, 128)))


def _odd_layer_body(x_ref, ng_ref, win_ref, wout_ref, lbl_ref, ong_ref, cos_ref, sin_ref, qng_ref, kng_ref, sink_ref,
                    y_ref, sout_ref, kout_ref, vout_ref, st_ref, kprev_ref, vprev_ref, *, c, pb, layer):
    step = pl.program_id(1)

    @pl.when(step == 0)
    def _():
        st_ref[...] = jnp.zeros_like(st_ref)
        kprev_ref[...] = jnp.zeros_like(kprev_ref)
        vprev_ref[...] = jnp.zeros_like(vprev_ref)

    x = x_ref[...]
    h = _rms(x, ng_ref[...]).astype(BF16)
    lb = _hgrn_lower_bound(lbl_ref[...], layer)
    ngrp = x.shape[0] // pb
    project = lambda g: jnp.dot(h[g * pb:(g + 1) * pb, :], win_ref[...], preferred_element_type=F32)
    proj = project(0)
    k_last, v_last = kprev_ref[...], vprev_ref[...]
    k_prev, v_prev = _swa_place(k_last), _swa_place(v_last)
    for g in range(ngrp):
        nxt = project(g + 1) if g + 1 < ngrp else None
        grows = slice(g * pb, (g + 1) * pb)
        qn, kn = _swa_prep(proj[:, OD_QD:OD_KD], proj[:, OD_KD:OD_VD], cos_ref[grows, :], sin_ref[grows, :],
                           qng_ref, kng_ref)
        staged = []
        for u in range(pb // WINDOW):
            us = slice(u * WINDOW, (u + 1) * WINDOW)
            k_last, v_last = kn[us, :], proj[us, OD_VD:OD_END]
            k_cur, v_cur = _swa_place(k_last), _swa_place(v_last)
            staged.append(_hgrn_tile(proj[us, :], lb, ong_ref, st_ref, c))
            staged.append(_swa_block(qn[us, :], k_prev, k_cur, v_prev, v_cur, sink_ref,
                                     True if g + u > 0 else step > 0))
            k_prev, v_prev = k_cur, v_cur
        outs = _interleave(*staged)
        o = jnp.concatenate([jnp.concatenate(outs[0::2], axis=0), jnp.concatenate(outs[1::2], axis=0)], axis=1)
        y_ref[grows, :] = x[grows, :] + jnp.dot(o.astype(BF16), wout_ref[...], preferred_element_type=F32)
        proj = nxt
    kprev_ref[...] = k_last
    vprev_ref[...] = v_last
    kout_ref[0] = k_last
    vout_ref[0] = v_last

    @pl.when(step == pl.num_programs(1) - 1)
    def _():
        sout_ref[0] = st_ref[...]


def _odd_layer(x, pos, ng, w_in, w_out, lb_logits, on_g, qn_g, kn_g, sinks, *, n_seq, r, pb, layer):
    t, d = x.shape
    seq_len = pos.shape[0]
    c = math.gcd(seq_len, HGRN_CHUNK)
    assert WINDOW % c == 0 and pb % WINDOW == 0 and r % pb == 0 and seq_len % r == 0
    steps = seq_len // r
    cos, sin = _rope_tables(pos, ROT_DIM_D, ROPE_THETA, HD_D, 2)
    qng, kng = _swa_gains(qn_g, kn_g)
    rowb = pl.BlockSpec((r, d), lambda b, i: (b * steps + i, 0))
    tab = pl.BlockSpec((r, 128), lambda b, i: (i, 0))
    st_spec = pl.BlockSpec((1, H_C, DK_C, DV_C), lambda b, i: (b, 0, 0, 0))
    kv_out = pl.BlockSpec((1, WINDOW, 128), lambda b, i: (b, 0, 0))
    return pl.pallas_call(
        functools.partial(_odd_layer_body, c=c, pb=pb, layer=layer),
        grid=(n_seq, steps),
        in_specs=[rowb, _resident((1, d)), w_in.spec(), w_out.spec(), _resident((DEPTH, 512)),
                  _resident((1, 512)), tab, tab, _resident((1, 512)), _resident((1, 128)), _resident((HQ_D, 128))],
        out_specs=[rowb, st_spec, kv_out, kv_out],
        out_shape=[jax.ShapeDtypeStruct((t, d), F32), jax.ShapeDtypeStruct((n_seq, H_C, DK_C, DV_C), F32),
                   jax.ShapeDtypeStruct((n_seq, WINDOW, 128), F32), jax.ShapeDtypeStruct((n_seq, WINDOW, 128), F32)],
        scratch_shapes=[pltpu.VMEM((H_C, DK_C, DV_C), F32), pltpu.VMEM((WINDOW, 128), F32),
                        pltpu.VMEM((WINDOW, 128), F32)],
        compiler_params=_cparams(("arbitrary", "arbitrary")),
        name="odd_layer",
    )(x, ng.reshape(1, d), w_in.array, w_out.array, lb_logits, on_g.reshape(1, 512), cos, sin, qng, kng,
      jnp.broadcast_to(sinks[:, None], (HQ_D, 128)))


def _mem_kv_body(m_ref, w_ref, g_ref, k_ref, v_ref):
    kv = jnp.dot(m_ref[...].astype(BF16), w_ref[...], preferred_element_type=F32)
    hw = H_X * HD_X
    for h in range(H_X):
        hs = slice(h * HD_X, (h + 1) * HD_X)
        k_ref[:, hs] = _rms(kv[:, hs], g_ref[...])
    v_ref[...] = kv[:, hw:]


def _mem_kv(mem, w, g):
    t, d = mem.shape
    hw = H_X * HD_X
    whole = lambda shape: pl.BlockSpec(shape, lambda i: (0,) * len(shape))
    return pl.pallas_call(
        _mem_kv_body,
        grid=(1,),
        in_specs=[whole((t, d)), w.spec(), whole((1, HD_X))],
        out_specs=[whole((t, hw)), whole((t, hw))],
        out_shape=[jax.ShapeDtypeStruct((t, hw), F32), jax.ShapeDtypeStruct((t, hw), F32)],
        compiler_params=_cparams(("arbitrary",)),
        name="mem_kv",
    )(mem, w.array, g.reshape(1, HD_X))


def _mem_attend_rows(x, g_ref, wq_ref, qng_ref, mk_ref, mv_ref, wo_ref):
    heads = [slice(h * HD_X, (h + 1) * HD_X) for h in range(H_X)]
    q = jnp.dot(_rms(x, g_ref[...]).astype(BF16), wq_ref[...], preferred_element_type=F32)
    yield None
    scores = [_dot_nt(_rms(q[:, hs], qng_ref[...]), mk_ref[0, :, hs]) for hs in heads]
    yield None
    outs = []
    for hs, sc in zip(heads, scores):
        sc = sc * (HD_X ** -0.5)
        p = jnp.exp(sc - jnp.max(sc, axis=-1, keepdims=True))
        outs.append(_dot(p, mv_ref[0, :, hs]) / jnp.sum(p, axis=-1, keepdims=True))
    yield None
    yield x + jnp.dot(jnp.concatenate(outs, axis=1).astype(BF16), wo_ref[...], preferred_element_type=F32)


def _mem_attend_body(x_ref, g_ref, wq_ref, qng_ref, mk_ref, mv_ref, wo_ref, y_ref, *, nsplit):
    rows = x_ref.shape[0] // nsplit
    parts = _interleave(*[_mem_attend_rows(x_ref[i * rows:(i + 1) * rows, :], g_ref, wq_ref, qng_ref, mk_ref, mv_ref,
                                           wo_ref) for i in range(nsplit)])
    for i, y in enumerate(parts):
        y_ref[i * rows:(i + 1) * rows, :] = y


def _mem_attend(x, g, wq, qn_g, mk, mv, wo, *, tm, nsplit, tiles_per_mem):
    t, d = x.shape
    hw = H_X * HD_X
    mem_spec = pl.BlockSpec((1, N_MEM, hw), lambda i: (i // tiles_per_mem, 0, 0))
    return pl.pallas_call(
        functools.partial(_mem_attend_body, nsplit=nsplit),
        grid=(t // tm,),
        in_specs=[pl.BlockSpec((tm, d), lambda i: (i, 0)), _resident((1, d)), wq.spec(),
                  _resident((1, HD_X)), mem_spec, mem_spec, wo.spec()],
        out_specs=pl.BlockSpec((tm, d), lambda i: (i, 0)),
        out_shape=jax.ShapeDtypeStruct((t, d), F32),
        compiler_params=_cparams(("arbitrary",)),
        name="mem_attend",
    )(x, g.reshape(1, d), wq.array, qn_g.reshape(1, HD_X), mk, mv, wo.array)


def _mem_attend_cached_body(x_ref, g_ref, wq_ref, qng_ref, mk_ref, mv_ref, wo_ref, y_ref, att_ref, *, nseq):
    x = x_ref[...]
    q = jnp.dot(_rms(x, g_ref[...]).astype(BF16), wq_ref[...], preferred_element_type=F32)
    l = x.shape[0] // nseq
    rows_all = H_X * l
    cols = N_MEM * H_X
    own = (lax.broadcasted_iota(jnp.int32, (rows_all, cols), 1) % H_X
           == lax.broadcasted_iota(jnp.int32, (rows_all, cols), 0) // l)
    scores = []
    for s in range(nseq):
        rs = slice(s * l, (s + 1) * l)
        qs = jnp.concatenate([_rms(q[rs, h * HD_X:(h + 1) * HD_X], qng_ref[...]) for h in range(H_X)], axis=0)
        scores.append(_dot_nt(qs, mk_ref[0, s]))
    for s in range(nseq):
        rs = slice(s * l, (s + 1) * l)
        sc = jnp.where(own, scores[s] * (HD_X ** -0.5), -jnp.inf)
        m = jnp.max(sc, axis=-1, keepdims=True)
        p = jnp.exp(sc - m)
        o = _dot(p, mv_ref[0, s]) / jnp.sum(p, axis=-1, keepdims=True)
        for h in range(H_X):
            att_ref[rs, h * HD_X:(h + 1) * HD_X] = o[h * l:(h + 1) * l, :]
    y_ref[...] = x + jnp.dot(att_ref[...].astype(BF16), wo_ref[...], preferred_element_type=F32)


def _mem_attend_cached(x, g, wq, qn_g, mk, mv, wo, *, layer, nseq):
    t, d = x.shape
    hw = H_X * HD_X
    tm = nseq * (t // mk.shape[1])
    full = lambda shape: pl.BlockSpec(shape, lambda i: (0,) * len(shape))
    mem_spec = pl.BlockSpec((1, nseq, N_MEM * H_X, HD_X), lambda i: (layer, i, 0, 0))
    return pl.pallas_call(
        functools.partial(_mem_attend_cached_body, nseq=nseq),
        grid=(t // tm,),
        in_specs=[pl.BlockSpec((tm, d), lambda i: (i, 0)), full((1, d)), wq.spec(), full((1, HD_X)),
                  mem_spec, mem_spec, wo.spec()],
        out_specs=pl.BlockSpec((tm, d), lambda i: (i, 0)),
        out_shape=jax.ShapeDtypeStruct((t, d), F32),
        scratch_shapes=[pltpu.VMEM((tm, hw), F32)],
        compiler_params=_cparams(("arbitrary",)),
        name="mem_attend_cached",
    )(x, g.reshape(1, d), wq.array, qn_g.reshape(1, HD_X), mk, mv, wo.array)


def _ffn_body(*refs, tm, ffc, tiles_per_seq, per_row_state):
    if per_row_state:
        x_ref, g_ref, wg_ref, wu_ref, cw_ref, cb_ref, wd_ref, c0_ref, y_ref, gt_ref, tail_ref = refs
    else:
        x_ref, g_ref, wg_ref, wu_ref, cw_ref, cb_ref, wd_ref, y_ref, gt_ref, tail_ref = refs
    ff = wg_ref.shape[1]
    x = x_ref[...]
    h = _rms(x, g_ref[...]).astype(BF16)
    hist = CONV_W - 1
    if per_row_state:
        ns = tm // 8
        r_in = lax.broadcasted_iota(jnp.int32, (tm, hist * ns), 0)
        c_in = lax.broadcasted_iota(jnp.int32, (tm, hist * ns), 1)
        same = (r_in // 8) == (c_in // hist)
        sel1 = (same & (r_in % 8 == 0) & (c_in % hist == hist - 1)).astype(BF16)
        sel2 = (same & (r_in % 8 < hist) & (c_in % hist == r_in % 8)).astype(BF16)
        r_out = lax.broadcasted_iota(jnp.int32, (hist * ns, tm), 0)
        c_out = lax.broadcasted_iota(jnp.int32, (hist * ns, tm), 1)
        sel_out = (((c_out // 8) == (r_out // hist)) & (c_out % 8 == 8 - hist + r_out % hist)).astype(BF16)

        def select(sel, val):
            return sum(jnp.dot(sel, piece, preferred_element_type=F32) for piece in _split3(val))
    else:
        @pl.when(pl.program_id(0) % tiles_per_seq == 0)
        def _():
            tail_ref[...] = jnp.zeros_like(tail_ref)

    row8 = lax.broadcasted_iota(jnp.int32, (8, ffc), 0)
    acc = None

    def down(acts):
        a = acts[0][0] if len(acts) == 1 else jnp.concatenate([p[0] for p in acts], axis=1)
        part = jnp.dot(a, wd_ref[acts[0][1].start:acts[-1][1].stop, :], preferred_element_type=F32)
        return part if acc is None else acc + part

    filling, ready = [], None
    for c0 in range(0, ff, ffc):
        cs = slice(c0, c0 + ffc)
        gate = jnp.dot(h, wg_ref[:, cs], preferred_element_type=F32)
        up = jnp.dot(h, wu_ref[:, cs], preferred_element_type=F32)
        if ready is not None:
            acc = down(ready)
            ready = None
        r1 = pltpu.roll(gate, 1, 0)
        r2 = pltpu.roll(gate, 2, 0)
        if per_row_state:
            pos = lax.broadcasted_iota(jnp.int32, gate.shape, 0) % 8
            c0c = c0_ref[:, cs]
            g1 = jnp.where(pos == 0, select(sel1, c0c), r1)
            g2 = jnp.where(pos < hist, select(sel2, c0c), r2)
            gt_ref[:, cs] = select(sel_out, gate)
        else:
            prev = tail_ref[:, cs]
            top1 = jnp.where(row8 == 0, prev[7:8, :], r1[0:8, :])
            top2 = jnp.where(row8 == 0, prev[6:7, :], jnp.where(row8 == 1, prev[7:8, :], r2[0:8, :]))
            g1 = jnp.concatenate([top1, r1[8:, :]], axis=0)
            g2 = jnp.concatenate([top2, r2[8:, :]], axis=0)
            tail_ref[:, cs] = gate[tm - 8:, :]
            gt_ref[:, cs] = gate[tm - 8:, :]
        conv = cb_ref[:, cs] + cw_ref[0:1, cs] * g2 + cw_ref[1:2, cs] * g1 + cw_ref[2:3, cs] * gate
        filling.append(((_gelu(conv) * up).astype(BF16), cs))
        if len(filling) == FFN_DOWN_GROUP:
            filling, ready = [], filling
    for acts in (ready, filling):
        if acts:
            acc = down(acts)
    y_ref[...] = x + acc


def _ffn(x, g, wg, wu, cw, cb, wd, *, tm, ffc, tiles_per_seq, c0=None):
    t, d = x.shape
    ff = wg.shape[1]
    per_row_state = c0 is not None
    tail_n = tm // 8 * (CONV_W - 1) if per_row_state else 8
    in_specs = [pl.BlockSpec((tm, d), lambda i: (i, 0)), _resident((1, d)), wg.spec(), wu.spec(),
                _resident((8, ff)), _resident((1, ff)), wd.spec()]
    args = [x, g.reshape(1, d), wg.array, wu.array, jnp.pad(cw, ((0, 8 - CONV_W), (0, 0))), cb.reshape(1, ff),
            wd.array]
    if per_row_state:
        in_specs += [pl.BlockSpec((tail_n, ff), lambda i: (i, 0))]
        args += [c0]
    return pl.pallas_call(
        functools.partial(_ffn_body, tm=tm, ffc=ffc, tiles_per_seq=tiles_per_seq, per_row_state=per_row_state),
        grid=(t // tm,),
        in_specs=in_specs,
        out_specs=[pl.BlockSpec((tm, d), lambda i: (i, 0)), pl.BlockSpec((tail_n, ff), lambda i: (i, 0))],
        out_shape=[jax.ShapeDtypeStruct((t, d), F32), jax.ShapeDtypeStruct((t // tm * tail_n, ff), F32)],
        scratch_shapes=[pltpu.VMEM((8, ff), F32)],
        compiler_params=_cparams(("arbitrary",)),
        name="ffn",
    )(*args)


def _run_prompt(x, pos, n_seq, mem, w):
    seq_len = pos.shape[0]
    tm = 1024
    hw = H_X * HD_X
    out = {k: [] for k in ("ret", "hgrn", "swa_k", "swa_v", "mem_k", "mem_v", "conv")}
    for l in range(DEPTH):
        j = l // 2
        if l % 2 == 0:
            x, s_new = _even_layer(x, pos, w["norm_mix_g"][l], w["ev_w_in"][j], w["ev_w_out"][j], w["ret_gn_g"][j],
                                   w["mlp_norm_g"][j], w["mlp_w_s"][j], w["mlp_b_s"][j], n_seq=n_seq, r=1024, pb=256)
            out["ret"].append(s_new)
        else:
            x, s_new, k_new, v_new = _odd_layer(x, pos, w["norm_mix_g"][l], w["od_w_in"][j], w["od_w_out"][j],
                                                w["hgrn_lb_logits"], w["hgrn_onorm_g"][j], w["swa_qnorm_g"][j],
                                                w["swa_knorm_g"][j], w["swa_sinks"][j], n_seq=n_seq, r=1024, pb=256,
                                                layer=l)
            out["hgrn"].append(s_new)
            out["swa_k"].append(k_new.reshape(n_seq, WINDOW, HKV_D, HD_D))
            out["swa_v"].append(v_new.reshape(n_seq, WINDOW, HKV_D, HD_D))
        mk, mv = _mem_kv(mem, w["mem_w_kv"][l], w["mem_knorm_g"][l])
        out["mem_k"].append(mk.reshape(n_seq, N_MEM, H_X, HD_X))
        out["mem_v"].append(mv.reshape(n_seq, N_MEM, H_X, HD_X))
        x = _mem_attend(x, w["norm_mem_g"][l], w["mem_w_q"][l], w["mem_qnorm_g"][l], mk.reshape(n_seq, N_MEM, hw),
                        mv.reshape(n_seq, N_MEM, hw), w["mem_w_o"][l], tm=2 * tm, nsplit=4,
                        tiles_per_mem=seq_len // (2 * tm))
        x, gt = _ffn(x, w["norm_ffn_g"][l], w["ffn_w_gate"][l], w["ffn_w_up"][l], w["ffn_conv_w"][l],
                     w["ffn_conv_b"][l], w["ffn_w_down"][l], tm=tm, ffc=256, tiles_per_seq=seq_len // tm)
        out["conv"].append(gt.reshape(n_seq, seq_len // tm, 8, D_FF)[:, -1, 8 - (CONV_W - 1):, :])
    return x, {name: jnp.stack(rows) for name, rows in out.items()}


def _run_sample(x, pos, n_seq, st, w):
    seq_len = pos.shape[0]
    assert seq_len == 8
    t = x.shape[0]
    out = {k: [] for k in ("ret", "chunk_v", "hgrn", "swa_k", "swa_v", "conv")}
    for l in range(DEPTH):
        j = l // 2
        if l % 2 == 0:
            x, s_new, v_rows = _even_mixer(x, pos, w["norm_mix_g"][l], w["ev_w_in"][j], w["ev_w_out"][j],
                                           st["ret"][j], w["ret_gn_g"][j], w["mlp_norm_g"][j], w["mlp_w_s"][j],
                                           w["mlp_b_s"][j], nseg=16)
            out["chunk_v"].append(v_rows.reshape(n_seq, seq_len, G_B, DG_B))
            out["ret"].append(s_new)
        else:
            x_mid, s_new = _hgrn_mixer(x, w["norm_mix_g"][l], w["od_w_in"][j], w["od_w_out"][j], w["hgrn_lb_logits"],
                                       w["hgrn_onorm_g"][j], st["hgrn"][j], c=seq_len, nseq=16, layer=l)
            x, k_new, v_new = _swa_sample(x, x_mid, pos, w["norm_mix_g"][l], w["od_w_in"][j], w["od_w_out"][j],
                                          jnp.transpose(st["swa_k"][j], (0, 2, 3, 1)),
                                          jnp.transpose(st["swa_v"][j], (0, 2, 3, 1)), w["swa_qnorm_g"][j],
                                          w["swa_knorm_g"][j], w["swa_sinks"][j], nseq=16, group=8)
            out["hgrn"].append(s_new)
            out["swa_k"].append(jnp.transpose(k_new, (0, 3, 1, 2)))
            out["swa_v"].append(jnp.transpose(v_new, (0, 3, 1, 2)))
        x = _mem_attend_cached(x, w["norm_mem_g"][l], w["mem_w_q"][l], w["mem_qnorm_g"][l],
                               st["mem_k"].reshape(DEPTH, n_seq, N_MEM * H_X, HD_X),
                               st["mem_v"].reshape(DEPTH, n_seq, N_MEM * H_X, HD_X), w["mem_w_o"][l], layer=l, nseq=8)
        c0 = st["conv"][l].reshape(n_seq * (CONV_W - 1), D_FF)
        x, gt = _ffn(x, w["norm_ffn_g"][l], w["ffn_w_gate"][l], w["ffn_w_up"][l], w["ffn_conv_w"][l],
                     w["ffn_conv_b"][l], w["ffn_w_down"][l], tm=512, ffc=256, tiles_per_seq=1, c0=c0)
        out["conv"].append(gt.reshape(n_seq, CONV_W - 1, D_FF))
    return x, {name: jnp.stack(rows) for name, rows in out.items()}


def kernel(x_prompt, x_sample, state_ret, state_hgrn, cache_swa_k, cache_swa_v, cache_mem_k, cache_mem_v,
           state_ffn_conv, mem_prompt, norm_mix_g, norm_mem_g, norm_ffn_g, ev_w_in, ev_w_out, ret_gn_g,
           mlp_norm_g, mlp_w_s, mlp_b_s, od_w_in, od_w_out, hgrn_lb_logits, hgrn_onorm_g, swa_qnorm_g,
           swa_knorm_g, swa_sinks, mem_w_q, mem_w_kv, mem_qnorm_g, mem_knorm_g, mem_w_o, ffn_w_gate,
           ffn_w_up, ffn_conv_w, ffn_conv_b, ffn_w_down):
    def bf(a):
        stacked = a.astype(BF16)
        return [_Layer(stacked, i) for i in range(a.shape[0])]

    w = dict(norm_mix_g=norm_mix_g, norm_mem_g=norm_mem_g, norm_ffn_g=norm_ffn_g, ev_w_in=bf(ev_w_in),
             ev_w_out=bf(ev_w_out), ret_gn_g=ret_gn_g, mlp_norm_g=mlp_norm_g, mlp_w_s=mlp_w_s, mlp_b_s=mlp_b_s,
             od_w_in=bf(od_w_in), od_w_out=bf(od_w_out), hgrn_lb_logits=hgrn_lb_logits, hgrn_onorm_g=hgrn_onorm_g,
             swa_qnorm_g=swa_qnorm_g, swa_knorm_g=swa_knorm_g, swa_sinks=swa_sinks, mem_w_q=bf(mem_w_q),
             mem_w_kv=bf(mem_w_kv), mem_qnorm_g=mem_qnorm_g, mem_knorm_g=mem_knorm_g, mem_w_o=bf(mem_w_o),
             ffn_w_gate=bf(ffn_w_gate), ffn_w_up=bf(ffn_w_up), ffn_conv_w=ffn_conv_w, ffn_conv_b=ffn_conv_b,
             ffn_w_down=bf(ffn_w_down))
    b, seq, d = x_prompt.shape
    db, dseq, _ = x_sample.shape
    pos_prompt = np.arange(seq)
    pos_sample = PAST_LEN + np.arange(dseq)
    y_p, ns_p = _run_prompt(x_prompt.reshape(b * seq, d), pos_prompt, b, mem_prompt.reshape(b * N_MEM, d), w)
    st = dict(ret=state_ret, hgrn=state_hgrn, swa_k=cache_swa_k, swa_v=cache_swa_v, mem_k=cache_mem_k,
              mem_v=cache_mem_v, conv=state_ffn_conv)
    y_s, ns_s = _run_sample(x_sample.reshape(db * dseq, d), pos_sample, db, st, w)
    return (y_p.reshape(b, seq, d), y_s.reshape(db, dseq, d), ns_p["ret"], ns_p["hgrn"], ns_p["swa_k"], ns_p["swa_v"],
            ns_p["mem_k"], ns_p["mem_v"], ns_p["conv"], ns_s["ret"], ns_s["chunk_v"], ns_s["hgrn"], ns_s["swa_k"],
            ns_s["swa_v"], ns_s["conv"])
```

```python
import functools
import math

import jax
import jax.numpy as jnp
import numpy as np
from jax import lax
from jax.experimental import pallas as pl
from jax.experimental.pallas import tpu as pltpu

F32 = jnp.float32
BF16 = jnp.bfloat16

D_MODEL = 1024
DEPTH = 2
PAST_LEN = 16384
H_A, DV_A, DK_A = 4, 128, 64
RET_CHUNK = 128
RET_THETA = 10000.0
G_B, DG_B = 4, 128
MLP_CHUNK = 128
H_C, DK_C, DV_C = 4, 128, 128
HGRN_CHUNK = 64
HD_D, HQ_D, HKV_D = 64, 8, 2
WINDOW = 128
ROPE_THETA = 500000.0
ROT_DIM_D = HD_D // 4
N_MEM, H_X, HD_X = 256, 4, 128
D_FF = 2816
CONV_W = 3
FFN_DOWN_GROUP = 6
EPS = 1e-6
SQRT_HALF = float(np.sqrt(0.5))
EV_Q, EV_K, EV_V, EV_G, EV_U, EV_VB, EV_END = 0, 256, 512, 1024, 1536, 2048, 2560
OD_Q, OD_F, OD_I, OD_G, OD_QD, OD_KD, OD_VD, OD_END = 0, 512, 1024, 1536, 2048, 2560, 2688, 2816

VMEM_LIMIT_BYTES = 56 * 1024 * 1024


def _cparams(sem):
    return pltpu.CompilerParams(dimension_semantics=sem, vmem_limit_bytes=VMEM_LIMIT_BYTES)


def _resident(shape):
    return pl.BlockSpec(shape, lambda *_: (0,) * len(shape), pipeline_mode=pl.Buffered(1))


class _Layer:
    def __init__(self, array, index):
        self.array, self.index, self.shape = array, index, array.shape[1:]

    def spec(self):
        index, zeros = self.index, (0,) * len(self.shape)
        return pl.BlockSpec((None,) + self.shape, lambda *_: (index,) + zeros, pipeline_mode=pl.Buffered(1))


def _rms(x, g):
    return x * lax.rsqrt(jnp.mean(x * x, axis=-1, keepdims=True) + EPS) * g


def _ln(x, g):
    mu = jnp.mean(x, axis=-1, keepdims=True)
    xc = x - mu
    return xc * lax.rsqrt(jnp.mean(xc * xc, axis=-1, keepdims=True) + EPS) * g


def _sigmoid(x):
    return 1.0 / (1.0 + jnp.exp(-x))


def _silu(x):
    return x * _sigmoid(x)


def _gelu(x):
    return 0.5 * x * (1.0 + lax.erf(x * SQRT_HALF))


def _dot(a, b):
    return jnp.dot(a.astype(BF16), b.astype(BF16), preferred_element_type=F32)


def _dot_nt(a, b):
    return lax.dot_general(a.astype(BF16), b.astype(BF16), (((1,), (1,)), ((), ())), preferred_element_type=F32)


def _dot_tn(a, b):
    return lax.dot_general(a.astype(BF16), b.astype(BF16), (((0,), (0,)), ((), ())), preferred_element_type=F32)


def _split3(x):
    hi = x.astype(BF16)
    r1 = x - hi.astype(F32)
    mid = r1.astype(BF16)
    lo = (r1 - mid.astype(F32)).astype(BF16)
    return hi, mid, lo


def _retention_consts(c, nseq=1):
    lg = np.log1p(-np.exp2(-5.0 - np.arange(H_A, dtype=np.float64)))
    idx = np.arange(c, dtype=np.float64)
    rel = idx[:, None] - idx[None, :]
    dmask = np.where(rel >= 0, np.exp(rel[None] * lg[:, None, None]), 0.0)
    q_dec = np.exp((idx + 1.0)[None, :] * lg[:, None])
    k_dec = np.exp((c - 1.0 - idx)[None, :] * lg[:, None])
    c_dec = np.exp(c * lg)
    qdec = np.tile(np.repeat(q_dec.T, DK_A, axis=1), (nseq, 1))
    kdec = np.tile(np.repeat(k_dec.T, DK_A, axis=1), (nseq, 1))
    dmask = np.stack([np.kron(np.eye(nseq), dmask[h]) for h in range(H_A)])
    cdec = np.broadcast_to(c_dec[:, None, None], (H_A, 8, DV_A))
    return tuple(jnp.asarray(a, F32) for a in (dmask, qdec, kdec, cdec))


def _rope_tables(pos, rot_dim, theta, head_dim, reps, row_reps=1):
    half = rot_dim // 2
    inv = theta ** (-np.arange(half, dtype=np.float64) / half)
    ang = pos.astype(np.float64)[:, None] * inv[None, :]
    cos = np.cos(ang)
    sin = np.sin(ang)
    l = pos.shape[0]
    pad = head_dim - rot_dim
    cos_h = np.concatenate([cos, cos, np.ones((l, pad))], axis=1)
    sin_h = np.concatenate([-sin, sin, np.zeros((l, pad))], axis=1)
    return (jnp.asarray(np.tile(cos_h, (row_reps, reps)), F32), jnp.asarray(np.tile(sin_h, (row_reps, reps)), F32))


def _even_consts(c, gn_g, mn_g, w_s, b_s, nseq=1):
    dmask, qdec, kdec, cdec = _retention_consts(c, nseq)
    bs_full = jnp.tile(jnp.repeat(b_s[:, :c].T, DG_B, axis=1), (nseq, 1))
    arrays = [dmask, qdec, kdec, cdec, gn_g.reshape(1, H_A * DV_A), mn_g.reshape(1, G_B * DG_B),
              jnp.tile(w_s[:, :c, :c], (1, nseq, nseq)), bs_full]
    return arrays, [a.shape for a in arrays]


def _retention_state_batched(qd, kd, v, s0_ref, cdec_ref, nseq):
    n = qd.shape[0]
    own = (lax.broadcasted_iota(jnp.int32, (n, nseq * 128), 1) // 128
           == lax.broadcasted_iota(jnp.int32, (n, nseq * 128), 0) // (n // nseq))
    place = lambda x: jnp.where(own, jnp.concatenate([x] * nseq, axis=1), 0.0)
    inter, s_new = [None] * H_A, [None] * H_A
    for p in range(H_A // 2):
        h0, h1 = 2 * p, 2 * p + 1
        lanes = slice(p * 128, (p + 1) * 128)
        s_a, s_b = s0_ref[:, h0], s0_ref[:, h1]
        z = jnp.zeros_like(s_a)
        slab = jnp.concatenate([jnp.concatenate([s_a, z], axis=2), jnp.concatenate([z, s_b], axis=2)], axis=1)
        res = _dot(place(qd[:, lanes]), slab.reshape(nseq * 2 * DK_A, 2 * DV_A))
        inter[h0], inter[h1] = res[:, :DV_A], res[:, DV_A:]
        upd = _dot_tn(place(kd[:, lanes]), v[:, 2 * p * DV_A:2 * (p + 1) * DV_A]).reshape(nseq, 2 * DK_A, 2 * DV_A)
        s_new[h0] = cdec_ref[h0, 0:1, :] * s_a + upd[:, :DK_A, :DV_A]
        s_new[h1] = cdec_ref[h1, 0:1, :] * s_b + upd[:, DK_A:, DV_A:]
    return inter, s_new


def _even_segment(pj, cos, sin, consts, s_old, nseq=1):
    dmask_ref, qdec_ref, kdec_ref, cdec_ref, gng_ref, mng_ref, ws_ref, bs_ref = consts
    c = pj.shape[0]
    width = H_A * DK_A
    lane = lax.broadcasted_iota(jnp.int32, (c, width), 1)
    first_half = (lane % DK_A) < (DK_A // 2)
    row = lax.broadcasted_iota(jnp.int32, (c, c), 0)
    col = lax.broadcasted_iota(jnp.int32, (c, c), 1)
    tril = (col <= row) & (row // (c // nseq) == col // (c // nseq))

    def rope(x):
        rot = jnp.where(first_half, pltpu.roll(x, width - DK_A // 2, 1), pltpu.roll(x, DK_A // 2, 1))
        return x * cos + rot * sin

    q = rope(pj[:, EV_Q:EV_K])
    k = rope(pj[:, EV_K:EV_V]) * (DK_A ** -0.5)
    v = pj[:, EV_V:EV_G]
    ga = pj[:, EV_G:EV_U]
    qd = q * qdec_ref[...]
    kd = k * kdec_ref[...]
    ks = [slice(h * DK_A, (h + 1) * DK_A) for h in range(H_A)]
    vs = [slice(h * DV_A, (h + 1) * DV_A) for h in range(H_A)]
    qk = [_dot_nt(q[:, ks[h]], k[:, ks[h]]) for h in range(H_A)]
    if nseq == 1:
        inter = [_dot(qd[:, ks[h]], s_old[h]) for h in range(H_A)]
        s_new = [cdec_ref[h, 0:1, :] * s_old[h] + _dot_tn(kd[:, ks[h]], v[:, vs[h]]) for h in range(H_A)]
    else:
        inter, s_new = _retention_state_batched(qd, kd, v, s_old, cdec_ref, nseq)
    vg = _gelu(pj[:, EV_VB:EV_END])
    vrows = [_ln(vg[:, gs], mng_ref[:, gs]) for gs in vs]
    mixed = [_dot(jnp.where(tril, ws_ref[g], 0.0), vrows[g]) for g in range(G_B)]
    yield s_new
    outs = []
    for h in range(H_A):
        o = _dot(qk[h] * dmask_ref[h], v[:, vs[h]]) + inter[h]
        outs.append(_silu(ga[:, vs[h]]) * _ln(o, gng_ref[:, vs[h]]))
    u = _gelu(pj[:, EV_U:EV_VB])
    for g, gs in enumerate(vs):
        outs.append(u[:, gs] * (mixed[g] + bs_ref[:, gs]))
    yield jnp.concatenate(outs, axis=1), s_new, jnp.concatenate(vrows, axis=1)


def _even_body(x_ref, ng_ref, win_ref, wout_ref, cos_ref, sin_ref, dmask_ref, qdec_ref, kdec_ref, cdec_ref, gng_ref,
               mng_ref, ws_ref, bs_ref, s0_ref, y_ref, sout_ref, vrows_ref, *, nseg):
    consts = (dmask_ref, qdec_ref, kdec_ref, cdec_ref, gng_ref, mng_ref, ws_ref, bs_ref)
    x = x_ref[...]
    pj = jnp.dot(_rms(x, ng_ref[...]).astype(BF16), win_ref[...], preferred_element_type=F32)
    (o, s_new, vrows), = _interleave(_even_segment(pj, cos_ref[...], sin_ref[...], consts, s0_ref, nseq=nseg))
    for h in range(H_A):
        sout_ref[:, h] = s_new[h]
    y_ref[...] = x + jnp.dot(o.astype(BF16), wout_ref[...], preferred_element_type=F32)
    vrows_ref[...] = vrows


def _even_mixer(x, pos, ng, w_in, w_out, s0, gn_g, mn_g, w_s, b_s, *, nseg):
    t, d = x.shape
    c = pos.shape[0]
    r = nseg * c
    consts, shapes = _even_consts(c, gn_g, mn_g, w_s, b_s, nseq=nseg)
    cos, sin = _rope_tables(pos, DK_A, RET_THETA, DK_A, H_A, row_reps=nseg)
    rowb = pl.BlockSpec((r, d), lambda i: (i, 0))
    st_spec = pl.BlockSpec((nseg, H_A, DK_A, DV_A), lambda i: (i, 0, 0, 0))
    return pl.pallas_call(
        functools.partial(_even_body, nseg=nseg),
        grid=(t // r,),
        in_specs=[rowb, _resident((1, d)), w_in.spec(), w_out.spec(), _resident((r, 256)), _resident((r, 256))]
        + [_resident(s) for s in shapes] + [st_spec],
        out_specs=[rowb, st_spec, pl.BlockSpec((r, 512), lambda i: (i, 0))],
        out_shape=[jax.ShapeDtypeStruct((t, d), F32), jax.ShapeDtypeStruct(s0.shape, F32),
                   jax.ShapeDtypeStruct((t, 512), F32)],
        compiler_params=_cparams(("arbitrary",)),
        name="even_mixer",
    )(x, ng.reshape(1, d), w_in.array, w_out.array, cos, sin, *consts, s0)


def _even_layer_body(x_ref, ng_ref, win_ref, wout_ref, cos_ref, sin_ref, dmask_ref, qdec_ref, kdec_ref, cdec_ref,
                     gng_ref, mng_ref, ws_ref, bs_ref, y_ref, sout_ref, st_ref, *, c, pb):
    step = pl.program_id(1)

    @pl.when(step == 0)
    def _():
        st_ref[...] = jnp.zeros_like(st_ref)

    consts = (dmask_ref, qdec_ref, kdec_ref, cdec_ref, gng_ref, mng_ref, ws_ref, bs_ref)
    x = x_ref[...]
    h = _rms(x, ng_ref[...]).astype(BF16)
    ngrp = x.shape[0] // pb
    project = lambda g: jnp.dot(h[g * pb:(g + 1) * pb, :], win_ref[...], preferred_element_type=F32)
    proj = project(0)
    state = [st_ref[hh] for hh in range(H_A)]
    pending = None
    mixed = []

    def finish(rows, seg):
        o, _, _ = next(seg)
        mixed.append((rows, o.astype(BF16)))
        if len(mixed) == pb // c:
            rs = slice(mixed[0][0].start, mixed[-1][0].stop)
            o_all = jnp.concatenate([m[1] for m in mixed], axis=0)
            y_ref[rs, :] = x[rs, :] + jnp.dot(o_all, wout_ref[...], preferred_element_type=F32)
            mixed.clear()

    for g in range(ngrp):
        nxt = project(g + 1) if g + 1 < ngrp else None
        for j in range(pb // c):
            rows = slice(g * pb + j * c, g * pb + (j + 1) * c)
            seg = _even_segment(proj[j * c:(j + 1) * c, :], cos_ref[rows, :], sin_ref[rows, :], consts, state)
            state = next(seg)
            if pending is not None:
                finish(*pending)
            pending = (rows, seg)
        proj = nxt
    finish(*pending)
    for hh in range(H_A):
        st_ref[hh] = state[hh]

    @pl.when(step == pl.num_programs(1) - 1)
    def _():
        sout_ref[0] = st_ref[...]


def _even_layer(x, pos, ng, w_in, w_out, gn_g, mn_g, w_s, b_s, *, n_seq, r, pb):
    t, d = x.shape
    seq_len = pos.shape[0]
    c = math.gcd(seq_len, RET_CHUNK)
    assert c == min(MLP_CHUNK, seq_len) and pb % c == 0 and r % pb == 0 and seq_len % r == 0
    steps = seq_len // r
    consts, shapes = _even_consts(c, gn_g, mn_g, w_s, b_s)
    cos, sin = _rope_tables(pos, DK_A, RET_THETA, DK_A, H_A)
    rowb = pl.BlockSpec((r, d), lambda b, i: (b * steps + i, 0))
    tab = pl.BlockSpec((r, 256), lambda b, i: (i, 0))
    st_spec = pl.BlockSpec((1, H_A, DK_A, DV_A), lambda b, i: (b, 0, 0, 0))
    return pl.pallas_call(
        functools.partial(_even_layer_body, c=c, pb=pb),
        grid=(n_seq, steps),
        in_specs=[rowb, _resident((1, d)), w_in.spec(), w_out.spec(), tab, tab]
        + [_resident(s) for s in shapes],
        out_specs=[rowb, st_spec],
        out_shape=[jax.ShapeDtypeStruct((t, d), F32), jax.ShapeDtypeStruct((n_seq, H_A, DK_A, DV_A), F32)],
        scratch_shapes=[pltpu.VMEM((H_A, DK_A, DV_A), F32)],
        compiler_params=_cparams(("arbitrary", "arbitrary")),
        name="even_layer",
    )(x, ng.reshape(1, d), w_in.array, w_out.array, cos, sin, *consts)


def _hgrn_lower_bound(lbl, layer):
    e = jnp.exp(lbl - jnp.max(lbl, axis=0, keepdims=True))
    sm = e / jnp.sum(e, axis=0, keepdims=True)
    acc = sm[0:1, :]
    first = acc
    for i in range(1, layer + 1):
        acc = acc + sm[i:i + 1, :]
    return acc - first


def _block_row(x, blk, j):
    n, lanes = x.shape
    if blk % 16 == 0:
        return jnp.concatenate(
            [jnp.broadcast_to(x[i * blk + j:i * blk + j + 1, :], (blk, lanes)) for i in range(n // blk)], axis=0)
    x3 = x.reshape(n // 8, 8, lanes)
    pick = lambda jj: jnp.broadcast_to(x3[:, jj:jj + 1, :], x3.shape).reshape(n, lanes)
    if blk == 8:
        return pick(j)
    assert blk == 4
    upper = (lax.broadcasted_iota(jnp.int32, (n, 1), 0) % 8) >= 4
    return jnp.where(upper, pick(4 + j), pick(j))


def _hgrn_gates(pj, lb):
    f = lb + (1.0 - lb) * _sigmoid(pj[:, OD_F:OD_I])
    return f, jnp.log(f), 1.0 - f, _silu(pj[:, OD_Q:OD_F]), pj[:, OD_I:OD_G], pj[:, OD_G:OD_QD]


def _chunk_cumsum(g, c):
    n = g.shape[0]
    row = lax.broadcasted_iota(jnp.int32, (n, n), 0)
    col = lax.broadcasted_iota(jnp.int32, (n, n), 1)
    tri = ((col <= row) & (row // c == col // c)).astype(BF16)
    return sum(jnp.dot(tri, piece, preferred_element_type=F32) for piece in _split3(g))


def _hgrn_intra(q, k, f, b, c):
    return _hgrn_assemble(_hgrn_level_products(q, k, f, b, c))


def _hgrn_level_products(q, k, f, b, c):
    n = q.shape[0]
    rloc = lax.broadcasted_iota(jnp.int32, (n, 1), 0)
    diag = jnp.sum(q * k, axis=-1, keepdims=True)
    prods = []
    blk = 2
    while blk <= c:
        second = (rloc % blk) >= blk // 2
        if blk == 2:
            e = f
            ke = k
        else:
            dq = b - _block_row(b, blk, blk // 2 - 1)
            e = jnp.exp(jnp.where(second, dq, -dq))
            ke = k * e
        prods.append((blk, _dot_nt(jnp.where(second, q * e, 0.0), jnp.where(second, 0.0, ke))))
        blk *= 2
    return diag, prods


def _hgrn_assemble(level_products):
    diag, prods = level_products
    n = diag.shape[0]
    row = lax.broadcasted_iota(jnp.int32, (n, n), 0)
    col = lax.broadcasted_iota(jnp.int32, (n, n), 1)
    a = jnp.where(row == col, diag, 0.0)
    for blk, p in prods:
        a = a + jnp.where((row // blk) == (col // blk), p, 0.0)
    return a


def _interleave(*gens):
    results = [None] * len(gens)
    live = list(enumerate(gens))
    while live:
        still = []
        for i, gen in live:
            try:
                results[i] = next(gen)
                still.append((i, gen))
            except StopIteration:
                pass
        live = still
    return results


def _hgrn_tile(pj, lb, ong_ref, st_ref, c):
    n = pj.shape[0]
    heads = [slice(h * DK_C, (h + 1) * DK_C) for h in range(H_C)]
    chunks = [slice(j * c, (j + 1) * c) for j in range(n // c)]
    f, g, k, q, v, gc = _hgrn_gates(pj, lb)
    b = _chunk_cumsum(g, c)
    qe = q * jnp.exp(b)
    kd = k * jnp.exp(_block_row(b, c, c - 1) - b)
    levels = [_hgrn_level_products(q[:, hs], k[:, hs], f[:, hs], b[:, hs], c) for hs in heads]
    kv = [[_dot_tn(kd[rs, hs], v[rs, hs]) for rs in chunks] for hs in heads]
    yield None
    o_intra = [_dot(_hgrn_assemble(levels[h]), v[:, hs]) for h, hs in enumerate(heads)]
    yield None
    outs = []
    for h, hs in enumerate(heads):
        s = st_ref[h]
        parts = []
        for j, rs in enumerate(chunks):
            parts.append(o_intra[h][rs, :] + _dot(qe[rs, hs], s))
            bl = b[rs.stop - 1:rs.stop, hs]
            decay = jnp.broadcast_to(jnp.exp(bl), (DK_C, DK_C)).T
            s = decay * s + kv[h][j]
        st_ref[h] = s
        outs.append(_rms(jnp.concatenate(parts, axis=0), ong_ref[:, hs]) * _silu(gc[:, hs]))
    yield jnp.concatenate(outs, axis=1)


def _hgrn_body(x_ref, ng_ref, win_ref, wout_ref, lbl_ref, ong_ref, s0_ref, y_ref, sout_ref, *, c, layer):
    x = x_ref[...]
    pj = jnp.dot(_rms(x, ng_ref[...]).astype(BF16), win_ref[:, OD_Q:OD_QD], preferred_element_type=F32)
    n = pj.shape[0]
    ns = n // c
    lb = _hgrn_lower_bound(lbl_ref[...], layer)
    f, g, k, q, v, gc = _hgrn_gates(pj, lb)
    b = _chunk_cumsum(g, c)
    blast = _block_row(b, c, c - 1)
    qe = q * jnp.exp(b)
    kd = k * jnp.exp(blast - b)
    own_cols = (lax.broadcasted_iota(jnp.int32, (n, ns * DK_C), 1) // DK_C
                == lax.broadcasted_iota(jnp.int32, (n, ns * DK_C), 0) // c)
    place = lambda x: jnp.where(own_cols, jnp.concatenate([x] * ns, axis=1), 0.0)
    seq_cols = (lax.broadcasted_iota(jnp.int32, (ns, ns * DK_C), 1) // DK_C
                == lax.broadcasted_iota(jnp.int32, (ns, ns * DK_C), 0))
    ones = jnp.ones((ns, DV_C), BF16)
    last_row = (lax.broadcasted_iota(jnp.int32, (ns, n), 1)
                == lax.broadcasted_iota(jnp.int32, (ns, n), 0) * c + (c - 1)).astype(BF16)
    b_last = sum(jnp.dot(last_row, piece, preferred_element_type=F32) for piece in _split3(b))
    outs = []
    for h in range(H_C):
        hs = slice(h * DK_C, (h + 1) * DK_C)
        vh = v[:, hs]
        s = s0_ref[:, h].reshape(ns * DK_C, DV_C)
        o = _dot(_hgrn_intra(q[:, hs], k[:, hs], f[:, hs], b[:, hs], c), vh) + _dot(place(qe[:, hs]), s)
        e_last = jnp.exp(b_last[:, hs])
        e_placed = jnp.where(seq_cols, jnp.concatenate([e_last] * ns, axis=1), 0.0)
        decay = sum(lax.dot_general(piece, ones, (((0,), (0,)), ((), ())), preferred_element_type=F32)
                    for piece in _split3(e_placed))
        s_new = decay * s + _dot_tn(place(kd[:, hs]), vh)
        sout_ref[:, h] = s_new.reshape(ns, DK_C, DV_C)
        outs.append(_rms(o, ong_ref[:, hs]) * _silu(gc[:, hs]))
    o_c = jnp.concatenate(outs, axis=1).astype(BF16)
    y_ref[...] = x + jnp.dot(o_c, wout_ref[0:H_C * DV_C, :], preferred_element_type=F32)


def _hgrn_mixer(x, ng, w_in, w_out, lb_logits, on_g, s0, *, c, nseq, layer):
    t, d = x.shape
    r = nseq * c
    rowb = pl.BlockSpec((r, d), lambda i: (i, 0))
    st_spec = pl.BlockSpec((nseq, H_C, DK_C, DV_C), lambda i: (i, 0, 0, 0))
    return pl.pallas_call(
        functools.partial(_hgrn_body, c=c, layer=layer),
        grid=(t // r,),
        in_specs=[rowb, _resident((1, d)), w_in.spec(), w_out.spec(), _resident((DEPTH, 512)), _resident((1, 512)),
                  st_spec],
        out_specs=[rowb, st_spec],
        out_shape=[jax.ShapeDtypeStruct((t, d), F32), jax.ShapeDtypeStruct(s0.shape, F32)],
        compiler_params=_cparams(("arbitrary",)),
        name="hgrn_mixer",
    )(x, ng.reshape(1, d), w_in.array, w_out.array, lb_logits, on_g.reshape(1, 512), s0)


def _head_norm(x, g):
    rows, width = x.shape
    lo = lax.broadcasted_iota(jnp.int32, (rows, 128), 1) < HD_D
    outs = []
    for t in range(width // 128):
        xt = x[:, t * 128:(t + 1) * 128]
        sq = xt * xt
        ss_lo = jnp.sum(jnp.where(lo, sq, 0.0), axis=-1, keepdims=True)
        ss_hi = jnp.sum(jnp.where(lo, 0.0, sq), axis=-1, keepdims=True)
        scale = jnp.where(lo, lax.rsqrt(ss_lo * (1.0 / HD_D) + EPS), lax.rsqrt(ss_hi * (1.0 / HD_D) + EPS))
        outs.append(xt * scale)
    return jnp.concatenate(outs, axis=1) * g


def _rope_partial(x, cos, sin):
    width = x.shape[1]
    half = ROT_DIM_D // 2
    lane = lax.broadcasted_iota(jnp.int32, x.shape, 1)
    rot = jnp.where((lane % HD_D) < half, pltpu.roll(x, width - half, 1), pltpu.roll(x, half, 1))
    return x * cos + rot * sin


def _swa_prep(q, k, cos, sin, qng_ref, kng_ref):
    qn = _rope_partial(_head_norm(q, qng_ref[...]), jnp.concatenate([cos] * (HQ_D // 2), axis=1),
                       jnp.concatenate([sin] * (HQ_D // 2), axis=1)).astype(BF16)
    return qn, _rope_partial(_head_norm(k, kng_ref[...]), cos, sin)


def _swa_place(x):
    lo = lax.broadcasted_iota(jnp.int32, x.shape, 1) < HD_D
    sw = pltpu.roll(x, HD_D, 1)
    return [[jnp.where(lo, x, 0.0).astype(BF16), jnp.where(lo, 0.0, sw).astype(BF16)],
            [jnp.where(lo, sw, 0.0).astype(BF16), jnp.where(lo, 0.0, x).astype(BF16)]]


def _swa_block(qn, k_prev, k_cur, v_prev, v_cur, sink_ref, allow_prev):
    rep = HQ_D // HKV_D
    lo = lax.broadcasted_iota(jnp.int32, (WINDOW, 128), 1) < HD_D
    qi = lax.broadcasted_iota(jnp.int32, (WINDOW, 2 * WINDOW), 0)
    ci = lax.broadcasted_iota(jnp.int32, (WINDOW, 2 * WINDOW), 1)
    cur = (ci >= WINDOW) & ((ci - WINDOW) <= qi)
    prev = (ci < WINDOW) & (ci > qi)
    mask = prev | cur if allow_prev is True else (prev & allow_prev) | cur
    keys = lambda g, half: jnp.concatenate([k_prev[g][half], k_cur[g][half]], axis=0)
    ones = [jnp.where(lo, 1.0, 0.0).astype(BF16), jnp.where(lo, 0.0, 1.0).astype(BF16)]
    vals = lambda g, half: jnp.concatenate(
        [jnp.concatenate([v_prev[g][half], v_cur[g][half]], axis=0), jnp.concatenate([ones[half]] * 2, axis=0)], axis=1)
    scores = [lax.dot_general(qn[:, (h // 2) * 128:(h // 2 + 1) * 128], keys(h // rep, h % 2),
                              (((1,), (1,)), ((), ())), preferred_element_type=F32) for h in range(HQ_D)]
    yield None
    pv, sink_term = [], []
    for h in range(HQ_D):
        s = jnp.where(mask, scores[h] * (HD_D ** -0.5), -jnp.inf)
        sink = sink_ref[h:h + 1, 0:1]
        m = jnp.maximum(jnp.max(s, axis=-1, keepdims=True), sink)
        pv.append(jnp.dot(jnp.exp(s - m).astype(BF16), vals(h // rep, h % 2), preferred_element_type=F32))
        sink_term.append(jnp.exp(sink - m))
    yield None
    tiles = []
    for t in range(HQ_D // 2):
        both = pv[2 * t] + pv[2 * t + 1]
        den = both[:, 128:] + jnp.where(lo, sink_term[2 * t], sink_term[2 * t + 1])
        tiles.append(both[:, :128] / den)
    yield jnp.concatenate(tiles, axis=1)


def _swa_sample_chain(q2, kt_cache, k_new, vt_cache, v_new, sinks2, half, l):
    rows = q2.shape[0]
    ns = rows // (2 * l)
    lo = lax.broadcasted_iota(jnp.int32, (1, 128), 1) < HD_D
    ones = jnp.where(lo, 1.0, 0.0) if half == 0 else jnp.where(lo, 0.0, 1.0)
    ext = lambda v: jnp.concatenate([v, jnp.broadcast_to(ones, v.shape).astype(BF16)], axis=1)
    nt = lambda a, b: lax.dot_general(a, b, (((1,), (1,)), ((), ())), preferred_element_type=F32)
    zeros = jnp.zeros((HD_D, kt_cache.shape[1]), BF16)
    band = lambda x: jnp.concatenate([x, zeros] if half == 0 else [zeros, x], axis=0)
    k_cache = band(kt_cache.astype(BF16))
    v_cache = jnp.concatenate([band(vt_cache.astype(BF16)), band(jnp.ones_like(zeros))], axis=0)
    s1 = jnp.dot(q2, k_cache, preferred_element_type=F32)
    s2 = nt(q2, k_new)
    yield None
    r1 = lax.broadcasted_iota(jnp.int32, s1.shape, 0)
    c1 = lax.broadcasted_iota(jnp.int32, s1.shape, 1)
    ok1 = ((r1 % (ns * l)) // l == c1 // WINDOW) & (c1 % WINDOW > r1 % l)
    r2 = lax.broadcasted_iota(jnp.int32, s2.shape, 0)
    c2 = lax.broadcasted_iota(jnp.int32, s2.shape, 1)
    ok2 = ((r2 % (ns * l)) // l == c2 // l) & (c2 % l <= r2 % l)
    s1 = jnp.where(ok1, s1 * (HD_D ** -0.5), -jnp.inf)
    s2 = jnp.where(ok2, s2 * (HD_D ** -0.5), -jnp.inf)
    sink = jnp.where(lax.broadcasted_iota(jnp.int32, (rows, 1), 0) < ns * l, sinks2[0], sinks2[1])
    m = jnp.maximum(jnp.maximum(jnp.max(s1, axis=-1, keepdims=True), jnp.max(s2, axis=-1, keepdims=True)), sink)
    res = (nt(jnp.exp(s1 - m).astype(BF16), v_cache)
           + jnp.dot(jnp.exp(s2 - m).astype(BF16), ext(v_new), preferred_element_type=F32))
    yield res, jnp.exp(sink - m)


def _swa_sample_body(x_ref, res_ref, ng_ref, win_ref, wout_ref, kc_ref, vc_ref, cos_ref, sin_ref, qng_ref, kng_ref,
                     sink_ref, y_ref, kout_ref, vout_ref, o_ref, *, nseq, l, group):
    rep = HQ_D // HKV_D
    pj = jnp.dot(_rms(x_ref[...], ng_ref[...]).astype(BF16), win_ref[:, OD_QD:OD_END], preferred_element_type=F32)
    qn, kn = _swa_prep(pj[:, 0:OD_KD - OD_QD], pj[:, OD_KD - OD_QD:OD_VD - OD_QD], cos_ref[...], sin_ref[...],
                       qng_ref, kng_ref)
    v = pj[:, OD_VD - OD_QD:]
    lo = lax.broadcasted_iota(jnp.int32, (group * l, 128), 1) < HD_D
    r_sel = lax.broadcasted_iota(jnp.int32, (group * l, group * WINDOW), 0)
    c_sel = lax.broadcasted_iota(jnp.int32, (group * l, group * WINDOW), 1)
    append = ((c_sel // WINDOW == r_sel // l) & (c_sel % WINDOW == WINDOW - l + r_sel % l)).astype(BF16)
    is_new = lax.broadcasted_iota(jnp.int32, (HKV_D * HD_D, group * WINDOW), 1) % WINDOW >= WINDOW - l

    def shifted(old_t, new_rows):
        moved = sum(lax.dot_general(piece, append, (((0,), (0,)), ((), ())), preferred_element_type=F32)
                    for piece in _split3(new_rows))
        return jnp.where(is_new, moved, pltpu.roll(old_t, group * WINDOW - l, 1))

    chains, where = [], []
    for gi in range(nseq // group):
        rs = slice(gi * group * l, (gi + 1) * group * l)
        seqs = range(gi * group, (gi + 1) * group)
        kt = [jnp.concatenate([kc_ref[s, g] for s in seqs], axis=1) for g in range(HKV_D)]
        vt = [jnp.concatenate([vc_ref[s, g] for s in seqs], axis=1) for g in range(HKV_D)]
        k_new, v_new = _swa_place(kn[rs, :]), _swa_place(v[rs, :])
        for g in range(HKV_D):
            tiles = [(g * rep) // 2, (g * rep) // 2 + 1]
            q2 = jnp.concatenate([qn[rs, t * 128:(t + 1) * 128] for t in tiles], axis=0)
            for half in range(2):
                sinks2 = [sink_ref[2 * t + half:2 * t + half + 1, 0:1] for t in tiles]
                chains.append(_swa_sample_chain(q2, kt[g], k_new[g][half], vt[g], v_new[g][half], sinks2, half, l))
                where.append((rs, tiles, half))
        k_all = shifted(jnp.concatenate(kt, axis=0), kn[rs, :])
        v_all = shifted(jnp.concatenate(vt, axis=0), v[rs, :])
        for i, s in enumerate(seqs):
            for g in range(HKV_D):
                kout_ref[s, g] = k_all[g * HD_D:(g + 1) * HD_D, i * WINDOW:(i + 1) * WINDOW]
                vout_ref[s, g] = v_all[g * HD_D:(g + 1) * HD_D, i * WINDOW:(i + 1) * WINDOW]
    results = _interleave(*chains)
    for i in range(0, len(chains), 2):
        (rs, tiles, _), (res0, st0), (res1, st1) = where[i], results[i], results[i + 1]
        both = res0 + res1
        for j, t in enumerate(tiles):
            js = slice(j * group * l, (j + 1) * group * l)
            den = both[js, 128:] + jnp.where(lo, st0[js, :], st1[js, :])
            o_ref[rs, t * 128:(t + 1) * 128] = both[js, :128] / den
    y_ref[...] = res_ref[...] + jnp.dot(o_ref[...].astype(BF16), wout_ref[H_C * DV_C:, :], preferred_element_type=F32)


def _swa_gains(qn_g, kn_g):
    return (jnp.tile(qn_g, HQ_D).reshape(1, HQ_D * HD_D), jnp.tile(kn_g, HKV_D).reshape(1, HKV_D * HD_D))


def _swa_sample(x, res, pos, ng, w_in, w_out, kcache, vcache, qn_g, kn_g, sinks, *, nseq, group):
    t, d = x.shape
    l = pos.shape[0]
    r = nseq * l
    cos, sin = _rope_tables(pos, ROT_DIM_D, ROPE_THETA, HD_D, 2, row_reps=nseq)
    qng, kng = _swa_gains(qn_g, kn_g)
    rowb = pl.BlockSpec((r, d), lambda i: (i, 0))
    cache = pl.BlockSpec((nseq, HKV_D, HD_D, WINDOW), lambda i: (i, 0, 0, 0))
    return pl.pallas_call(
        functools.partial(_swa_sample_body, nseq=nseq, l=l, group=group),
        grid=(t // r,),
        in_specs=[rowb, rowb, _resident((1, d)), w_in.spec(), w_out.spec(), cache, cache, _resident((r, 128)),
                  _resident((r, 128)), _resident((1, 512)), _resident((1, 128)), _resident((HQ_D, 128))],
        out_specs=[rowb, cache, cache],
        out_shape=[jax.ShapeDtypeStruct((t, d), F32), jax.ShapeDtypeStruct(kcache.shape, F32),
                   jax.ShapeDtypeStruct(vcache.shape, F32)],
        scratch_shapes=[pltpu.VMEM((r, HQ_D * HD_D), F32)],
        compiler_params=_cparams(("arbitrary",)),
        name="swa_sample",
    )(x, res, ng.reshape(1, d), w_in.array, w_out.array, kcache, vcache, cos, sin, qng, kng,
      jnp.broadcast_to(sinks[:, None], (HQ_D, 128)))


def _odd_layer_body(x_ref, ng_ref, win_ref, wout_ref, lbl_ref, ong_ref, cos_ref, sin_ref, qng_ref, kng_ref, sink_ref,
                    y_ref, sout_ref, kout_ref, vout_ref, st_ref, kprev_ref, vprev_ref, *, c, pb, layer):
    step = pl.program_id(1)

    @pl.when(step == 0)
    def _():
        st_ref[...] = jnp.zeros_like(st_ref)
        kprev_ref[...] = jnp.zeros_like(kprev_ref)
        vprev_ref[...] = jnp.zeros_like(vprev_ref)

    x = x_ref[...]
    h = _rms(x, ng_ref[...]).astype(BF16)
    lb = _hgrn_lower_bound(lbl_ref[...], layer)
    ngrp = x.shape[0] // pb
    project = lambda g: jnp.dot(h[g * pb:(g + 1) * pb, :], win_ref[...], preferred_element_type=F32)
    proj = project(0)
    k_last, v_last = kprev_ref[...], vprev_ref[...]
    k_prev, v_prev = _swa_place(k_last), _swa_place(v_last)
    for g in range(ngrp):
        nxt = project(g + 1) if g + 1 < ngrp else None
        grows = slice(g * pb, (g + 1) * pb)
        qn, kn = _swa_prep(proj[:, OD_QD:OD_KD], proj[:, OD_KD:OD_VD], cos_ref[grows, :], sin_ref[grows, :],
                           qng_ref, kng_ref)
        staged = []
        for u in range(pb // WINDOW):
            us = slice(u * WINDOW, (u + 1) * WINDOW)
            k_last, v_last = kn[us, :], proj[us, OD_VD:OD_END]
            k_cur, v_cur = _swa_place(k_last), _swa_place(v_last)
            staged.append(_hgrn_tile(proj[us, :], lb, ong_ref, st_ref, c))
            staged.append(_swa_block(qn[us, :], k_prev, k_cur, v_prev, v_cur, sink_ref,
                                     True if g + u > 0 else step > 0))
            k_prev, v_prev = k_cur, v_cur
        outs = _interleave(*staged)
        o = jnp.concatenate([jnp.concatenate(outs[0::2], axis=0), jnp.concatenate(outs[1::2], axis=0)], axis=1)
        y_ref[grows, :] = x[grows, :] + jnp.dot(o.astype(BF16), wout_ref[...], preferred_element_type=F32)
        proj = nxt
    kprev_ref[...] = k_last
    vprev_ref[...] = v_last
    kout_ref[0] = k_last
    vout_ref[0] = v_last

    @pl.when(step == pl.num_programs(1) - 1)
    def _():
        sout_ref[0] = st_ref[...]


def _odd_layer(x, pos, ng, w_in, w_out, lb_logits, on_g, qn_g, kn_g, sinks, *, n_seq, r, pb, layer):
    t, d = x.shape
    seq_len = pos.shape[0]
    c = math.gcd(seq_len, HGRN_CHUNK)
    assert WINDOW % c == 0 and pb % WINDOW == 0 and r % pb == 0 and seq_len % r == 0
    steps = seq_len // r
    cos, sin = _rope_tables(pos, ROT_DIM_D, ROPE_THETA, HD_D, 2)
    qng, kng = _swa_gains(qn_g, kn_g)
    rowb = pl.BlockSpec((r, d), lambda b, i: (b * steps + i, 0))
    tab = pl.BlockSpec((r, 128), lambda b, i: (i, 0))
    st_spec = pl.BlockSpec((1, H_C, DK_C, DV_C), lambda b, i: (b, 0, 0, 0))
    kv_out = pl.BlockSpec((1, WINDOW, 128), lambda b, i: (b, 0, 0))
    return pl.pallas_call(
        functools.partial(_odd_layer_body, c=c, pb=pb, layer=layer),
        grid=(n_seq, steps),
        in_specs=[rowb, _resident((1, d)), w_in.spec(), w_out.spec(), _resident((DEPTH, 512)),
                  _resident((1, 512)), tab, tab, _resident((1, 512)), _resident((1, 128)), _resident((HQ_D, 128))],
        out_specs=[rowb, st_spec, kv_out, kv_out],
        out_shape=[jax.ShapeDtypeStruct((t, d), F32), jax.ShapeDtypeStruct((n_seq, H_C, DK_C, DV_C), F32),
                   jax.ShapeDtypeStruct((n_seq, WINDOW, 128), F32), jax.ShapeDtypeStruct((n_seq, WINDOW, 128), F32)],
        scratch_shapes=[pltpu.VMEM((H_C, DK_C, DV_C), F32), pltpu.VMEM((WINDOW, 128), F32),
                        pltpu.VMEM((WINDOW, 128), F32)],
        compiler_params=_cparams(("arbitrary", "arbitrary")),
        name="odd_layer",
    )(x, ng.reshape(1, d), w_in.array, w_out.array, lb_logits, on_g.reshape(1, 512), cos, sin, qng, kng,
      jnp.broadcast_to(sinks[:, None], (HQ_D, 128)))


def _mem_kv_body(m_ref, w_ref, g_ref, k_ref, v_ref):
    kv = jnp.dot(m_ref[...].astype(BF16), w_ref[...], preferred_element_type=F32)
    hw = H_X * HD_X
    for h in range(H_X):
        hs = slice(h * HD_X, (h + 1) * HD_X)
        k_ref[:, hs] = _rms(kv[:, hs], g_ref[...])
    v_ref[...] = kv[:, hw:]


def _mem_kv(mem, w, g):
    t, d = mem.shape
    hw = H_X * HD_X
    whole = lambda shape: pl.BlockSpec(shape, lambda i: (0,) * len(shape))
    return pl.pallas_call(
        _mem_kv_body,
        grid=(1,),
        in_specs=[whole((t, d)), w.spec(), whole((1, HD_X))],
        out_specs=[whole((t, hw)), whole((t, hw))],
        out_shape=[jax.ShapeDtypeStruct((t, hw), F32), jax.ShapeDtypeStruct((t, hw), F32)],
        compiler_params=_cparams(("arbitrary",)),
        name="mem_kv",
    )(mem, w.array, g.reshape(1, HD_X))


def _mem_attend_rows(x, g_ref, wq_ref, qng_ref, mk_ref, mv_ref, wo_ref):
    heads = [slice(h * HD_X, (h + 1) * HD_X) for h in range(H_X)]
    q = jnp.dot(_rms(x, g_ref[...]).astype(BF16), wq_ref[...], preferred_element_type=F32)
    yield None
    scores = [_dot_nt(_rms(q[:, hs], qng_ref[...]), mk_ref[0, :, hs]) for hs in heads]
    yield None
    outs = []
    for hs, sc in zip(heads, scores):
        sc = sc * (HD_X ** -0.5)
        p = jnp.exp(sc - jnp.max(sc, axis=-1, keepdims=True))
        outs.append(_dot(p, mv_ref[0, :, hs]) / jnp.sum(p, axis=-1, keepdims=True))
    yield None
    yield x + jnp.dot(jnp.concatenate(outs, axis=1).astype(BF16), wo_ref[...], preferred_element_type=F32)


def _mem_attend_body(x_ref, g_ref, wq_ref, qng_ref, mk_ref, mv_ref, wo_ref, y_ref, *, nsplit):
    rows = x_ref.shape[0] // nsplit
    parts = _interleave(*[_mem_attend_rows(x_ref[i * rows:(i + 1) * rows, :], g_ref, wq_ref, qng_ref, mk_ref, mv_ref,
                                           wo_ref) for i in range(nsplit)])
    for i, y in enumerate(parts):
        y_ref[i * rows:(i + 1) * rows, :] = y


def _mem_attend(x, g, wq, qn_g, mk, mv, wo, *, tm, nsplit, tiles_per_mem):
    t, d = x.shape
    hw = H_X * HD_X
    mem_spec = pl.BlockSpec((1, N_MEM, hw), lambda i: (i // tiles_per_mem, 0, 0))
    return pl.pallas_call(
        functools.partial(_mem_attend_body, nsplit=nsplit),
        grid=(t // tm,),
        in_specs=[pl.BlockSpec((tm, d), lambda i: (i, 0)), _resident((1, d)), wq.spec(),
                  _resident((1, HD_X)), mem_spec, mem_spec, wo.spec()],
        out_specs=pl.BlockSpec((tm, d), lambda i: (i, 0)),
        out_shape=jax.ShapeDtypeStruct((t, d), F32),
        compiler_params=_cparams(("arbitrary",)),
        name="mem_attend",
    )(x, g.reshape(1, d), wq.array, qn_g.reshape(1, HD_X), mk, mv, wo.array)


def _mem_attend_cached_body(x_ref, g_ref, wq_ref, qng_ref, mk_ref, mv_ref, wo_ref, y_ref, att_ref, *, nseq):
    x = x_ref[...]
    q = jnp.dot(_rms(x, g_ref[...]).astype(BF16), wq_ref[...], preferred_element_type=F32)
    l = x.shape[0] // nseq
    rows_all = H_X * l
    cols = N_MEM * H_X
    own = (lax.broadcasted_iota(jnp.int32, (rows_all, cols), 1) % H_X
           == lax.broadcasted_iota(jnp.int32, (rows_all, cols), 0) // l)
    scores = []
    for s in range(nseq):
        rs = slice(s * l, (s + 1) * l)
        qs = jnp.concatenate([_rms(q[rs, h * HD_X:(h + 1) * HD_X], qng_ref[...]) for h in range(H_X)], axis=0)
        scores.append(_dot_nt(qs, mk_ref[0, s]))
    for s in range(nseq):
        rs = slice(s * l, (s + 1) * l)
        sc = jnp.where(own, scores[s] * (HD_X ** -0.5), -jnp.inf)
        m = jnp.max(sc, axis=-1, keepdims=True)
        p = jnp.exp(sc - m)
        o = _dot(p, mv_ref[0, s]) / jnp.sum(p, axis=-1, keepdims=True)
        for h in range(H_X):
            att_ref[rs, h * HD_X:(h + 1) * HD_X] = o[h * l:(h + 1) * l, :]
    y_ref[...] = x + jnp.dot(att_ref[...].astype(BF16), wo_ref[...], preferred_element_type=F32)


def _mem_attend_cached(x, g, wq, qn_g, mk, mv, wo, *, layer, nseq):
    t, d = x.shape
    hw = H_X * HD_X
    tm = nseq * (t // mk.shape[1])
    full = lambda shape: pl.BlockSpec(shape, lambda i: (0,) * len(shape))
    mem_spec = pl.BlockSpec((1, nseq, N_MEM * H_X, HD_X), lambda i: (layer, i, 0, 0))
    return pl.pallas_call(
        functools.partial(_mem_attend_cached_body, nseq=nseq),
        grid=(t // tm,),
        in_specs=[pl.BlockSpec((tm, d), lambda i: (i, 0)), full((1, d)), wq.spec(), full((1, HD_X)),
                  mem_spec, mem_spec, wo.spec()],
        out_specs=pl.BlockSpec((tm, d), lambda i: (i, 0)),
        out_shape=jax.ShapeDtypeStruct((t, d), F32),
        scratch_shapes=[pltpu.VMEM((tm, hw), F32)],
        compiler_params=_cparams(("arbitrary",)),
        name="mem_attend_cached",
    )(x, g.reshape(1, d), wq.array, qn_g.reshape(1, HD_X), mk, mv, wo.array)


def _ffn_body(*refs, tm, ffc, tiles_per_seq, per_row_state):
    if per_row_state:
        x_ref, g_ref, wg_ref, wu_ref, cw_ref, cb_ref, wd_ref, c0_ref, y_ref, gt_ref, tail_ref = refs
    else:
        x_ref, g_ref, wg_ref, wu_ref, cw_ref, cb_ref, wd_ref, y_ref, gt_ref, tail_ref = refs
    ff = wg_ref.shape[1]
    x = x_ref[...]
    h = _rms(x, g_ref[...]).astype(BF16)
    hist = CONV_W - 1
    if per_row_state:
        ns = tm // 8
        r_in = lax.broadcasted_iota(jnp.int32, (tm, hist * ns), 0)
        c_in = lax.broadcasted_iota(jnp.int32, (tm, hist * ns), 1)
        same = (r_in // 8) == (c_in // hist)
        sel1 = (same & (r_in % 8 == 0) & (c_in % hist == hist - 1)).astype(BF16)
        sel2 = (same & (r_in % 8 < hist) & (c_in % hist == r_in % 8)).astype(BF16)
        r_out = lax.broadcasted_iota(jnp.int32, (hist * ns, tm), 0)
        c_out = lax.broadcasted_iota(jnp.int32, (hist * ns, tm), 1)
        sel_out = (((c_out // 8) == (r_out // hist)) & (c_out % 8 == 8 - hist + r_out % hist)).astype(BF16)

        def select(sel, val):
            return sum(jnp.dot(sel, piece, preferred_element_type=F32) for piece in _split3(val))
    else:
        @pl.when(pl.program_id(0) % tiles_per_seq == 0)
        def _():
            tail_ref[...] = jnp.zeros_like(tail_ref)

    row8 = lax.broadcasted_iota(jnp.int32, (8, ffc), 0)
    acc = None

    def down(acts):
        a = acts[0][0] if len(acts) == 1 else jnp.concatenate([p[0] for p in acts], axis=1)
        part = jnp.dot(a, wd_ref[acts[0][1].start:acts[-1][1].stop, :], preferred_element_type=F32)
        return part if acc is None else acc + part

    filling, ready = [], None
    for c0 in range(0, ff, ffc):
        cs = slice(c0, c0 + ffc)
        gate = jnp.dot(h, wg_ref[:, cs], preferred_element_type=F32)
        up = jnp.dot(h, wu_ref[:, cs], preferred_element_type=F32)
        if ready is not None:
            acc = down(ready)
            ready = None
        r1 = pltpu.roll(gate, 1, 0)
        r2 = pltpu.roll(gate, 2, 0)
        if per_row_state:
            pos = lax.broadcasted_iota(jnp.int32, gate.shape, 0) % 8
            c0c = c0_ref[:, cs]
            g1 = jnp.where(pos == 0, select(sel1, c0c), r1)
            g2 = jnp.where(pos < hist, select(sel2, c0c), r2)
            gt_ref[:, cs] = select(sel_out, gate)
        else:
            prev = tail_ref[:, cs]
            top1 = jnp.where(row8 == 0, prev[7:8, :], r1[0:8, :])
            top2 = jnp.where(row8 == 0, prev[6:7, :], jnp.where(row8 == 1, prev[7:8, :], r2[0:8, :]))
            g1 = jnp.concatenate([top1, r1[8:, :]], axis=0)
            g2 = jnp.concatenate([top2, r2[8:, :]], axis=0)
            tail_ref[:, cs] = gate[tm - 8:, :]
            gt_ref[:, cs] = gate[tm - 8:, :]
        conv = cb_ref[:, cs] + cw_ref[0:1, cs] * g2 + cw_ref[1:2, cs] * g1 + cw_ref[2:3, cs] * gate
        filling.append(((_gelu(conv) * up).astype(BF16), cs))
        if len(filling) == FFN_DOWN_GROUP:
            filling, ready = [], filling
    for acts in (ready, filling):
        if acts:
            acc = down(acts)
    y_ref[...] = x + acc


def _ffn(x, g, wg, wu, cw, cb, wd, *, tm, ffc, tiles_per_seq, c0=None):
    t, d = x.shape
    ff = wg.shape[1]
    per_row_state = c0 is not None
    tail_n = tm // 8 * (CONV_W - 1) if per_row_state else 8
    in_specs = [pl.BlockSpec((tm, d), lambda i: (i, 0)), _resident((1, d)), wg.spec(), wu.spec(),
                _resident((8, ff)), _resident((1, ff)), wd.spec()]
    args = [x, g.reshape(1, d), wg.array, wu.array, jnp.pad(cw, ((0, 8 - CONV_W), (0, 0))), cb.reshape(1, ff),
            wd.array]
    if per_row_state:
        in_specs += [pl.BlockSpec((tail_n, ff), lambda i: (i, 0))]
        args += [c0]
    return pl.pallas_call(
        functools.partial(_ffn_body, tm=tm, ffc=ffc, tiles_per_seq=tiles_per_seq, per_row_state=per_row_state),
        grid=(t // tm,),
        in_specs=in_specs,
        out_specs=[pl.BlockSpec((tm, d), lambda i: (i, 0)), pl.BlockSpec((tail_n, ff), lambda i: (i, 0))],
        out_shape=[jax.ShapeDtypeStruct((t, d), F32), jax.ShapeDtypeStruct((t // tm * tail_n, ff), F32)],
        scratch_shapes=[pltpu.VMEM((8, ff), F32)],
        compiler_params=_cparams(("arbitrary",)),
        name="ffn",
    )(*args)


def _run_prompt(x, pos, n_seq, mem, w):
    seq_len = pos.shape[0]
    tm = 512
    hw = H_X * HD_X
    out = {k: [] for k in ("ret", "hgrn", "swa_k", "swa_v", "mem_k", "mem_v", "conv")}
    for l in range(DEPTH):
        j = l // 2
        if l % 2 == 0:
            x, s_new = _even_layer(x, pos, w["norm_mix_g"][l], w["ev_w_in"][j], w["ev_w_out"][j], w["ret_gn_g"][j],
                                   w["mlp_norm_g"][j], w["mlp_w_s"][j], w["mlp_b_s"][j], n_seq=n_seq, r=1024, pb=512)
            out["ret"].append(s_new)
        else:
            x, s_new, k_new, v_new = _odd_layer(x, pos, w["norm_mix_g"][l], w["od_w_in"][j], w["od_w_out"][j],
                                                w["hgrn_lb_logits"], w["hgrn_onorm_g"][j], w["swa_qnorm_g"][j],
                                                w["swa_knorm_g"][j], w["swa_sinks"][j], n_seq=n_seq, r=1024, pb=256,
                                                layer=l)
            out["hgrn"].append(s_new)
            out["swa_k"].append(k_new.reshape(n_seq, WINDOW, HKV_D, HD_D))
            out["swa_v"].append(v_new.reshape(n_seq, WINDOW, HKV_D, HD_D))
        mk, mv = _mem_kv(mem, w["mem_w_kv"][l], w["mem_knorm_g"][l])
        out["mem_k"].append(mk.reshape(n_seq, N_MEM, H_X, HD_X))
        out["mem_v"].append(mv.reshape(n_seq, N_MEM, H_X, HD_X))
        x = _mem_attend(x, w["norm_mem_g"][l], w["mem_w_q"][l], w["mem_qnorm_g"][l], mk.reshape(n_seq, N_MEM, hw),
                        mv.reshape(n_seq, N_MEM, hw), w["mem_w_o"][l], tm=2 * tm, nsplit=4,
                        tiles_per_mem=seq_len // (2 * tm))
        x, gt = _ffn(x, w["norm_ffn_g"][l], w["ffn_w_gate"][l], w["ffn_w_up"][l], w["ffn_conv_w"][l],
                     w["ffn_conv_b"][l], w["ffn_w_down"][l], tm=tm, ffc=256, tiles_per_seq=seq_len // tm)
        out["conv"].append(gt.reshape(n_seq, seq_len // tm, 8, D_FF)[:, -1, 8 - (CONV_W - 1):, :])
    return x, {name: jnp.stack(rows) for name, rows in out.items()}


def _run_sample(x, pos, n_seq, st, w):
    seq_len = pos.shape[0]
    assert seq_len == 8
    t = x.shape[0]
    out = {k: [] for k in ("ret", "chunk_v", "hgrn", "swa_k", "swa_v", "conv")}
    for l in range(DEPTH):
        j = l // 2
        if l % 2 == 0:
            x, s_new, v_rows = _even_mixer(x, pos, w["norm_mix_g"][l], w["ev_w_in"][j], w["ev_w_out"][j],
                                           st["ret"][j], w["ret_gn_g"][j], w["mlp_norm_g"][j], w["mlp_w_s"][j],
                                           w["mlp_b_s"][j], nseg=16)
            out["chunk_v"].append(v_rows.reshape(n_seq, seq_len, G_B, DG_B))
            out["ret"].append(s_new)
        else:
            x_mid, s_new = _hgrn_mixer(x, w["norm_mix_g"][l], w["od_w_in"][j], w["od_w_out"][j], w["hgrn_lb_logits"],
                                       w["hgrn_onorm_g"][j], st["hgrn"][j], c=seq_len, nseq=16, layer=l)
            x, k_new, v_new = _swa_sample(x, x_mid, pos, w["norm_mix_g"][l], w["od_w_in"][j], w["od_w_out"][j],
                                          jnp.transpose(st["swa_k"][j], (0, 2, 3, 1)),
                                          jnp.transpose(st["swa_v"][j], (0, 2, 3, 1)), w["swa_qnorm_g"][j],
                                          w["swa_knorm_g"][j], w["swa_sinks"][j], nseq=16, group=8)
            out["hgrn"].append(s_new)
            out["swa_k"].append(jnp.transpose(k_new, (0, 3, 1, 2)))
            out["swa_v"].append(jnp.transpose(v_new, (0, 3, 1, 2)))
        x = _mem_attend_cached(x, w["norm_mem_g"][l], w["mem_w_q"][l], w["mem_qnorm_g"][l],
                               st["mem_k"].reshape(DEPTH, n_seq, N_MEM * H_X, HD_X),
                               st["mem_v"].reshape(DEPTH, n_seq, N_MEM * H_X, HD_X), w["mem_w_o"][l], layer=l, nseq=8)
        c0 = st["conv"][l].reshape(n_seq * (CONV_W - 1), D_FF)
        x, gt = _ffn(x, w["norm_ffn_g"][l], w["ffn_w_gate"][l], w["ffn_w_up"][l], w["ffn_conv_w"][l],
                     w["ffn_conv_b"][l], w["ffn_w_down"][l], tm=512, ffc=256, tiles_per_seq=1, c0=c0)
        out["conv"].append(gt.reshape(n_seq, CONV_W - 1, D_FF))
    return x, {name: jnp.stack(rows) for name, rows in out.items()}


def kernel(x_prompt, x_sample, state_ret, state_hgrn, cache_swa_k, cache_swa_v, cache_mem_k, cache_mem_v,
           state_ffn_conv, mem_prompt, norm_mix_g, norm_mem_g, norm_ffn_g, ev_w_in, ev_w_out, ret_gn_g,
           mlp_norm_g, mlp_w_s, mlp_b_s, od_w_in, od_w_out, hgrn_lb_logits, hgrn_onorm_g, swa_qnorm_g,
           swa_knorm_g, swa_sinks, mem_w_q, mem_w_kv, mem_qnorm_g, mem_knorm_g, mem_w_o, ffn_w_gate,
           ffn_w_up, ffn_conv_w, ffn_conv_b, ffn_w_down):
    def bf(a):
        stacked = a.astype(BF16)
        return [_Layer(stacked, i) for i in range(a.shape[0])]

    w = dict(norm_mix_g=norm_mix_g, norm_mem_g=norm_mem_g, norm_ffn_g=norm_ffn_g, ev_w_in=bf(ev_w_in),
             ev_w_out=bf(ev_w_out), ret_gn_g=ret_gn_g, mlp_norm_g=mlp_norm_g, mlp_w_s=mlp_w_s, mlp_b_s=mlp_b_s,
             od_w_in=bf(od_w_in), od_w_out=bf(od_w_out), hgrn_lb_logits=hgrn_lb_logits, hgrn_onorm_g=hgrn_onorm_g,
             swa_qnorm_g=swa_qnorm_g, swa_knorm_g=swa_knorm_g, swa_sinks=swa_sinks, mem_w_q=bf(mem_w_q),
             mem_w_kv=bf(mem_w_kv), mem_qnorm_g=mem_qnorm_g, mem_knorm_g=mem_knorm_g, mem_w_o=bf(mem_w_o),
             ffn_w_gate=bf(ffn_w_gate), ffn_w_up=bf(ffn_w_up), ffn_conv_w=ffn_conv_w, ffn_conv_b=ffn_conv_b,
             ffn_w_down=bf(ffn_w_down))
    b, seq, d = x_prompt.shape
    db, dseq, _ = x_sample.shape
    pos_prompt = np.arange(seq)
    pos_sample = PAST_LEN + np.arange(dseq)
    y_p, ns_p = _run_prompt(x_prompt.reshape(b * seq, d), pos_prompt, b, mem_prompt.reshape(b * N_MEM, d), w)
    st = dict(ret=state_ret, hgrn=state_hgrn, swa_k=cache_swa_k, swa_v=cache_swa_v, mem_k=cache_mem_k,
              mem_v=cache_mem_v, conv=state_ffn_conv)
    y_s, ns_s = _run_sample(x_sample.reshape(db * dseq, d), pos_sample, db, st, w)
    return (y_p.reshape(b, seq, d), y_s.reshape(db, dseq, d), ns_p["ret"], ns_p["hgrn"], ns_p["swa_k"], ns_p["swa_v"],
            ns_p["mem_k"], ns_p["mem_v"], ns_p["conv"], ns_s["ret"], ns_s["chunk_v"], ns_s["hgrn"], ns_s["swa_k"],
            ns_s["swa_v"], ns_s["conv"])
```
